```python
import jax, jax.numpy as jnp
from jax import lax
import numpy as np

D_MODEL = 2048
BATCH = 4
SEQ = 4096
DEPTH = 2

GRID_W = 64
CTX_LEN = 256
N_MIXERS = 2
N_HGRN_LAYERS = (DEPTH + 1) // 2
N_MLA_LAYERS = DEPTH // 2

HG_HEADS = 16
HG_DK = 128
HG_DV = D_MODEL // HG_HEADS
HG_KW = HG_HEADS * HG_DK
HG_VW = HG_HEADS * HG_DV
HG_CHUNK = 64

MLA_HEADS = 16
MLA_Q_RANK = 512
MLA_KV_RANK = 512
MLA_NOPE = 128
MLA_ROPE = 64
MLA_V = 128
MLA_SCALE = (MLA_NOPE + MLA_ROPE) ** -0.5
ROPE_BASE = 10000.0
Q_BLOCK = 128

N_EXPERTS = 32
TOP_K = 4
D_EXPERT = 2048
SWIGLU_LIMIT = 7.0
SWIGLU_ALPHA = 1.702
EXPERT_BLOCK = 128

DEEPNORM_ALPHA = (2 * DEPTH) ** 0.25
DEEPNORM_BETA = (8 * DEPTH) ** -0.25
LN_EPS = 1e-5
RMS_EPS = 1e-6
N_MOD = 6

kernel_name = 'hybrid_hgrn2_mla_moe_dit'


def layer_norm(x):
    xf = x.astype(jnp.float32)
    mu = jnp.mean(xf, axis=-1, keepdims=True)
    var = jnp.mean(jnp.square(xf - mu), axis=-1, keepdims=True)
    return (xf - mu) * lax.rsqrt(var + LN_EPS)


def rms_norm(x, w):
    xf = x.astype(jnp.float32)
    return (xf * lax.rsqrt(jnp.mean(xf * xf, axis=-1, keepdims=True) + RMS_EPS) * w).astype(x.dtype)


def modulate(x, shift, scale):
    return (layer_norm(x) * (1.0 + scale) + shift).astype(x.dtype)


def post_norm(x, y, gate, g, b):
    return (layer_norm(DEEPNORM_ALPHA * x + gate * y) * g + b).astype(x.dtype)


def axial_rope_tables(n_tokens):
    rows = n_tokens // GRID_W
    row = jnp.repeat(jnp.arange(rows, dtype=jnp.float32), GRID_W)
    col = jnp.tile(jnp.arange(GRID_W, dtype=jnp.float32), rows)
    n_freq = MLA_ROPE // 4
    freqs = ROPE_BASE ** (-jnp.arange(n_freq, dtype=jnp.float32) / n_freq)
    ang_r = row[:, None] * freqs
    ang_c = col[:, None] * freqs
    return (jnp.cos(ang_r), jnp.sin(ang_r), jnp.cos(ang_c), jnp.sin(ang_c))


def rotate(x, cos, sin):
    n = x.shape[-1] // 2
    x1, x2 = x[..., :n], x[..., n:]
    return jnp.concatenate([x1 * cos - x2 * sin, x1 * sin + x2 * cos], axis=-1)


def axial_rope(x, tabs):
    cos_r, sin_r, cos_c, sin_c = tabs
    half = MLA_ROPE // 2
    out = jnp.concatenate([rotate(x[..., :half], cos_r, sin_r), rotate(x[..., half:], cos_c, sin_c)], axis=-1)
    return out.astype(x.dtype)


def gla_scan(q, k, v, log_f):
    B, L, H, DK = q.shape
    DV = v.shape[-1]
    n = L // HG_CHUNK

    def to_chunks(a):
        return a.astype(jnp.float32).reshape(B, n, HG_CHUNK, H, a.shape[-1]).transpose(1, 0, 3, 2, 4)

    mask = jnp.tril(jnp.ones((HG_CHUNK, HG_CHUNK), dtype=bool))[:, :, None]

    def step(S, inp):
        qc, kc, vc, gc = inp
        b = jnp.cumsum(gc, axis=2)
        inter = jnp.einsum('bhtk,bhkv->bhtv', qc * jnp.exp(b), S)
        diff = jnp.where(mask, b[:, :, :, None, :] - b[:, :, None, :, :], -jnp.inf)
        attn = jnp.einsum('bhtk,bhsk,bhtsk->bhts', qc, kc, jnp.exp(diff))
        intra = jnp.einsum('bhts,bhsv->bhtv', attn, vc)
        b_last = b[:, :, -1:, :]
        S_new = jnp.exp(b_last[:, :, 0, :])[..., None] * S + jnp.einsum('bhsk,bhsv->bhkv', kc * jnp.exp(b_last - b), vc)
        return S_new, inter + intra

    S0 = jnp.zeros((B, H, DK, DV), jnp.float32)
    _, o = lax.scan(step, S0, (to_chunks(q), to_chunks(k), to_chunks(v), to_chunks(log_f)))
    return o.transpose(1, 0, 3, 2, 4).reshape(B, L, H, DV)


def hgrn2_mixer(h_lat, h_ctx, w_in, gnorm_w, w_o, lower_bound, need_ctx):
    B, S, _ = h_lat.shape
    Lc = h_ctx.shape[1]
    L = Lc + S
    p = jnp.concatenate([h_ctx, h_lat], axis=1) @ w_in
    o1, o2, o3, o4 = HG_KW, 2 * HG_KW, 3 * HG_KW, 3 * HG_KW + HG_VW
    q = jax.nn.silu(p[..., :o1]).reshape(B, L, HG_HEADS, HG_DK)
    v = p[..., o3:o4].reshape(B, L, HG_HEADS, HG_DV)
    g = p[..., o4:]

    def gates(f_raw, lb):
        forget = lb + (1.0 - lb) * jax.nn.sigmoid(f_raw.astype(jnp.float32))
        shp = (B, L, HG_HEADS, HG_DK)
        return jnp.log(forget).reshape(shp), (1.0 - forget).reshape(shp)

    def flip_parts(a):
        return jnp.concatenate([jnp.flip(a[:, :Lc], axis=1), jnp.flip(a[:, Lc:], axis=1)], axis=1)

    logf_f, k_f = gates(p[..., o1:o2], lower_bound[0])
    logf_b, k_b = gates(p[..., o2:o3], lower_bound[1])
    o_fwd = gla_scan(q, k_f, v, logf_f)
    o_bwd = flip_parts(gla_scan(flip_parts(q), flip_parts(k_b), flip_parts(v), flip_parts(logf_b)))
    o = rms_norm(o_fwd + o_bwd, gnorm_w.reshape(HG_HEADS, HG_DV)).reshape(B, L, HG_VW)
    o = (o * jax.nn.silu(g.astype(jnp.float32))).astype(h_lat.dtype)
    y_lat = o[:, Lc:] @ w_o
    y_ctx = o[:, :Lc] @ w_o if need_ctx else None
    return y_lat, y_ctx


def mla_attend(qn, qp, kn, kp, v):
    s = jnp.einsum('bqhd,bkhd->bhqk', qn, kn) + jnp.einsum('bqhd,bkd->bhqk', qp, kp)
    p = jax.nn.softmax(s.astype(jnp.float32) * MLA_SCALE, axis=-1).astype(v.dtype)
    return jnp.einsum('bhqk,bkhd->bqhd', p, v)


def mla_mixer(h_lat, h_ctx, w_in, q_norm_w, kv_norm_w, w_uq, w_ukv, w_o, rope_tabs, need_ctx):
    B, S, _ = h_lat.shape
    Lc = h_ctx.shape[1]
    kv_cols = MLA_KV_RANK + MLA_ROPE

    def queries(cq):
        q = (rms_norm(cq, q_norm_w) @ w_uq).reshape(*cq.shape[:2], MLA_HEADS, MLA_NOPE + MLA_ROPE)
        return q[..., :MLA_NOPE], q[..., MLA_NOPE:]

    def keys_values(pkv):
        kv = (rms_norm(pkv[..., :MLA_KV_RANK], kv_norm_w) @ w_ukv).reshape(*pkv.shape[:2], MLA_HEADS, MLA_NOPE + MLA_V)
        return kv[..., :MLA_NOPE], kv[..., MLA_NOPE:], pkv[..., MLA_KV_RANK:]

    p_lat = h_lat @ w_in
    p_ctx = h_ctx @ (w_in if need_ctx else w_in[:, MLA_Q_RANK:])
    qn_l, qp_l = queries(p_lat[..., :MLA_Q_RANK])
    qp_l = axial_rope(qp_l, tuple(t[:, None, :] for t in rope_tabs))
    kn_l, v_l, kp_l = keys_values(p_lat[..., MLA_Q_RANK:])
    kp_l = axial_rope(kp_l, rope_tabs)
    kn_c, v_c, kp_c = keys_values(p_ctx[..., -kv_cols:])
    kn = jnp.concatenate([kn_c, kn_l], axis=1)
    kp = jnp.concatenate([kp_c, kp_l], axis=1)
    v = jnp.concatenate([v_c, v_l], axis=1)
    nblk = S // Q_BLOCK

    def to_blocks(a):
        return a.reshape(B, nblk, Q_BLOCK, *a.shape[2:]).swapaxes(0, 1)

    o_lat = lax.map(lambda qb: mla_attend(qb[0], qb[1], kn, kp, v), (to_blocks(qn_l), to_blocks(qp_l)))
    o_lat = o_lat.swapaxes(0, 1).reshape(B, S, MLA_HEADS * MLA_V)
    y_lat = o_lat @ w_o
    y_ctx = None
    if need_ctx:
        qn_c, qp_c = queries(p_ctx[..., :MLA_Q_RANK])
        o_ctx = mla_attend(qn_c, qp_c, kn_c, kp_c, v_c).reshape(B, Lc, MLA_HEADS * MLA_V)
        y_ctx = o_ctx @ w_o
    return y_lat, y_ctx


def moe_ffn(h, router_w, router_b, w_gu, b_gu, w_dn, b_dn):
    T = h.shape[0]
    n_assign = T * TOP_K
    logits = (h @ router_w + router_b).astype(jnp.float32)
    top_logit, top_e = lax.top_k(logits, TOP_K)
    gate = jax.nn.softmax(top_logit, axis=-1)
    e_flat = top_e.reshape(-1)
    tok_flat = jnp.arange(n_assign, dtype=jnp.int32) // TOP_K
    order = jnp.argsort(e_flat)
    e_sorted = e_flat[order]
    counts = jnp.bincount(e_flat, length=N_EXPERTS)
    padded = (counts + EXPERT_BLOCK - 1) // EXPERT_BLOCK * EXPERT_BLOCK
    start = jnp.cumsum(counts) - counts
    pad_end = jnp.cumsum(padded)
    pad_start = pad_end - padded
    dest = pad_start[e_sorted] + jnp.arange(n_assign, dtype=jnp.int32) - start[e_sorted]
    n_blocks = -(-n_assign // EXPERT_BLOCK) + N_EXPERTS
    n_rows = n_blocks * EXPERT_BLOCK
    row_tok = jnp.zeros((n_rows,), jnp.int32).at[dest].set(tok_flat[order])
    row_gate = jnp.zeros((n_rows,), jnp.float32).at[dest].set(gate.reshape(-1)[order])
    block_e = jnp.minimum(jnp.searchsorted(pad_end, jnp.arange(n_blocks, dtype=jnp.int32) * EXPERT_BLOCK, side='right'), N_EXPERTS - 1)

    def expert_block(args):
        tok, e = args
        gu = h[tok] @ w_gu[e] + b_gu[e]
        x_glu = jnp.minimum(gu[:, 0::2], SWIGLU_LIMIT)
        x_lin = jnp.clip(gu[:, 1::2], -SWIGLU_LIMIT, SWIGLU_LIMIT)
        act = x_glu * jax.nn.sigmoid(SWIGLU_ALPHA * x_glu) * (x_lin + 1.0)
        return act @ w_dn[e] + b_dn[e]

    y = lax.map(expert_block, (row_tok.reshape(n_blocks, EXPERT_BLOCK), block_e))
    y = y.reshape(n_rows, h.shape[-1]) * row_gate[:, None].astype(h.dtype)
    return jax.ops.segment_sum(y, row_tok, num_segments=T)


def setup_inputs(seed: int = 0) -> dict:
    key = jax.random.key(seed)
    ks = jax.random.split(key, 24)
    f32 = jnp.float32
    D = D_MODEL

    def nrm(k, shape, scale):
        return jax.random.normal(k, shape, f32) * scale

    return {
        'x': nrm(ks[0], (BATCH, SEQ, D), 1.0),
        'c': nrm(ks[1], (BATCH, D), 1.0),
        'ctx': nrm(ks[2], (BATCH, CTX_LEN, D), 1.0),
        'c_ctx': nrm(ks[3], (D,), 1.0),
        'ada_w': nrm(ks[4], (DEPTH, D, N_MOD * D), 0.2 * D ** -0.5),
        'ada_b': nrm(ks[5], (DEPTH, N_MOD * D), 0.02),
        'ln_g': 1.0 + nrm(ks[6], (DEPTH, 2, D), 0.02),
        'ln_b': nrm(ks[7], (DEPTH, 2, D), 0.02),
        'hg_w_in': nrm(ks[8], (N_HGRN_LAYERS, D, 3 * HG_KW + 2 * HG_VW), D ** -0.5),
        'hg_gnorm': 1.0 + nrm(ks[9], (N_HGRN_LAYERS, HG_VW), 0.02),
        'hg_lb_logits': nrm(ks[10], (DEPTH + 1, 2, HG_KW), 0.5),
        'hg_w_o': nrm(ks[11], (N_HGRN_LAYERS, HG_VW, D), HG_VW ** -0.5 * DEEPNORM_BETA),
        'mla_w_in': nrm(ks[12], (N_MLA_LAYERS, D, MLA_Q_RANK + MLA_KV_RANK + MLA_ROPE), D ** -0.5),
        'mla_q_norm': 1.0 + nrm(ks[13], (N_MLA_LAYERS, MLA_Q_RANK), 0.02),
        'mla_kv_norm': 1.0 + nrm(ks[14], (N_MLA_LAYERS, MLA_KV_RANK), 0.02),
        'mla_w_uq': nrm(ks[15], (N_MLA_LAYERS, MLA_Q_RANK, MLA_HEADS * (MLA_NOPE + MLA_ROPE)), MLA_Q_RANK ** -0.5),
        'mla_w_ukv': nrm(ks[16], (N_MLA_LAYERS, MLA_KV_RANK, MLA_HEADS * (MLA_NOPE + MLA_V)), MLA_KV_RANK ** -0.5),
        'mla_w_o': nrm(ks[17], (N_MLA_LAYERS, MLA_HEADS * MLA_V, D), (MLA_HEADS * MLA_V) ** -0.5 * DEEPNORM_BETA),
        'router_w': nrm(ks[18], (DEPTH, D, N_EXPERTS), D ** -0.5),
        'router_b': nrm(ks[19], (DEPTH, N_EXPERTS), 0.01),
        'exp_w_gu': nrm(ks[20], (DEPTH, N_EXPERTS, D, 2 * D_EXPERT), D ** -0.5),
        'exp_b_gu': nrm(ks[21], (DEPTH, N_EXPERTS, 2 * D_EXPERT), 0.02),
        'exp_w_dn': nrm(ks[22], (DEPTH, N_EXPERTS, D_EXPERT, D), D_EXPERT ** -0.5 * DEEPNORM_BETA),
        'exp_b_dn': nrm(ks[23], (DEPTH, N_EXPERTS, D), 0.02),
    }


def reference(x, c, ctx, c_ctx, ada_w, ada_b, ln_g, ln_b, hg_w_in, hg_gnorm, hg_lb_logits, hg_w_o,
              mla_w_in, mla_q_norm, mla_kv_norm, mla_w_uq, mla_w_ukv, mla_w_o,
              router_w, router_b, exp_w_gu, exp_b_gu, exp_w_dn, exp_b_dn):
    B, S, D = x.shape
    Lc = ctx.shape[1]
    rope_tabs = axial_rope_tables(S)
    lower_bounds = jnp.cumsum(jax.nn.softmax(hg_lb_logits.astype(jnp.float32), axis=0), axis=0)
    for i in range(DEPTH):
        need_ctx = i < DEPTH - 1
        j = i // N_MIXERS
        mod = (jax.nn.silu(c) @ ada_w[i] + ada_b[i]).reshape(B, 1, N_MOD, D)
        mod_c = (jax.nn.silu(c_ctx) @ ada_w[i] + ada_b[i]).reshape(N_MOD, D)
        h_lat = modulate(x, mod[:, :, 0], mod[:, :, 1])
        h_ctx = modulate(ctx, mod_c[0], mod_c[1])
        if i % N_MIXERS == 0:
            y_lat, y_ctx = hgrn2_mixer(h_lat, h_ctx, hg_w_in[j], hg_gnorm[j], hg_w_o[j], lower_bounds[i], need_ctx)
        else:
            y_lat, y_ctx = mla_mixer(h_lat, h_ctx, mla_w_in[j], mla_q_norm[j], mla_kv_norm[j], mla_w_uq[j],
                                     mla_w_ukv[j], mla_w_o[j], rope_tabs, need_ctx)
        x = post_norm(x, y_lat, mod[:, :, 2], ln_g[i, 0], ln_b[i, 0])
        h_tok = modulate(x, mod[:, :, 3], mod[:, :, 4]).reshape(B * S, D)
        if need_ctx:
            ctx = post_norm(ctx, y_ctx, mod_c[2], ln_g[i, 0], ln_b[i, 0])
            h_tok = jnp.concatenate([h_tok, modulate(ctx, mod_c[3], mod_c[4]).reshape(B * Lc, D)], axis=0)
        y = moe_ffn(h_tok, router_w[i], router_b[i], exp_w_gu[i], exp_b_gu[i], exp_w_dn[i], exp_b_dn[i])
        x = post_norm(x, y[:B * S].reshape(B, S, D), mod[:, :, 5], ln_g[i, 1], ln_b[i, 1])
        if need_ctx:
            ctx = post_norm(ctx, y[B * S:].reshape(B, Lc, D), mod_c[5], ln_g[i, 1], ln_b[i, 1])
    return x
```

```python
import functools

import jax
import jax.numpy as jnp
from jax import lax
from jax.experimental import pallas as pl
from jax.experimental.pallas import tpu as pltpu

F32 = jnp.float32
BF16 = jnp.bfloat16
I32 = jnp.int32

D_MODEL = 2048
DEPTH = 2
GRID_W = 64
N_MOD = 6

HG_HEADS = 16
HG_DK = 128
HG_W = HG_HEADS * HG_DK
HG_SUB = 16

MLA_HEADS = 16
MLA_Q_RANK = 512
MLA_KV_RANK = 512
MLA_NOPE = 128
MLA_ROPE = 64
MLA_V = 128
MLA_QK_PAD = 256
MLA_SCALE = (MLA_NOPE + MLA_ROPE) ** -0.5
ROPE_BASE = 10000.0

N_EXPERTS = 32
TOP_K = 4
D_EXPERT = 2048
SWIGLU_LIMIT = 7.0
SWIGLU_ALPHA = 1.702

DEEPNORM_ALPHA = (2 * DEPTH) ** 0.25
LN_EPS = 1e-5
RMS_EPS = 1e-6

LANE = 128
ROW_TILE = 256
MOE_TM = 256
MOE_TOK = 128
VMEM_LIMIT = 48 * 1024 * 1024

_HIGHEST = lax.Precision.HIGHEST


def _cparams(sem):
    return pltpu.CompilerParams(dimension_semantics=sem, vmem_limit_bytes=VMEM_LIMIT)


def _ln(x):
    mu = jnp.mean(x, axis=-1, keepdims=True)
    xc = x - mu
    var = jnp.mean(xc * xc, axis=-1, keepdims=True)
    return xc * lax.rsqrt(var + LN_EPS)


def _silu(x):
    return x * jax.nn.sigmoid(x)


def _ada_kernel(c_ref, w_ref, b_ref, o_ref):
    s = _silu(c_ref[...])
    o_ref[0] = jnp.dot(s, w_ref[0], precision=_HIGHEST, preferred_element_type=F32) + b_ref[0]


def _ada(cc, ada_w, ada_b):
    n = N_MOD * D_MODEL
    tn = 1024
    return pl.pallas_call(
        _ada_kernel,
        grid=(DEPTH, n // tn),
        in_specs=[
            pl.BlockSpec((8, D_MODEL), lambda i, j: (0, 0)),
            pl.BlockSpec((1, D_MODEL, tn), lambda i, j: (i, 0, j)),
            pl.BlockSpec((1, 1, tn), lambda i, j: (i, 0, j)),
        ],
        out_specs=pl.BlockSpec((1, 8, tn), lambda i, j: (i, 0, j)),
        out_shape=jax.ShapeDtypeStruct((DEPTH, 8, n), F32),
        compiler_params=_cparams(("parallel", "parallel")),
        name="ada",
    )(cc, ada_w, ada_b.reshape(DEPTH, 1, n))


def _lnmod_kernel(x_ref, m_ref, o_ref, *, si):
    m = m_ref[0, 0]
    o_ref[...] = (_ln(x_ref[...]) * (1.0 + m[si + 1:si + 2]) + m[si:si + 1]).astype(o_ref.dtype)


def _ln_mod(x, modt, mod_map, si):
    t = x.shape[0]
    return pl.pallas_call(
        functools.partial(_lnmod_kernel, si=si),
        grid=(t // ROW_TILE,),
        in_specs=[
            pl.BlockSpec((ROW_TILE, D_MODEL), lambda i: (i, 0)),
            pl.BlockSpec((1, 1, 8, D_MODEL), lambda i: mod_map(i) + (0, 0)),
        ],
        out_specs=pl.BlockSpec((ROW_TILE, D_MODEL), lambda i: (i, 0)),
        out_shape=jax.ShapeDtypeStruct((t, D_MODEL), BF16),
        compiler_params=_cparams(("parallel",)),
        name="ln_mod",
    )(x, modt)


def _pn_router_kernel(x_ref, y_ref, m_ref, g_ref, b_ref, rw_ref, rb_ref, xn_ref, h_ref, te_ref, tg_ref):
    m = m_ref[0, 0]
    z = DEEPNORM_ALPHA * x_ref[...] + m[2:3] * y_ref[...].astype(F32)
    xn = _ln(z) * g_ref[...] + b_ref[...]
    xn_ref[...] = xn
    h = _ln(xn) * (1.0 + m[4:5]) + m[3:4]
    h_ref[...] = h
    logits = jnp.dot(h, rw_ref[...], precision=_HIGHEST, preferred_element_type=F32) + rb_ref[...]
    lane = lax.broadcasted_iota(I32, logits.shape, 1).astype(F32)
    neg = jnp.float32(-jnp.inf)
    cur = jnp.where(lane < N_EXPERTS, logits, neg)
    vals, idxs = [], []
    for _ in range(TOP_K):
        mx = jnp.max(cur, axis=-1, keepdims=True)
        ix = jnp.min(jnp.where(cur == mx, lane, float(LANE)), axis=-1, keepdims=True)
        vals.append(mx)
        idxs.append(ix)
        cur = jnp.where(lane == ix, neg, cur)
    es = [jnp.exp(v - vals[0]) for v in vals]
    den = es[0] + es[1] + es[2] + es[3]
    te = jnp.zeros(logits.shape, F32)
    tg = jnp.zeros(logits.shape, F32)
    for k in range(TOP_K):
        te = jnp.where(lane == k, idxs[k], te)
        tg = jnp.where(lane == k, es[k] / den, tg)
    te_ref[...] = te.astype(I32)
    tg_ref[...] = tg


def _pn_router(x, y, modt, ln_g, ln_b, rw, rb, n_tok, x_map, mod_map):
    rwp = jnp.zeros((D_MODEL, LANE), F32).at[:, :N_EXPERTS].set(rw)
    rbp = jnp.zeros((1, LANE), F32).at[0, :N_EXPERTS].set(rb)
    row = pl.BlockSpec((ROW_TILE, D_MODEL), lambda i: (i, 0))
    vec = pl.BlockSpec((1, D_MODEL), lambda i: (0, 0))
    nar = pl.BlockSpec((ROW_TILE, LANE), lambda i: (i, 0))
    return pl.pallas_call(
        _pn_router_kernel,
        grid=(n_tok // ROW_TILE,),
        in_specs=[
            pl.BlockSpec((ROW_TILE, D_MODEL), lambda i: (x_map(i), 0)),
            row,
            pl.BlockSpec((1, 1, 8, D_MODEL), lambda i: mod_map(i) + (0, 0)),
            vec, vec,
            pl.BlockSpec((D_MODEL, LANE), lambda i: (0, 0)),
            pl.BlockSpec((1, LANE), lambda i: (0, 0)),
        ],
        out_specs=[row, row, nar, nar],
        out_shape=[
            jax.ShapeDtypeStruct((n_tok, D_MODEL), F32),
            jax.ShapeDtypeStruct((n_tok, D_MODEL), F32),
            jax.ShapeDtypeStruct((n_tok, LANE), I32),
            jax.ShapeDtypeStruct((n_tok, LANE), F32),
        ],
        compiler_params=_cparams(("parallel",)),
        name="pn_router",
    )(x, y, modt, ln_g.reshape(1, D_MODEL), ln_b.reshape(1, D_MODEL), rwp, rbp)


def _mm_kernel(*refs, n_w, n_x, epilogue):
    a = refs[0][...]
    accs = [jnp.dot(a, w[...], preferred_element_type=F32) for w in refs[1:1 + n_w]]
    extras = [e[...] for e in refs[1 + n_w:1 + n_w + n_x]]
    o_ref = refs[-1]
    o_ref[...] = epilogue(accs, extras).astype(o_ref.dtype)


def _mm(a, ws, n_out, tn, out_dtype, epilogue=None, extras=(), tm=512, w_col0=0, name="mm"):
    m, k = a.shape
    tm = min(tm, m)
    assert m % tm == 0 and n_out % tn == 0 and w_col0 % tn == 0
    if epilogue is None:
        epilogue = lambda accs, ex: accs[0]
    c0 = w_col0 // tn
    in_specs = [pl.BlockSpec((tm, k), lambda j, i: (i, 0))]
    in_specs += [pl.BlockSpec((k, tn), lambda j, i: (0, j + c0)) for _ in ws]
    in_specs += [pl.BlockSpec(bs, im) for _, bs, im in extras]
    return pl.pallas_call(
        functools.partial(_mm_kernel, n_w=len(ws), n_x=len(extras), epilogue=epilogue),
        grid=(n_out // tn, m // tm),
        in_specs=in_specs,
        out_specs=pl.BlockSpec((tm, tn), lambda j, i: (i, j)),
        out_shape=jax.ShapeDtypeStruct((m, n_out), out_dtype),
        compiler_params=_cparams(("parallel", "parallel")),
        name=name,
    )(a, *ws, *[e[0] for e in extras])


def _scan_kernel(*refs, rev, hb, tl, final):
    if final:
        (q_ref, f_ref, v_ref, lb_ref, ones_ref, of_ref, g_ref, gw_ref, o_ref,
         st_ref, kp_ref, bp_ref, vp_ref, osc_ref) = refs
    else:
        q_ref, f_ref, v_ref, lb_ref, ones_ref, o_ref, st_ref, kp_ref, bp_ref, vp_ref = refs
        osc_ref = None
    w = hb * HG_DK
    nblk = tl // HG_SUB

    @pl.when(pl.program_id(2) == 0)
    def _():
        st_ref[...] = jnp.zeros(st_ref.shape, F32)

    lb = lb_ref[...]
    forget = lb + (1.0 - lb) * jax.nn.sigmoid(f_ref[0])
    logf = jnp.log(forget)
    kk = 1.0 - forget
    r16 = lax.broadcasted_iota(I32, (tl, w), 0) & (HG_SUB - 1)
    cf = logf
    cr = logf
    for s in (1, 2, 4, 8):
        cf = cf + jnp.where(r16 >= s, pltpu.roll(cf, s, 0), 0.0)
        cr = cr + jnp.where(r16 < HG_SUB - s, pltpu.roll(cr, tl - s, 0), 0.0)
    tot = cf + cr - logf
    b, rest = (cr, cf - logf) if rev else (cf, cr - logf)

    qf = q_ref[0].astype(F32)
    vb = v_ref[0]
    lo, pad = (0, tl) if rev else (HG_SUB, 0)
    zpad = jnp.zeros((HG_SUB, w), F32)
    for ref, val in ((kp_ref, kk), (bp_ref, b), (vp_ref, vb.astype(F32))):
        ref[pl.ds(pad, HG_SUB), :] = zpad
        ref[pl.ds(lo, tl), :] = val
    ones = ones_ref[...]
    acc = jnp.zeros((tl, w), F32)
    for d in range(HG_SUB):
        src = pl.ds(lo + d if rev else lo - d, tl)
        mask = (r16 < HG_SUB - d) if rev else (r16 >= d)
        e = jnp.where(mask, jnp.exp(b - bp_ref[src, :]), 0.0)
        prod = (qf * kp_ref[src, :] * e).astype(BF16)
        acc = acc + jnp.dot(prod, ones, preferred_element_type=F32) * vp_ref[src, :]

    qs = (qf * jnp.exp(b)).astype(BF16)
    kd = (kk * jnp.exp(rest)).astype(BF16)
    dst = osc_ref if final else o_ref.at[0]
    for j in (range(nblk - 1, -1, -1) if rev else range(nblk)):
        r0 = j * HG_SUB
        gt = jnp.exp(tot[r0:r0 + 1, :])
        for h in range(hb):
            c0 = h * HG_DK
            st = st_ref[h]
            oi = lax.dot_general(qs[r0:r0 + HG_SUB, c0:c0 + HG_DK], st.astype(BF16),
                                 (((1,), (1,)), ((), ())), preferred_element_type=F32)
            upd = lax.dot_general(vb[r0:r0 + HG_SUB, c0:c0 + HG_DK], kd[r0:r0 + HG_SUB, c0:c0 + HG_DK],
                                  (((0,), (0,)), ((), ())), preferred_element_type=F32)
            st_ref[h] = st * gt[:, c0:c0 + HG_DK] + upd
            dst[pl.ds(r0, HG_SUB), pl.ds(c0, HG_DK)] = acc[r0:r0 + HG_SUB, c0:c0 + HG_DK] + oi

    if final:
        o = osc_ref[...] + of_ref[0]
        g = g_ref[0].astype(F32)
        gw = gw_ref[...]
        for h in range(hb):
            c0 = h * HG_DK
            oh = o[:, c0:c0 + HG_DK]
            ms = jnp.mean(oh * oh, axis=-1, keepdims=True)
            o_ref[0, :, pl.ds(c0, HG_DK)] = (
                oh * lax.rsqrt(ms + RMS_EPS) * gw[:, c0:c0 + HG_DK] * g[:, c0:c0 + HG_DK]).astype(o_ref.dtype)


def _hgrn_scan(q, fr, v, lb, rev, n_ctx, o_fwd=None, g=None, gnorm=None, hb=2, tl=256):
    bsz, l, _ = q.shape
    w = hb * HG_DK
    nl = l // tl
    nc = n_ctx // tl
    final = rev

    def lmap(i):
        if not rev:
            return i
        return jnp.where(i < nc, nc - 1 - i, nl - 1 - (i - nc))

    foff = (HG_W // w) if rev else 0
    tile = lambda b, h, i: (b, lmap(i), h)
    ones = (jnp.arange(w)[:, None] // HG_DK == jnp.arange(w)[None, :] // HG_DK).astype(BF16)
    in_specs = [
        pl.BlockSpec((1, tl, w), tile),
        pl.BlockSpec((1, tl, w), lambda b, h, i: (b, lmap(i), h + foff)),
        pl.BlockSpec((1, tl, w), tile),
        pl.BlockSpec((1, w), lambda b, h, i: (0, h)),
        pl.BlockSpec((w, w), lambda b, h, i: (0, 0)),
    ]
    args = [q, fr, v, lb.reshape(1, HG_W), ones]
    scratch = [pltpu.VMEM((hb, HG_DK, HG_DK), F32)] + [pltpu.VMEM((tl + HG_SUB, w), F32)] * 3
    if final:
        in_specs += [pl.BlockSpec((1, tl, w), tile), pl.BlockSpec((1, tl, w), tile),
                     pl.BlockSpec((1, w), lambda b, h, i: (0, h))]
        args += [o_fwd, g, gnorm.reshape(1, HG_W)]
        scratch += [pltpu.VMEM((tl, w), F32)]
    return pl.pallas_call(
        functools.partial(_scan_kernel, rev=rev, hb=hb, tl=tl, final=final),
        grid=(bsz, HG_W // w, nl),
        in_specs=in_specs,
        out_specs=pl.BlockSpec((1, tl, w), tile),
        out_shape=jax.ShapeDtypeStruct((bsz, l, HG_W), BF16 if final else F32),
        scratch_shapes=scratch,
        compiler_params=_cparams(("parallel", "parallel", "arbitrary")),
        name="hgrn_scan_bwd" if rev else "hgrn_scan_fwd",
    )(*args)


def _attn_kernel(q_ref, kn_ref, kp_ref, v_ref, o_ref, k_scr):
    @pl.when(pl.program_id(2) == 0)
    def _():
        k_scr[:, 0:MLA_NOPE] = kn_ref[0]
        k_scr[:, MLA_NOPE:MLA_QK_PAD] = kp_ref[0]

    s = lax.dot_general(q_ref[0], k_scr[...], (((1,), (1,)), ((), ())), preferred_element_type=F32)
    p = jnp.exp(s - jnp.max(s, axis=-1, keepdims=True))
    den = jnp.sum(p, axis=-1, keepdims=True)
    o = jnp.dot(p.astype(BF16), v_ref[0], preferred_element_type=F32)
    o_ref[0] = (o / den).astype(o_ref.dtype)


def _attention(qf, kv, kp, n_ctx, tq=256):
    bsz, l, _ = qf.shape
    s = l - n_ctx
    qoff = n_ctx // tq
    return pl.pallas_call(
        _attn_kernel,
        grid=(bsz, MLA_HEADS, s // tq),
        in_specs=[
            pl.BlockSpec((1, tq, MLA_QK_PAD), lambda b, h, i: (b, i + qoff, h)),
            pl.BlockSpec((1, l, MLA_NOPE), lambda b, h, i: (b, 0, 2 * h)),
            pl.BlockSpec((1, l, LANE), lambda b, h, i: (b, 0, 0)),
            pl.BlockSpec((1, l, MLA_V), lambda b, h, i: (b, 0, 2 * h + 1)),
        ],
        out_specs=pl.BlockSpec((1, tq, MLA_V), lambda b, h, i: (b, i, h)),
        out_shape=jax.ShapeDtypeStruct((bsz, s, MLA_HEADS * MLA_V), BF16),
        scratch_shapes=[pltpu.VMEM((l, MLA_QK_PAD), BF16)],
        compiler_params=_cparams(("parallel", "parallel", "arbitrary")),
        name="mla_attn",
    )(qf, kv, kp, kv)


def _dispatch_kernel(cnt_ref, pst_ref, dest_ref, h_hbm, xs_hbm, sems, psem):
    i = pl.program_id(0)
    n = pl.num_programs(0)
    slot = i % 2

    def row_copy(src, dst, sem):
        return pltpu.make_async_copy(h_hbm.at[pl.ds(src, 1), :], xs_hbm.at[pl.ds(dst, 1), :], sem)

    def issue(t, c):
        for k in range(TOP_K):
            row_copy(i * MOE_TOK + t, dest_ref[0, 0, t * TOP_K + k], sems.at[slot]).start()
        return c

    lax.fori_loop(0, MOE_TOK, issue, 0)

    def drain(s):
        pltpu.make_async_copy(h_hbm.at[pl.ds(0, MOE_TOK * TOP_K), :], xs_hbm.at[pl.ds(0, MOE_TOK * TOP_K), :],
                              sems.at[s]).wait()

    @pl.when(i > 0)
    def _():
        drain(1 - slot)

    @pl.when(i == n - 1)
    def _():
        drain(slot)

    @pl.when(i == 0)
    def _():
        for e in range(N_EXPERTS):
            first = pst_ref[e] + cnt_ref[e]
            last = pst_ref[e + 1]
            lax.fori_loop(first, last, lambda r, c: (row_copy(0, r, psem).start(), c)[1], 0)
            lax.fori_loop(first, last, lambda r, c: (row_copy(0, r, psem).wait(), c)[1], 0)

        def block_copy(m):
            return pltpu.make_async_copy(h_hbm.at[pl.ds(0, MOE_TM), :], xs_hbm.at[pl.ds(m * MOE_TM, MOE_TM), :], psem)

        used = pst_ref[N_EXPERTS] // MOE_TM
        total = xs_hbm.shape[0] // MOE_TM
        lax.fori_loop(used, total, lambda m, c: (block_copy(m).start(), c)[1], 0)
        lax.fori_loop(used, total, lambda m, c: (block_copy(m).wait(), c)[1], 0)


def _dispatch(h, dest, counts, pad_start, n_rows):
    t = h.shape[0]
    nb = t // MOE_TOK
    return pl.pallas_call(
        _dispatch_kernel,
        grid_spec=pltpu.PrefetchScalarGridSpec(
            num_scalar_prefetch=2,
            grid=(nb,),
            in_specs=[
                pl.BlockSpec((1, 1, MOE_TOK * TOP_K), lambda i, c, p: (i, 0, 0), memory_space=pltpu.SMEM),
                pl.BlockSpec(memory_space=pl.ANY),
            ],
            out_specs=pl.BlockSpec(memory_space=pl.ANY),
            scratch_shapes=[pltpu.SemaphoreType.DMA((2,)), pltpu.SemaphoreType.DMA(())],
        ),
        out_shape=jax.ShapeDtypeStruct((n_rows, D_MODEL), F32),
        compiler_params=_cparams(("arbitrary",)),
        name="moe_dispatch",
    )(counts, pad_start, dest.reshape(nb, 1, MOE_TOK * TOP_K), h)


def _gmm1_kernel(be_ref, nu_ref, x_ref, wg_ref, wl_ref, bg_ref, bl_ref, o_ref):
    @pl.when(pl.program_id(1) < nu_ref[0])
    def _():
        x = x_ref[...].astype(BF16)
        g = jnp.dot(x, wg_ref[0], preferred_element_type=F32) + bg_ref[0]
        u = jnp.dot(x, wl_ref[0], preferred_element_type=F32) + bl_ref[0]
        g = jnp.minimum(g, SWIGLU_LIMIT)
        u = jnp.clip(u, -SWIGLU_LIMIT, SWIGLU_LIMIT)
        o_ref[...] = (g * jax.nn.sigmoid(SWIGLU_ALPHA * g) * (u + 1.0)).astype(o_ref.dtype)

    @pl.when(pl.program_id(1) >= nu_ref[0])
    def _():
        o_ref[...] = jnp.zeros(o_ref.shape, o_ref.dtype)


def _gmm2_kernel(be_ref, nu_ref, x_ref, w_ref, b_ref, o_ref):
    @pl.when(pl.program_id(1) < nu_ref[0])
    def _():
        o_ref[...] = jnp.dot(x_ref[...], w_ref[0], preferred_element_type=F32) + b_ref[0]

    @pl.when(pl.program_id(1) >= nu_ref[0])
    def _():
        o_ref[...] = jnp.zeros(o_ref.shape, o_ref.dtype)


def _experts(xs, block_e, n_used, wg, wl, bg, bl, wd, bd, tn=1024):
    n_rows = xs.shape[0]
    nb = n_rows // MOE_TM

    def mrow(n, m, be, nu):
        return (jnp.minimum(m, nu[0] - 1), 0)

    def wmap(n, m, be, nu):
        return (be[jnp.minimum(m, nu[0] - 1)], 0, n)

    hid = pl.pallas_call(
        _gmm1_kernel,
        grid_spec=pltpu.PrefetchScalarGridSpec(
            num_scalar_prefetch=2,
            grid=(D_EXPERT // tn, nb),
            in_specs=[
                pl.BlockSpec((MOE_TM, D_MODEL), mrow),
                pl.BlockSpec((1, D_MODEL, tn), wmap),
                pl.BlockSpec((1, D_MODEL, tn), wmap),
                pl.BlockSpec((1, 1, tn), wmap),
                pl.BlockSpec((1, 1, tn), wmap),
            ],
            out_specs=pl.BlockSpec((MOE_TM, tn), lambda n, m, be, nu: (m, n)),
        ),
        out_shape=jax.ShapeDtypeStruct((n_rows, D_EXPERT), BF16),
        compiler_params=_cparams(("parallel", "arbitrary")),
        name="moe_gate_up",
    )(block_e, n_used, xs, wg, wl, bg, bl)
    return pl.pallas_call(
        _gmm2_kernel,
        grid_spec=pltpu.PrefetchScalarGridSpec(
            num_scalar_prefetch=2,
            grid=(D_MODEL // tn, nb),
            in_specs=[
                pl.BlockSpec((MOE_TM, D_EXPERT), mrow),
                pl.BlockSpec((1, D_EXPERT, tn), wmap),
                pl.BlockSpec((1, 1, tn), wmap),
            ],
            out_specs=pl.BlockSpec((MOE_TM, tn), lambda n, m, be, nu: (m, n)),
        ),
        out_shape=jax.ShapeDtypeStruct((n_rows, D_MODEL), F32),
        compiler_params=_cparams(("parallel", "arbitrary")),
        name="moe_down",
    )(block_e, n_used, hid, wd, bd)


def _combine_kernel(dest_ref, gate_ref, y_hbm, x_ref, m_ref, g_ref, b_ref, o_ref, buf, sem):
    def issue(t, c):
        for k in range(TOP_K):
            pltpu.make_async_copy(y_hbm.at[pl.ds(dest_ref[0, 0, t * TOP_K + k], 1), :],
                                  buf.at[k, pl.ds(t, 1), :], sem).start()
        return c

    lax.fori_loop(0, MOE_TOK, issue, 0)
    for k in range(TOP_K):
        pltpu.make_async_copy(y_hbm.at[pl.ds(0, MOE_TOK), :], buf.at[k], sem).wait()
    gate = gate_ref[...]
    y = gate[:, 0:1] * buf[0]
    for k in range(1, TOP_K):
        y = y + gate[:, k:k + 1] * buf[k]
    m = m_ref[0, 0]
    z = DEEPNORM_ALPHA * x_ref[...] + m[5:6] * y
    o_ref[...] = _ln(z) * g_ref[...] + b_ref[...]


def _combine_pn(y_rows, dest, gate, x, modt, ln_g, ln_b, mod_map):
    t = x.shape[0]
    nb = t // MOE_TOK
    row = pl.BlockSpec((MOE_TOK, D_MODEL), lambda i: (i, 0))
    vec = pl.BlockSpec((1, D_MODEL), lambda i: (0, 0))
    return pl.pallas_call(
        _combine_kernel,
        grid=(nb,),
        in_specs=[
            pl.BlockSpec((1, 1, MOE_TOK * TOP_K), lambda i: (i, 0, 0), memory_space=pltpu.SMEM),
            pl.BlockSpec((MOE_TOK, LANE), lambda i: (i, 0)),
            pl.BlockSpec(memory_space=pl.ANY),
            row,
            pl.BlockSpec((1, 1, 8, D_MODEL), lambda i: mod_map(i) + (0, 0)),
            vec, vec,
        ],
        out_specs=row,
        out_shape=jax.ShapeDtypeStruct((t, D_MODEL), F32),
        scratch_shapes=[pltpu.VMEM((TOP_K, MOE_TOK, D_MODEL), F32), pltpu.SemaphoreType.DMA(())],
        compiler_params=_cparams(("arbitrary",)),
        name="moe_combine",
    )(dest.reshape(nb, 1, MOE_TOK * TOP_K), gate, y_rows, x, modt,
      ln_g.reshape(1, D_MODEL), ln_b.reshape(1, D_MODEL))


def _moe(h, top_e, gate, x, modt, mod_map, ln_g, ln_b, w_gu, b_gu, w_dn, b_dn):
    t = h.shape[0]
    n_assign = t * TOP_K
    nb = n_assign // MOE_TM + N_EXPERTS
    e_flat = top_e[:, :TOP_K].reshape(-1)
    onehot = (e_flat[:, None] == jnp.arange(N_EXPERTS, dtype=I32)[None, :]).astype(I32)
    ranks = jnp.cumsum(onehot, axis=0)
    counts = ranks[-1]
    rank = jnp.sum(ranks * onehot, axis=1) - 1
    padded = (counts + MOE_TM - 1) // MOE_TM * MOE_TM
    pad_end = jnp.cumsum(padded)
    pad_start = jnp.concatenate([jnp.zeros((1,), I32), pad_end]).astype(I32)
    dest = (pad_start[e_flat] + rank).astype(I32).reshape(t, TOP_K)
    block_e = jnp.minimum(
        jnp.searchsorted(pad_end, jnp.arange(nb, dtype=I32) * MOE_TM, side="right"), N_EXPERTS - 1).astype(I32)
    n_used = (pad_end[-1:] // MOE_TM).astype(I32)

    xs = _dispatch(h, dest, counts.astype(I32), pad_start, nb * MOE_TM)
    wg = w_gu[:, :, 0::2].astype(BF16)
    wl = w_gu[:, :, 1::2].astype(BF16)
    bg = b_gu[:, None, 0::2]
    bl = b_gu[:, None, 1::2]
    y_rows = _experts(xs, block_e, n_used, wg, wl, bg, bl, w_dn.astype(BF16), b_dn[:, None, :])
    return _combine_pn(y_rows, dest, gate, x, modt, ln_g, ln_b, mod_map)


def _rope_tables(n_ctx, seq):
    pos = jnp.arange(seq)
    row = (pos // GRID_W).astype(F32)
    col = (pos % GRID_W).astype(F32)
    n_freq = MLA_ROPE // 4
    freqs = ROPE_BASE ** (-jnp.arange(n_freq, dtype=F32) / n_freq)
    ar = row[:, None] * freqs
    ac = col[:, None] * freqs
    cos = jnp.concatenate([jnp.cos(ar), jnp.cos(ar), jnp.cos(ac), jnp.cos(ac)], axis=-1)
    sin = jnp.concatenate([jnp.sin(ar), jnp.sin(ar), jnp.sin(ac), jnp.sin(ac)], axis=-1)
    cos = jnp.concatenate([jnp.ones((n_ctx, MLA_ROPE), F32), cos], axis=0)
    sin = jnp.concatenate([jnp.zeros((n_ctx, MLA_ROPE), F32), sin], axis=0)
    return cos, sin


def _rot_cols(w):
    q = MLA_ROPE // 4
    a, b, c, d = w[..., :q], w[..., q:2 * q], w[..., 2 * q:3 * q], w[..., 3 * q:]
    return jnp.concatenate([-b, a, -d, c], axis=-1)


def _rmsnorm_epilogue(accs, ex):
    x = accs[0]
    return x * lax.rsqrt(jnp.mean(x * x, axis=-1, keepdims=True) + RMS_EPS) * ex[0]


def _hgrn_layer(xs, modt, mod_map, w_in, gnorm, w_o, lbs, bsz, l, n_ctx):
    h = _ln_mod(xs, modt, mod_map, 0)
    wb = w_in.astype(BF16)
    silu_ep = lambda accs, ex: _silu(accs[0])
    q = _mm(h, [wb], HG_W, 1024, BF16, silu_ep, w_col0=0, name="hg_q")
    fr = _mm(h, [wb], 2 * HG_W, 1024, F32, w_col0=HG_W, name="hg_f")
    v = _mm(h, [wb], HG_W, 1024, BF16, w_col0=3 * HG_W, name="hg_v")
    g = _mm(h, [wb], HG_W, 1024, BF16, silu_ep, w_col0=4 * HG_W, name="hg_g")
    r3 = lambda a: a.reshape(bsz, l, a.shape[-1])
    o_f = _hgrn_scan(r3(q), r3(fr), r3(v), lbs[0], False, n_ctx)
    o = _hgrn_scan(r3(q), r3(fr), r3(v), lbs[1], True, n_ctx, o_fwd=o_f, g=r3(g), gnorm=gnorm)
    return _mm(o.reshape(bsz * l, HG_W), [w_o.astype(BF16)], D_MODEL, 1024, BF16, name="hg_o")


def _mla_layer(xs, modt, mod_map, w_in, q_norm, kv_norm, w_uq, w_ukv, w_o, bsz, l, n_ctx):
    h = _ln_mod(xs, modt, mod_map, 0)
    wb = w_in.astype(BF16)
    nl = l // ROW_TILE
    vec = lambda n: ((1, n), lambda j, i: (0, 0))
    cq = _mm(h, [wb], MLA_Q_RANK, 512, BF16, _rmsnorm_epilogue,
             extras=[(q_norm.reshape(1, -1),) + vec(MLA_Q_RANK)], tm=ROW_TILE, w_col0=0, name="mla_cq")
    ckv = _mm(h, [wb], MLA_KV_RANK, 512, BF16, _rmsnorm_epilogue,
              extras=[(kv_norm.reshape(1, -1),) + vec(MLA_KV_RANK)], tm=ROW_TILE, w_col0=MLA_Q_RANK, name="mla_ckv")
    cos, sin = _rope_tables(n_ctx, l - n_ctx)
    rope_ep = lambda accs, ex: accs[0] * ex[0] + accs[1] * ex[1]
    tab = lambda n: ((ROW_TILE, n), lambda j, i: (i % nl, 0))
    w_kp = w_in[:, MLA_Q_RANK + MLA_KV_RANK:]
    zk = jnp.zeros((D_MODEL, LANE - MLA_ROPE), F32)
    zt = jnp.zeros((l, LANE - MLA_ROPE), F32)
    kp = _mm(h, [jnp.concatenate([w_kp, zk], 1).astype(BF16), jnp.concatenate([_rot_cols(w_kp), zk], 1).astype(BF16)],
             LANE, LANE, BF16, rope_ep,
             extras=[(jnp.concatenate([cos, zt], 1),) + tab(LANE), (jnp.concatenate([sin, zt], 1),) + tab(LANE)],
             tm=ROW_TILE, name="mla_kp")
    wq = w_uq.reshape(MLA_Q_RANK, MLA_HEADS, MLA_NOPE + MLA_ROPE)
    zq = jnp.zeros((MLA_Q_RANK, MLA_HEADS, MLA_QK_PAD - MLA_NOPE - MLA_ROPE), F32)
    wqa = jnp.concatenate([wq, zq], -1).reshape(MLA_Q_RANK, -1).astype(BF16)
    wqb = jnp.concatenate([jnp.zeros_like(wq[..., :MLA_NOPE]), _rot_cols(wq[..., MLA_NOPE:]), zq], -1)
    wqb = wqb.reshape(MLA_Q_RANK, -1).astype(BF16)
    zt = jnp.zeros((l, MLA_QK_PAD - MLA_NOPE - MLA_ROPE), F32)
    cq_tab = jnp.concatenate([jnp.ones((l, MLA_NOPE), F32), cos, zt], 1) * MLA_SCALE
    sq_tab = jnp.concatenate([jnp.zeros((l, MLA_NOPE), F32), sin, zt], 1) * MLA_SCALE
    qf = _mm(cq, [wqa, wqb], MLA_HEADS * MLA_QK_PAD, MLA_QK_PAD, BF16, rope_ep,
             extras=[(cq_tab,) + tab(MLA_QK_PAD), (sq_tab,) + tab(MLA_QK_PAD)], tm=ROW_TILE, name="mla_q")
    kv = _mm(ckv, [w_ukv.astype(BF16)], MLA_HEADS * (MLA_NOPE + MLA_V), 1024, BF16, name="mla_kv")
    r3 = lambda a: a.reshape(bsz, l, a.shape[-1])
    o = _attention(r3(qf), r3(kv), r3(kp), n_ctx)
    return _mm(o.reshape(bsz * (l - n_ctx), MLA_HEADS * MLA_V), [w_o.astype(BF16)], D_MODEL, 1024, BF16, name="mla_o")


def kernel(x, c, ctx, c_ctx, ada_w, ada_b, ln_g, ln_b, hg_w_in, hg_gnorm, hg_lb_logits, hg_w_o, mla_w_in,
           mla_q_norm, mla_kv_norm, mla_w_uq, mla_w_ukv, mla_w_o, router_w, router_b, exp_w_gu, exp_b_gu,
           exp_w_dn, exp_b_dn):
    bsz, seq, d = x.shape
    n_ctx = ctx.shape[1]
    l = n_ctx + seq
    nl = l // ROW_TILE
    ns = seq // ROW_TILE
    nc = n_ctx // ROW_TILE

    cc = jnp.zeros((8, d), F32).at[:bsz].set(c).at[bsz].set(c_ctx)
    mods = _ada(cc, ada_w, ada_b).reshape(DEPTH, 8, N_MOD, d)
    pad = jnp.zeros((DEPTH, bsz, 8 - N_MOD, d), F32)
    m_lat = jnp.concatenate([mods[:, :bsz], pad], axis=2)
    m_ctx = jnp.concatenate([jnp.broadcast_to(mods[:, bsz:bsz + 1], (DEPTH, bsz, N_MOD, d)), pad], axis=2)
    modt = jnp.stack([m_ctx, m_lat], axis=2)
    lower = jnp.cumsum(jax.nn.softmax(hg_lb_logits.astype(F32), axis=0), axis=0)

    xs = jnp.concatenate([ctx, x], axis=1).reshape(bsz * l, d)

    def map_all(r):
        per = l // r
        return lambda i: (i // per, ((i % per) >= n_ctx // r).astype(I32))

    def map_lat(r):
        per = seq // r
        return lambda i: (i // per, 1)

    y = _hgrn_layer(xs, modt[0], map_all(ROW_TILE), hg_w_in[0], hg_gnorm[0], hg_w_o[0], lower[0], bsz, l, n_ctx)
    xs, h, te, tg = _pn_router(xs, y, modt[0], ln_g[0, 0], ln_b[0, 0], router_w[0], router_b[0], bsz * l,
                               lambda i: i, map_all(ROW_TILE))
    xs = _moe(h, te, tg, xs, modt[0], map_all(MOE_TOK), ln_g[0, 1], ln_b[0, 1],
              exp_w_gu[0], exp_b_gu[0], exp_w_dn[0], exp_b_dn[0])

    y = _mla_layer(xs, modt[1], map_all(ROW_TILE), mla_w_in[0], mla_q_norm[0], mla_kv_norm[0], mla_w_uq[0],
                   mla_w_ukv[0], mla_w_o[0], bsz, l, n_ctx)
    xl, h, te, tg = _pn_router(xs, y, modt[1], ln_g[1, 0], ln_b[1, 0], router_w[1], router_b[1], bsz * seq,
                               lambda i: (i // ns) * nl + nc + i % ns, map_lat(ROW_TILE))
    out = _moe(h, te, tg, xl, modt[1], map_lat(MOE_TOK), ln_g[1, 1], ln_b[1, 1],
               exp_w_gu[1], exp_b_gu[1], exp_w_dn[1], exp_b_dn[1])
    return out.reshape(bsz, seq, d)
```

```python
import functools

import jax
import jax.numpy as jnp
from jax import lax
from jax.experimental import pallas as pl
from jax.experimental.pallas import tpu as pltpu

F32 = jnp.float32
BF16 = jnp.bfloat16
I32 = jnp.int32

D_MODEL = 2048
DEPTH = 2
GRID_W = 64
N_MOD = 6

HG_HEADS = 16
HG_DK = 128
HG_W = HG_HEADS * HG_DK
HG_SUB = 16

MLA_HEADS = 16
MLA_Q_RANK = 512
MLA_KV_RANK = 512
MLA_NOPE = 128
MLA_ROPE = 64
MLA_V = 128
MLA_QK_PAD = 256
MLA_SCALE = (MLA_NOPE + MLA_ROPE) ** -0.5
ROPE_BASE = 10000.0

N_EXPERTS = 32
TOP_K = 4
D_EXPERT = 2048
SWIGLU_LIMIT = 7.0
SWIGLU_ALPHA = 1.702

DEEPNORM_ALPHA = (2 * DEPTH) ** 0.25
LN_EPS = 1e-5
RMS_EPS = 1e-6

LANE = 128
MXU_N = 256
ROW_TILE = 256
MOE_TM = 256
MOE_TOK = 128
VMEM_LIMIT = 48 * 1024 * 1024

_HIGHEST = lax.Precision.HIGHEST


def _cparams(sem):
    return pltpu.CompilerParams(dimension_semantics=sem, vmem_limit_bytes=VMEM_LIMIT)


def _ln(x):
    mu = jnp.mean(x, axis=-1, keepdims=True)
    xc = x - mu
    var = jnp.mean(xc * xc, axis=-1, keepdims=True)
    return xc * lax.rsqrt(var + LN_EPS)


def _silu(x):
    return x * jax.nn.sigmoid(x)


def _ada_kernel(c_ref, w_ref, b_ref, o_ref):
    s = _silu(c_ref[...])
    o_ref[0] = jnp.dot(s, w_ref[0], precision=_HIGHEST, preferred_element_type=F32) + b_ref[0]


def _ada(cc, ada_w, ada_b):
    n = N_MOD * D_MODEL
    tn = 1024
    return pl.pallas_call(
        _ada_kernel,
        grid=(DEPTH, n // tn),
        in_specs=[
            pl.BlockSpec((8, D_MODEL), lambda i, j: (0, 0)),
            pl.BlockSpec((1, D_MODEL, tn), lambda i, j: (i, 0, j)),
            pl.BlockSpec((1, 1, tn), lambda i, j: (i, 0, j)),
        ],
        out_specs=pl.BlockSpec((1, 8, tn), lambda i, j: (i, 0, j)),
        out_shape=jax.ShapeDtypeStruct((DEPTH, 8, n), F32),
        compiler_params=_cparams(("parallel", "parallel")),
        name="ada",
    )(cc, ada_w, ada_b.reshape(DEPTH, 1, n))


def _lnmod_kernel(x_ref, m_ref, o_ref, *, si):
    m = m_ref[0, 0]
    o_ref[...] = (_ln(x_ref[...]) * (1.0 + m[si + 1:si + 2]) + m[si:si + 1]).astype(o_ref.dtype)


def _ln_mod(x, modt, mod_map, si):
    t = x.shape[0]
    return pl.pallas_call(
        functools.partial(_lnmod_kernel, si=si),
        grid=(t // ROW_TILE,),
        in_specs=[
            pl.BlockSpec((ROW_TILE, D_MODEL), lambda i: (i, 0)),
            pl.BlockSpec((1, 1, 8, D_MODEL), lambda i: mod_map(i) + (0, 0)),
        ],
        out_specs=pl.BlockSpec((ROW_TILE, D_MODEL), lambda i: (i, 0)),
        out_shape=jax.ShapeDtypeStruct((t, D_MODEL), BF16),
        compiler_params=_cparams(("parallel",)),
        name="ln_mod",
    )(x, modt)


def _pn_router_kernel(x_ref, y_ref, m_ref, g_ref, b_ref, rw_ref, rb_ref, xn_ref, h_ref, te_ref, tg_ref):
    m = m_ref[0, 0]
    z = DEEPNORM_ALPHA * x_ref[...] + m[2:3] * y_ref[...].astype(F32)
    xn = _ln(z) * g_ref[...] + b_ref[...]
    xn_ref[...] = xn
    h = _ln(xn) * (1.0 + m[4:5]) + m[3:4]
    h_ref[...] = h
    logits = jnp.dot(h, rw_ref[...], precision=_HIGHEST, preferred_element_type=F32) + rb_ref[...]
    lane = lax.broadcasted_iota(I32, logits.shape, 1).astype(F32)
    neg = jnp.float32(-jnp.inf)
    cur = jnp.where(lane < N_EXPERTS, logits, neg)
    vals, idxs = [], []
    for _ in range(TOP_K):
        mx = jnp.max(cur, axis=-1, keepdims=True)
        ix = jnp.min(jnp.where(cur == mx, lane, float(LANE)), axis=-1, keepdims=True)
        vals.append(mx)
        idxs.append(ix)
        cur = jnp.where(lane == ix, neg, cur)
    es = [jnp.exp(v - vals[0]) for v in vals]
    den = es[0] + es[1] + es[2] + es[3]
    te = jnp.zeros(logits.shape, F32)
    tg = jnp.zeros(logits.shape, F32)
    for k in range(TOP_K):
        te = jnp.where(lane == k, idxs[k], te)
        tg = jnp.where(lane == k, es[k] / den, tg)
    te_ref[...] = te.astype(I32)
    tg_ref[...] = tg


def _pn_router(x, y, modt, ln_g, ln_b, rw, rb, n_tok, x_map, mod_map):
    rwp = jnp.zeros((D_MODEL, LANE), F32).at[:, :N_EXPERTS].set(rw)
    rbp = jnp.zeros((1, LANE), F32).at[0, :N_EXPERTS].set(rb)
    row = pl.BlockSpec((ROW_TILE, D_MODEL), lambda i: (i, 0))
    vec = pl.BlockSpec((1, D_MODEL), lambda i: (0, 0))
    nar = pl.BlockSpec((ROW_TILE, LANE), lambda i: (i, 0))
    return pl.pallas_call(
        _pn_router_kernel,
        grid=(n_tok // ROW_TILE,),
        in_specs=[
            pl.BlockSpec((ROW_TILE, D_MODEL), lambda i: (x_map(i), 0)),
            row,
            pl.BlockSpec((1, 1, 8, D_MODEL), lambda i: mod_map(i) + (0, 0)),
            vec, vec,
            pl.BlockSpec((D_MODEL, LANE), lambda i: (0, 0)),
            pl.BlockSpec((1, LANE), lambda i: (0, 0)),
        ],
        out_specs=[row, row, nar, nar],
        out_shape=[
            jax.ShapeDtypeStruct((n_tok, D_MODEL), F32),
            jax.ShapeDtypeStruct((n_tok, D_MODEL), F32),
            jax.ShapeDtypeStruct((n_tok, LANE), I32),
            jax.ShapeDtypeStruct((n_tok, LANE), F32),
        ],
        compiler_params=_cparams(("parallel",)),
        name="pn_router",
    )(x, y, modt, ln_g.reshape(1, D_MODEL), ln_b.reshape(1, D_MODEL), rwp, rbp)


def _mm_kernel(*refs, n_w, n_x, epilogue):
    a = refs[0][...]
    accs = [jnp.dot(a, w[...], preferred_element_type=F32) for w in refs[1:1 + n_w]]
    extras = [e[...] for e in refs[1 + n_w:1 + n_w + n_x]]
    o_ref = refs[-1]
    o_ref[...] = epilogue(accs, extras).astype(o_ref.dtype)


def _mm(a, ws, n_out, tn, out_dtype, epilogue=None, extras=(), tm=512, w_col0=0, name="mm"):
    m, k = a.shape
    tm = min(tm, m)
    assert m % tm == 0 and n_out % tn == 0 and w_col0 % tn == 0
    if epilogue is None:
        epilogue = lambda accs, ex: accs[0]
    c0 = w_col0 // tn
    in_specs = [pl.BlockSpec((tm, k), lambda j, i: (i, 0))]
    in_specs += [pl.BlockSpec((k, tn), lambda j, i: (0, j + c0)) for _ in ws]
    in_specs += [pl.BlockSpec(bs, im) for _, bs, im in extras]
    return pl.pallas_call(
        functools.partial(_mm_kernel, n_w=len(ws), n_x=len(extras), epilogue=epilogue),
        grid=(n_out // tn, m // tm),
        in_specs=in_specs,
        out_specs=pl.BlockSpec((tm, tn), lambda j, i: (i, j)),
        out_shape=jax.ShapeDtypeStruct((m, n_out), out_dtype),
        compiler_params=_cparams(("parallel", "parallel")),
        name=name,
    )(a, *ws, *[e[0] for e in extras])


def _scan_kernel(*refs, rev, hb, tl, final):
    if final:
        (q_ref, f_ref, v_ref, lb_ref, ones_ref, of_ref, g_ref, gw_ref, o_ref,
         st_ref, kp_ref, bp_ref, vp_ref, osc_ref) = refs
    else:
        q_ref, f_ref, v_ref, lb_ref, ones_ref, o_ref, st_ref, kp_ref, bp_ref, vp_ref = refs
        osc_ref = None
    w = hb * HG_DK
    nblk = tl // HG_SUB

    @pl.when(pl.program_id(2) == 0)
    def _():
        st_ref[...] = jnp.zeros(st_ref.shape, F32)

    lb = lb_ref[...]
    forget = lb + (1.0 - lb) * jax.nn.sigmoid(f_ref[0])
    logf = jnp.log(forget)
    kk = 1.0 - forget
    r16 = lax.broadcasted_iota(I32, (tl, w), 0) & (HG_SUB - 1)
    cf = logf
    cr = logf
    for s in (1, 2, 4, 8):
        cf = cf + jnp.where(r16 >= s, pltpu.roll(cf, s, 0), 0.0)
        cr = cr + jnp.where(r16 < HG_SUB - s, pltpu.roll(cr, tl - s, 0), 0.0)
    tot = cf + cr - logf
    b, rest = (cr, cf - logf) if rev else (cf, cr - logf)

    qf = q_ref[0].astype(F32)
    vb = v_ref[0]
    lo, pad = (0, tl) if rev else (HG_SUB, 0)
    zpad = jnp.zeros((HG_SUB, w), F32)
    for ref, val in ((kp_ref, kk), (bp_ref, b), (vp_ref, vb.astype(F32))):
        ref[pl.ds(pad, HG_SUB), :] = zpad
        ref[pl.ds(lo, tl), :] = val
    ones = ones_ref[...]
    acc = jnp.zeros((tl, w), F32)
    for d in range(HG_SUB):
        src = pl.ds(lo + d if rev else lo - d, tl)
        mask = (r16 < HG_SUB - d) if rev else (r16 >= d)
        e = jnp.where(mask, jnp.exp(b - bp_ref[src, :]), 0.0)
        prod = (qf * kp_ref[src, :] * e).astype(BF16)
        acc = acc + jnp.dot(prod, ones, preferred_element_type=F32) * vp_ref[src, :]

    qs = (qf * jnp.exp(b)).astype(BF16)
    kd = (kk * jnp.exp(rest)).astype(BF16)
    dst = osc_ref if final else o_ref.at[0]
    for j in (range(nblk - 1, -1, -1) if rev else range(nblk)):
        r0 = j * HG_SUB
        gt = jnp.exp(tot[r0:r0 + 1, :])
        for h in range(hb):
            c0 = h * HG_DK
            st = st_ref[h]
            oi = lax.dot_general(qs[r0:r0 + HG_SUB, c0:c0 + HG_DK], st.astype(BF16),
                                 (((1,), (1,)), ((), ())), preferred_element_type=F32)
            upd = lax.dot_general(vb[r0:r0 + HG_SUB, c0:c0 + HG_DK], kd[r0:r0 + HG_SUB, c0:c0 + HG_DK],
                                  (((0,), (0,)), ((), ())), preferred_element_type=F32)
            st_ref[h] = st * gt[:, c0:c0 + HG_DK] + upd
            dst[pl.ds(r0, HG_SUB), pl.ds(c0, HG_DK)] = acc[r0:r0 + HG_SUB, c0:c0 + HG_DK] + oi

    if final:
        o = osc_ref[...] + of_ref[0]
        g = g_ref[0].astype(F32)
        gw = gw_ref[...]
        for h in range(hb):
            c0 = h * HG_DK
            oh = o[:, c0:c0 + HG_DK]
            ms = jnp.mean(oh * oh, axis=-1, keepdims=True)
            o_ref[0, :, pl.ds(c0, HG_DK)] = (
                oh * lax.rsqrt(ms + RMS_EPS) * gw[:, c0:c0 + HG_DK] * g[:, c0:c0 + HG_DK]).astype(o_ref.dtype)


def _hgrn_scan(q, fr, v, lb, rev, n_ctx, o_fwd=None, g=None, gnorm=None, hb=2, tl=256):
    bsz, l, _ = q.shape
    w = hb * HG_DK
    nl = l // tl
    nc = n_ctx // tl
    final = rev

    def lmap(i):
        if not rev:
            return i
        return jnp.where(i < nc, nc - 1 - i, nl - 1 - (i - nc))

    foff = (HG_W // w) if rev else 0
    tile = lambda b, h, i: (b, lmap(i), h)
    ones = (jnp.arange(w)[:, None] // HG_DK == jnp.arange(w)[None, :] // HG_DK).astype(BF16)
    in_specs = [
        pl.BlockSpec((1, tl, w), tile),
        pl.BlockSpec((1, tl, w), lambda b, h, i: (b, lmap(i), h + foff)),
        pl.BlockSpec((1, tl, w), tile),
        pl.BlockSpec((1, w), lambda b, h, i: (0, h)),
        pl.BlockSpec((w, w), lambda b, h, i: (0, 0)),
    ]
    args = [q, fr, v, lb.reshape(1, HG_W), ones]
    scratch = [pltpu.VMEM((hb, HG_DK, HG_DK), F32)] + [pltpu.VMEM((tl + HG_SUB, w), F32)] * 3
    if final:
        in_specs += [pl.BlockSpec((1, tl, w), tile), pl.BlockSpec((1, tl, w), tile),
                     pl.BlockSpec((1, w), lambda b, h, i: (0, h))]
        args += [o_fwd, g, gnorm.reshape(1, HG_W)]
        scratch += [pltpu.VMEM((tl, w), F32)]
    return pl.pallas_call(
        functools.partial(_scan_kernel, rev=rev, hb=hb, tl=tl, final=final),
        grid=(bsz, HG_W // w, nl),
        in_specs=in_specs,
        out_specs=pl.BlockSpec((1, tl, w), tile),
        out_shape=jax.ShapeDtypeStruct((bsz, l, HG_W), BF16 if final else F32),
        scratch_shapes=scratch,
        compiler_params=_cparams(("parallel", "parallel", "arbitrary")),
        name="hgrn_scan_bwd" if rev else "hgrn_scan_fwd",
    )(*args)


def _attn_kernel(q_ref, kn_ref, kp_ref, v_ref, o_ref, k_scr):
    @pl.when(pl.program_id(2) == 0)
    def _():
        k_scr[:, 0:MLA_NOPE] = kn_ref[0]
        k_scr[:, MLA_NOPE:MLA_QK_PAD] = kp_ref[0]

    s = lax.dot_general(q_ref[0], k_scr[...], (((1,), (1,)), ((), ())), preferred_element_type=F32)
    p = jnp.exp(s - jnp.max(s, axis=-1, keepdims=True))
    den = jnp.sum(p, axis=-1, keepdims=True)
    o = jnp.dot(p.astype(BF16), v_ref[0], preferred_element_type=F32)
    o_ref[0] = (o / den).astype(o_ref.dtype)


def _attention(qf, kv, kp, n_ctx, tq=256):
    bsz, l, _ = qf.shape
    s = l - n_ctx
    qoff = n_ctx // tq
    return pl.pallas_call(
        _attn_kernel,
        grid=(bsz, MLA_HEADS, s // tq),
        in_specs=[
            pl.BlockSpec((1, tq, MLA_QK_PAD), lambda b, h, i: (b, i + qoff, h)),
            pl.BlockSpec((1, l, MLA_NOPE), lambda b, h, i: (b, 0, 2 * h)),
            pl.BlockSpec((1, l, LANE), lambda b, h, i: (b, 0, 0)),
            pl.BlockSpec((1, l, MLA_V), lambda b, h, i: (b, 0, 2 * h + 1)),
        ],
        out_specs=pl.BlockSpec((1, tq, MLA_V), lambda b, h, i: (b, i, h)),
        out_shape=jax.ShapeDtypeStruct((bsz, s, MLA_HEADS * MLA_V), BF16),
        scratch_shapes=[pltpu.VMEM((l, MLA_QK_PAD), BF16)],
        compiler_params=_cparams(("parallel", "parallel", "arbitrary")),
        name="mla_attn",
    )(qf, kv, kp, kv)


def _dispatch_kernel(cnt_ref, pst_ref, dest_ref, h_ref, xs_hbm, sem, psem):
    i = pl.program_id(0)

    def row_copy(t, dst, s):
        return pltpu.make_async_copy(h_ref.at[pl.ds(t, 1), :], xs_hbm.at[pl.ds(dst, 1), :], s)

    def issue(t, c):
        for k in range(TOP_K):
            row_copy(t, dest_ref[0, 0, t * TOP_K + k], sem).start()
        return c

    lax.fori_loop(0, MOE_TOK, issue, 0)

    @pl.when(i == 0)
    def _():
        for e in range(N_EXPERTS):
            first = pst_ref[e] + cnt_ref[e]
            last = pst_ref[e + 1]
            lax.fori_loop(first, last, lambda r, c: (row_copy(0, r, psem).start(), c)[1], 0)
            lax.fori_loop(first, last, lambda r, c: (row_copy(0, r, psem).wait(), c)[1], 0)

        def block_copy(m):
            return pltpu.make_async_copy(h_ref, xs_hbm.at[pl.ds(m * MOE_TOK, MOE_TOK), :], psem)

        used = pst_ref[N_EXPERTS] // MOE_TOK
        total = xs_hbm.shape[0] // MOE_TOK
        lax.fori_loop(used, total, lambda m, c: (block_copy(m).start(), c)[1], 0)
        lax.fori_loop(used, total, lambda m, c: (block_copy(m).wait(), c)[1], 0)

    for _ in range(TOP_K):
        pltpu.make_async_copy(h_ref, xs_hbm.at[pl.ds(0, MOE_TOK), :], sem).wait()


def _dispatch(h, dest, counts, pad_start, n_rows):
    t = h.shape[0]
    nb = t // MOE_TOK
    return pl.pallas_call(
        _dispatch_kernel,
        grid_spec=pltpu.PrefetchScalarGridSpec(
            num_scalar_prefetch=2,
            grid=(nb,),
            in_specs=[
                pl.BlockSpec((1, 1, MOE_TOK * TOP_K), lambda i, c, p: (i, 0, 0), memory_space=pltpu.SMEM),
                pl.BlockSpec((MOE_TOK, D_MODEL), lambda i, c, p: (i, 0)),
            ],
            out_specs=pl.BlockSpec(memory_space=pl.ANY),
            scratch_shapes=[pltpu.SemaphoreType.DMA(()), pltpu.SemaphoreType.DMA(())],
        ),
        out_shape=jax.ShapeDtypeStruct((n_rows, D_MODEL), F32),
        compiler_params=_cparams(("arbitrary",)),
        name="moe_dispatch",
    )(counts, pad_start, dest.reshape(nb, 1, MOE_TOK * TOP_K), h)


def _expert_switch(be_ref, nu_ref):
    m = pl.program_id(1)
    live = m < nu_ref[0]
    fresh = jnp.logical_or(m == 0, be_ref[m] != be_ref[jnp.maximum(m - 1, 0)])
    return live, jnp.logical_and(live, fresh)


def _gmm1_kernel(be_ref, nu_ref, x_ref, w_ref, p_ref, bg_ref, bl_ref, o_ref, wg_scr, wl_scr):
    live, fresh = _expert_switch(be_ref, nu_ref)
    half = MXU_N // 2

    @pl.when(fresh)
    def _():
        for c in range(w_ref.shape[2] // MXU_N):
            blk = w_ref[0, :, c * MXU_N:(c + 1) * MXU_N].astype(BF16)
            sp = jnp.dot(blk, p_ref[...], preferred_element_type=F32)
            wg_scr[:, c * half:(c + 1) * half] = sp[:, :half].astype(BF16)
            wl_scr[:, c * half:(c + 1) * half] = sp[:, half:].astype(BF16)

    @pl.when(live)
    def _():
        x = x_ref[...].astype(BF16)
        g = jnp.dot(x, wg_scr[...], preferred_element_type=F32) + bg_ref[0]
        u = jnp.dot(x, wl_scr[...], preferred_element_type=F32) + bl_ref[0]
        g = jnp.minimum(g, SWIGLU_LIMIT)
        u = jnp.clip(u, -SWIGLU_LIMIT, SWIGLU_LIMIT)
        o_ref[...] = (g * jax.nn.sigmoid(SWIGLU_ALPHA * g) * (u + 1.0)).astype(o_ref.dtype)

    @pl.when(jnp.logical_not(live))
    def _():
        o_ref[...] = jnp.zeros(o_ref.shape, o_ref.dtype)


def _gmm2_kernel(be_ref, nu_ref, x_ref, w_ref, b_ref, o_ref, w_scr):
    live, fresh = _expert_switch(be_ref, nu_ref)

    @pl.when(fresh)
    def _():
        w_scr[...] = w_ref[0].astype(BF16)

    @pl.when(live)
    def _():
        o_ref[...] = jnp.dot(x_ref[...], w_scr[...], preferred_element_type=F32) + b_ref[0]

    @pl.when(jnp.logical_not(live))
    def _():
        o_ref[...] = jnp.zeros(o_ref.shape, o_ref.dtype)


def _experts(xs, block_e, n_used, w_gu, bg, bl, w_dn, bd, tn=512, tn2=1024):
    n_rows = xs.shape[0]
    nb = n_rows // MOE_TM
    idx = jnp.arange(MXU_N)
    perm = (idx[:, None] == jnp.where(idx < MXU_N // 2, 2 * idx, 2 * (idx - MXU_N // 2) + 1)[None, :]).astype(BF16)

    def mrow(n, m, be, nu):
        return (jnp.minimum(m, nu[0] - 1), 0)

    def wmap(n, m, be, nu):
        return (be[jnp.minimum(m, nu[0] - 1)], 0, n)

    hid = pl.pallas_call(
        _gmm1_kernel,
        grid_spec=pltpu.PrefetchScalarGridSpec(
            num_scalar_prefetch=2,
            grid=(D_EXPERT // tn, nb),
            in_specs=[
                pl.BlockSpec((MOE_TM, D_MODEL), mrow),
                pl.BlockSpec((1, D_MODEL, 2 * tn), wmap),
                pl.BlockSpec((MXU_N, MXU_N), lambda n, m, be, nu: (0, 0)),
                pl.BlockSpec((1, 1, tn), wmap),
                pl.BlockSpec((1, 1, tn), wmap),
            ],
            out_specs=pl.BlockSpec((MOE_TM, tn), lambda n, m, be, nu: (m, n)),
            scratch_shapes=[pltpu.VMEM((D_MODEL, tn), BF16), pltpu.VMEM((D_MODEL, tn), BF16)],
        ),
        out_shape=jax.ShapeDtypeStruct((n_rows, D_EXPERT), BF16),
        compiler_params=_cparams(("parallel", "arbitrary")),
        name="moe_gate_up",
    )(block_e, n_used, xs, w_gu, perm, bg, bl)
    return pl.pallas_call(
        _gmm2_kernel,
        grid_spec=pltpu.PrefetchScalarGridSpec(
            num_scalar_prefetch=2,
            grid=(D_MODEL // tn2, nb),
            in_specs=[
                pl.BlockSpec((MOE_TM, D_EXPERT), mrow),
                pl.BlockSpec((1, D_EXPERT, tn2), wmap),
                pl.BlockSpec((1, 1, tn2), wmap),
            ],
            out_specs=pl.BlockSpec((MOE_TM, tn2), lambda n, m, be, nu: (m, n)),
            scratch_shapes=[pltpu.VMEM((D_EXPERT, tn2), BF16)],
        ),
        out_shape=jax.ShapeDtypeStruct((n_rows, D_MODEL), F32),
        compiler_params=_cparams(("parallel", "arbitrary")),
        name="moe_down",
    )(block_e, n_used, hid, w_dn, bd)


def _combine_kernel(dest_ref, gate_ref, y_hbm, x_ref, m_ref, g_ref, b_ref, o_ref, buf, sem):
    def issue(t, c):
        for k in range(TOP_K):
            pltpu.make_async_copy(y_hbm.at[pl.ds(dest_ref[0, 0, t * TOP_K + k], 1), :],
                                  buf.at[k, pl.ds(t, 1), :], sem).start()
        return c

    lax.fori_loop(0, MOE_TOK, issue, 0)
    for k in range(TOP_K):
        pltpu.make_async_copy(y_hbm.at[pl.ds(0, MOE_TOK), :], buf.at[k], sem).wait()
    gate = gate_ref[...]
    y = gate[:, 0:1] * buf[0]
    for k in range(1, TOP_K):
        y = y + gate[:, k:k + 1] * buf[k]
    m = m_ref[0, 0]
    z = DEEPNORM_ALPHA * x_ref[...] + m[5:6] * y
    o_ref[...] = _ln(z) * g_ref[...] + b_ref[...]


def _combine_pn(y_rows, dest, gate, x, modt, ln_g, ln_b, mod_map):
    t = x.shape[0]
    nb = t // MOE_TOK
    row = pl.BlockSpec((MOE_TOK, D_MODEL), lambda i: (i, 0))
    vec = pl.BlockSpec((1, D_MODEL), lambda i: (0, 0))
    return pl.pallas_call(
        _combine_kernel,
        grid=(nb,),
        in_specs=[
            pl.BlockSpec((1, 1, MOE_TOK * TOP_K), lambda i: (i, 0, 0), memory_space=pltpu.SMEM),
            pl.BlockSpec((MOE_TOK, LANE), lambda i: (i, 0)),
            pl.BlockSpec(memory_space=pl.ANY),
            row,
            pl.BlockSpec((1, 1, 8, D_MODEL), lambda i: mod_map(i) + (0, 0)),
            vec, vec,
        ],
        out_specs=row,
        out_shape=jax.ShapeDtypeStruct((t, D_MODEL), F32),
        scratch_shapes=[pltpu.VMEM((TOP_K, MOE_TOK, D_MODEL), F32), pltpu.SemaphoreType.DMA(())],
        compiler_params=_cparams(("arbitrary",)),
        name="moe_combine",
    )(dest.reshape(nb, 1, MOE_TOK * TOP_K), gate, y_rows, x, modt,
      ln_g.reshape(1, D_MODEL), ln_b.reshape(1, D_MODEL))


def _moe(h, top_e, gate, x, modt, mod_map, ln_g, ln_b, w_gu, b_gu, w_dn, b_dn):
    t = h.shape[0]
    n_assign = t * TOP_K
    nb = n_assign // MOE_TM + N_EXPERTS
    e_flat = top_e[:, :TOP_K].reshape(-1)
    onehot = (e_flat[:, None] == jnp.arange(N_EXPERTS, dtype=I32)[None, :]).astype(I32)
    ranks = jnp.cumsum(onehot, axis=0)
    counts = ranks[-1]
    rank = jnp.sum(ranks * onehot, axis=1) - 1
    padded = (counts + MOE_TM - 1) // MOE_TM * MOE_TM
    pad_end = jnp.cumsum(padded)
    pad_start = jnp.concatenate([jnp.zeros((1,), I32), pad_end]).astype(I32)
    dest = (pad_start[e_flat] + rank).astype(I32).reshape(t, TOP_K)
    block_e = jnp.minimum(
        jnp.searchsorted(pad_end, jnp.arange(nb, dtype=I32) * MOE_TM, side="right"), N_EXPERTS - 1).astype(I32)
    n_used = (pad_end[-1:] // MOE_TM).astype(I32)

    xs = _dispatch(h, dest, counts.astype(I32), pad_start, nb * MOE_TM)
    y_rows = _experts(xs, block_e, n_used, w_gu, b_gu[:, None, 0::2], b_gu[:, None, 1::2], w_dn, b_dn[:, None, :])
    return _combine_pn(y_rows, dest, gate, x, modt, ln_g, ln_b, mod_map)


def _rope_tables(n_ctx, seq):
    pos = jnp.arange(seq)
    row = (pos // GRID_W).astype(F32)
    col = (pos % GRID_W).astype(F32)
    n_freq = MLA_ROPE // 4
    freqs = ROPE_BASE ** (-jnp.arange(n_freq, dtype=F32) / n_freq)
    ar = row[:, None] * freqs
    ac = col[:, None] * freqs
    cos = jnp.concatenate([jnp.cos(ar), jnp.cos(ar), jnp.cos(ac), jnp.cos(ac)], axis=-1)
    sin = jnp.concatenate([jnp.sin(ar), jnp.sin(ar), jnp.sin(ac), jnp.sin(ac)], axis=-1)
    cos = jnp.concatenate([jnp.ones((n_ctx, MLA_ROPE), F32), cos], axis=0)
    sin = jnp.concatenate([jnp.zeros((n_ctx, MLA_ROPE), F32), sin], axis=0)
    return cos, sin


def _rot_cols(w):
    q = MLA_ROPE // 4
    a, b, c, d = w[..., :q], w[..., q:2 * q], w[..., 2 * q:3 * q], w[..., 3 * q:]
    return jnp.concatenate([-b, a, -d, c], axis=-1)


def _rmsnorm_epilogue(accs, ex):
    x = accs[0]
    return x * lax.rsqrt(jnp.mean(x * x, axis=-1, keepdims=True) + RMS_EPS) * ex[0]


def _hgrn_layer(xs, modt, mod_map, w_in, gnorm, w_o, lbs, bsz, l, n_ctx):
    h = _ln_mod(xs, modt, mod_map, 0)
    wb = w_in.astype(BF16)
    silu_ep = lambda accs, ex: _silu(accs[0])
    q = _mm(h, [wb], HG_W, 1024, BF16, silu_ep, w_col0=0, name="hg_q")
    fr = _mm(h, [wb], 2 * HG_W, 1024, F32, w_col0=HG_W, name="hg_f")
    v = _mm(h, [wb], HG_W, 1024, BF16, w_col0=3 * HG_W, name="hg_v")
    g = _mm(h, [wb], HG_W, 1024, BF16, silu_ep, w_col0=4 * HG_W, name="hg_g")
    r3 = lambda a: a.reshape(bsz, l, a.shape[-1])
    o_f = _hgrn_scan(r3(q), r3(fr), r3(v), lbs[0], False, n_ctx)
    o = _hgrn_scan(r3(q), r3(fr), r3(v), lbs[1], True, n_ctx, o_fwd=o_f, g=r3(g), gnorm=gnorm)
    return _mm(o.reshape(bsz * l, HG_W), [w_o.astype(BF16)], D_MODEL, 1024, BF16, name="hg_o")


def _mla_layer(xs, modt, mod_map, w_in, q_norm, kv_norm, w_uq, w_ukv, w_o, bsz, l, n_ctx):
    h = _ln_mod(xs, modt, mod_map, 0)
    wb = w_in.astype(BF16)
    nl = l // ROW_TILE
    vec = lambda n: ((1, n), lambda j, i: (0, 0))
    cq = _mm(h, [wb], MLA_Q_RANK, 512, BF16, _rmsnorm_epilogue,
             extras=[(q_norm.reshape(1, -1),) + vec(MLA_Q_RANK)], tm=ROW_TILE, w_col0=0, name="mla_cq")
    ckv = _mm(h, [wb], MLA_KV_RANK, 512, BF16, _rmsnorm_epilogue,
              extras=[(kv_norm.reshape(1, -1),) + vec(MLA_KV_RANK)], tm=ROW_TILE, w_col0=MLA_Q_RANK, name="mla_ckv")
    cos, sin = _rope_tables(n_ctx, l - n_ctx)
    rope_ep = lambda accs, ex: accs[0] * ex[0] + accs[1] * ex[1]
    tab = lambda n: ((ROW_TILE, n), lambda j, i: (i % nl, 0))
    w_kp = w_in[:, MLA_Q_RANK + MLA_KV_RANK:]
    zk = jnp.zeros((D_MODEL, LANE - MLA_ROPE), F32)
    zt = jnp.zeros((l, LANE - MLA_ROPE), F32)
    kp = _mm(h, [jnp.concatenate([w_kp, zk], 1).astype(BF16), jnp.concatenate([_rot_cols(w_kp), zk], 1).astype(BF16)],
             LANE, LANE, BF16, rope_ep,
             extras=[(jnp.concatenate([cos, zt], 1),) + tab(LANE), (jnp.concatenate([sin, zt], 1),) + tab(LANE)],
             tm=ROW_TILE, name="mla_kp")
    wq = w_uq.reshape(MLA_Q_RANK, MLA_HEADS, MLA_NOPE + MLA_ROPE)
    zq = jnp.zeros((MLA_Q_RANK, MLA_HEADS, MLA_QK_PAD - MLA_NOPE - MLA_ROPE), F32)
    wqa = jnp.concatenate([wq, zq], -1).reshape(MLA_Q_RANK, -1).astype(BF16)
    wqb = jnp.concatenate([jnp.zeros_like(wq[..., :MLA_NOPE]), _rot_cols(wq[..., MLA_NOPE:]), zq], -1)
    wqb = wqb.reshape(MLA_Q_RANK, -1).astype(BF16)
    zt = jnp.zeros((l, MLA_QK_PAD - MLA_NOPE - MLA_ROPE), F32)
    cq_tab = jnp.concatenate([jnp.ones((l, MLA_NOPE), F32), cos, zt], 1) * MLA_SCALE
    sq_tab = jnp.concatenate([jnp.zeros((l, MLA_NOPE), F32), sin, zt], 1) * MLA_SCALE
    hq = 8
    qf = _mm(cq, [wqa, wqb], MLA_HEADS * MLA_QK_PAD, hq * MLA_QK_PAD, BF16, rope_ep,
             extras=[(jnp.tile(cq_tab, (1, hq)),) + tab(hq * MLA_QK_PAD),
                     (jnp.tile(sq_tab, (1, hq)),) + tab(hq * MLA_QK_PAD)], tm=ROW_TILE, name="mla_q")
    kv = _mm(ckv, [w_ukv.astype(BF16)], MLA_HEADS * (MLA_NOPE + MLA_V), 1024, BF16, name="mla_kv")
    r3 = lambda a: a.reshape(bsz, l, a.shape[-1])
    o = _attention(r3(qf), r3(kv), r3(kp), n_ctx)
    return _mm(o.reshape(bsz * (l - n_ctx), MLA_HEADS * MLA_V), [w_o.astype(BF16)], D_MODEL, 1024, BF16, name="mla_o")


def kernel(x, c, ctx, c_ctx, ada_w, ada_b, ln_g, ln_b, hg_w_in, hg_gnorm, hg_lb_logits, hg_w_o, mla_w_in,
           mla_q_norm, mla_kv_norm, mla_w_uq, mla_w_ukv, mla_w_o, router_w, router_b, exp_w_gu, exp_b_gu,
           exp_w_dn, exp_b_dn):
    bsz, seq, d = x.shape
    n_ctx = ctx.shape[1]
    l = n_ctx + seq
    nl = l // ROW_TILE
    ns = seq // ROW_TILE
    nc = n_ctx // ROW_TILE

    cc = jnp.zeros((8, d), F32).at[:bsz].set(c).at[bsz].set(c_ctx)
    mods = _ada(cc, ada_w, ada_b).reshape(DEPTH, 8, N_MOD, d)
    pad = jnp.zeros((DEPTH, bsz, 8 - N_MOD, d), F32)
    m_lat = jnp.concatenate([mods[:, :bsz], pad], axis=2)
    m_ctx = jnp.concatenate([jnp.broadcast_to(mods[:, bsz:bsz + 1], (DEPTH, bsz, N_MOD, d)), pad], axis=2)
    modt = jnp.stack([m_ctx, m_lat], axis=2)
    lower = jnp.cumsum(jax.nn.softmax(hg_lb_logits.astype(F32), axis=0), axis=0)

    xs = jnp.concatenate([ctx, x], axis=1).reshape(bsz * l, d)

    def map_all(r):
        per = l // r
        return lambda i: (i // per, ((i % per) >= n_ctx // r).astype(I32))

    def map_lat(r):
        per = seq // r
        return lambda i: (i // per, 1)

    y = _hgrn_layer(xs, modt[0], map_all(ROW_TILE), hg_w_in[0], hg_gnorm[0], hg_w_o[0], lower[0], bsz, l, n_ctx)
    xs, h, te, tg = _pn_router(xs, y, modt[0], ln_g[0, 0], ln_b[0, 0], router_w[0], router_b[0], bsz * l,
                               lambda i: i, map_all(ROW_TILE))
    xs = _moe(h, te, tg, xs, modt[0], map_all(MOE_TOK), ln_g[0, 1], ln_b[0, 1],
              exp_w_gu[0], exp_b_gu[0], exp_w_dn[0], exp_b_dn[0])

    y = _mla_layer(xs, modt[1], map_all(ROW_TILE), mla_w_in[0], mla_q_norm[0], mla_kv_norm[0], mla_w_uq[0],
                   mla_w_ukv[0], mla_w_o[0], bsz, l, n_ctx)
    xl, h, te, tg = _pn_router(xs, y, modt[1], ln_g[1, 0], ln_b[1, 0], router_w[1], router_b[1], bsz * seq,
                               lambda i: (i // ns) * nl + nc + i % ns, map_lat(ROW_TILE))
    out = _moe(h, te, tg, xl, modt[1], map_lat(MOE_TOK), ln_g[1, 1], ln_b[1, 1],
               exp_w_gu[1], exp_b_gu[1], exp_w_dn[1], exp_b_dn[1])
    return out.reshape(bsz, seq, d)
```

```python
import functools

import jax
import jax.numpy as jnp
from jax import lax
from jax.experimental import pallas as pl
from jax.experimental.pallas import tpu as pltpu

F32 = jnp.float32
BF16 = jnp.bfloat16
I32 = jnp.int32

D_MODEL = 2048
DEPTH = 2
GRID_W = 64
N_MOD = 6

HG_HEADS = 16
HG_DK = 128
HG_W = HG_HEADS * HG_DK
HG_SUB = 16

MLA_HEADS = 16
MLA_Q_RANK = 512
MLA_KV_RANK = 512
MLA_NOPE = 128
MLA_ROPE = 64
MLA_V = 128
MLA_QK_PAD = 256
MLA_SCALE = (MLA_NOPE + MLA_ROPE) ** -0.5
ROPE_BASE = 10000.0
LOG2_E = 1.4426950408889634

N_EXPERTS = 32
TOP_K = 4
D_EXPERT = 2048
SWIGLU_LIMIT = 7.0
SWIGLU_ALPHA = 1.702

DEEPNORM_ALPHA = (2 * DEPTH) ** 0.25
LN_EPS = 1e-5
RMS_EPS = 1e-6

LANE = 128
MXU_N = 256
ROW_TILE = 256
MOE_TM = 256
MOE_TOK = 128
ATTN_CHUNKS = 4
VMEM_LIMIT = 48 * 1024 * 1024

_HIGHEST = lax.Precision.HIGHEST


def _cparams(sem):
    return pltpu.CompilerParams(dimension_semantics=sem, vmem_limit_bytes=VMEM_LIMIT)


def _ln(x):
    mu = jnp.mean(x, axis=-1, keepdims=True)
    xc = x - mu
    var = jnp.mean(xc * xc, axis=-1, keepdims=True)
    return xc * lax.rsqrt(var + LN_EPS)


def _silu(x):
    return x * jax.nn.sigmoid(x)


def _ada_kernel(c_ref, w_ref, b_ref, o_ref):
    s = _silu(c_ref[...])
    o_ref[0] = jnp.dot(s, w_ref[0], precision=_HIGHEST, preferred_element_type=F32) + b_ref[0]


def _ada(cc, ada_w, ada_b):
    n = N_MOD * D_MODEL
    tn = 1024
    return pl.pallas_call(
        _ada_kernel,
        grid=(DEPTH, n // tn),
        in_specs=[
            pl.BlockSpec((8, D_MODEL), lambda i, j: (0, 0)),
            pl.BlockSpec((1, D_MODEL, tn), lambda i, j: (i, 0, j)),
            pl.BlockSpec((1, 1, tn), lambda i, j: (i, 0, j)),
        ],
        out_specs=pl.BlockSpec((1, 8, tn), lambda i, j: (i, 0, j)),
        out_shape=jax.ShapeDtypeStruct((DEPTH, 8, n), F32),
        compiler_params=_cparams(("parallel", "parallel")),
        name="ada",
    )(cc, ada_w, ada_b.reshape(DEPTH, 1, n))


def _lnmod_kernel(x_ref, m_ref, o_ref, *, si):
    m = m_ref[0, 0]
    o_ref[...] = (_ln(x_ref[...]) * (1.0 + m[si + 1:si + 2]) + m[si:si + 1]).astype(o_ref.dtype)


def _ln_mod(x, modt, mod_map, si):
    t = x.shape[0]
    return pl.pallas_call(
        functools.partial(_lnmod_kernel, si=si),
        grid=(t // ROW_TILE,),
        in_specs=[
            pl.BlockSpec((ROW_TILE, D_MODEL), lambda i: (i, 0)),
            pl.BlockSpec((1, 1, 8, D_MODEL), lambda i: mod_map(i) + (0, 0)),
        ],
        out_specs=pl.BlockSpec((ROW_TILE, D_MODEL), lambda i: (i, 0)),
        out_shape=jax.ShapeDtypeStruct((t, D_MODEL), BF16),
        compiler_params=_cparams(("parallel",)),
        name="ln_mod",
    )(x, modt)


def _pn_router_kernel(x_ref, y_ref, m_ref, g_ref, b_ref, rw_ref, rb_ref, xn_ref, h_ref, te_ref, tg_ref):
    m = m_ref[0, 0]
    z = DEEPNORM_ALPHA * x_ref[...] + m[2:3] * y_ref[...].astype(F32)
    xn = _ln(z) * g_ref[...] + b_ref[...]
    xn_ref[...] = xn
    h = _ln(xn) * (1.0 + m[4:5]) + m[3:4]
    h_ref[...] = h
    logits = jnp.dot(h, rw_ref[...], precision=_HIGHEST, preferred_element_type=F32) + rb_ref[...]
    lane = lax.broadcasted_iota(I32, logits.shape, 1).astype(F32)
    neg = jnp.float32(-jnp.inf)
    cur = jnp.where(lane < N_EXPERTS, logits, neg)
    vals, idxs = [], []
    for _ in range(TOP_K):
        mx = jnp.max(cur, axis=-1, keepdims=True)
        ix = jnp.min(jnp.where(cur == mx, lane, float(LANE)), axis=-1, keepdims=True)
        vals.append(mx)
        idxs.append(ix)
        cur = jnp.where(lane == ix, neg, cur)
    es = [jnp.exp(v - vals[0]) for v in vals]
    den = es[0] + es[1] + es[2] + es[3]
    te = jnp.zeros(logits.shape, F32)
    tg = jnp.zeros(logits.shape, F32)
    for k in range(TOP_K):
        te = jnp.where(lane == k, idxs[k], te)
        tg = jnp.where(lane == k, es[k] / den, tg)
    te_ref[...] = te.astype(I32)
    tg_ref[...] = tg


def _pn_router(x, y, modt, ln_g, ln_b, rw, rb, n_tok, x_map, mod_map):
    rwp = jnp.zeros((D_MODEL, LANE), F32).at[:, :N_EXPERTS].set(rw)
    rbp = jnp.zeros((1, LANE), F32).at[0, :N_EXPERTS].set(rb)
    row = pl.BlockSpec((ROW_TILE, D_MODEL), lambda i: (i, 0))
    vec = pl.BlockSpec((1, D_MODEL), lambda i: (0, 0))
    nar = pl.BlockSpec((ROW_TILE, LANE), lambda i: (i, 0))
    return pl.pallas_call(
        _pn_router_kernel,
        grid=(n_tok // ROW_TILE,),
        in_specs=[
            pl.BlockSpec((ROW_TILE, D_MODEL), lambda i: (x_map(i), 0)),
            row,
            pl.BlockSpec((1, 1, 8, D_MODEL), lambda i: mod_map(i) + (0, 0)),
            vec, vec,
            pl.BlockSpec((D_MODEL, LANE), lambda i: (0, 0)),
            pl.BlockSpec((1, LANE), lambda i: (0, 0)),
        ],
        out_specs=[row, row, nar, nar],
        out_shape=[
            jax.ShapeDtypeStruct((n_tok, D_MODEL), F32),
            jax.ShapeDtypeStruct((n_tok, D_MODEL), F32),
            jax.ShapeDtypeStruct((n_tok, LANE), I32),
            jax.ShapeDtypeStruct((n_tok, LANE), F32),
        ],
        compiler_params=_cparams(("parallel",)),
        name="pn_router",
    )(x, y, modt, ln_g.reshape(1, D_MODEL), ln_b.reshape(1, D_MODEL), rwp, rbp)


def _mm_kernel(*refs, n_w, n_x, epilogue):
    a = refs[0][...]
    accs = [jnp.dot(a, w[...], preferred_element_type=F32) for w in refs[1:1 + n_w]]
    extras = [e[...] for e in refs[1 + n_w:1 + n_w + n_x]]
    o_ref = refs[-1]
    o_ref[...] = epilogue(accs, extras).astype(o_ref.dtype)


def _mm(a, ws, n_out, tn, out_dtype, epilogue=None, extras=(), tm=512, w_col0=0, name="mm"):
    m, k = a.shape
    tm = min(tm, m)
    assert m % tm == 0 and n_out % tn == 0 and w_col0 % tn == 0
    if epilogue is None:
        epilogue = lambda accs, ex: accs[0]
    c0 = w_col0 // tn
    in_specs = [pl.BlockSpec((tm, k), lambda j, i: (i, 0))]
    in_specs += [pl.BlockSpec((k, tn), lambda j, i: (0, j + c0)) for _ in ws]
    in_specs += [pl.BlockSpec(bs, im) for _, bs, im in extras]
    return pl.pallas_call(
        functools.partial(_mm_kernel, n_w=len(ws), n_x=len(extras), epilogue=epilogue),
        grid=(n_out // tn, m // tm),
        in_specs=in_specs,
        out_specs=pl.BlockSpec((tm, tn), lambda j, i: (i, j)),
        out_shape=jax.ShapeDtypeStruct((m, n_out), out_dtype),
        compiler_params=_cparams(("parallel", "parallel")),
        name=name,
    )(a, *ws, *[e[0] for e in extras])


def _scan_kernel(*refs, rev, hb, tl, final):
    n_in = 8 if final else 5
    q_ref, f_ref, v_ref, lb_ref, ones_ref = refs[:5]
    o_ref = refs[n_in]
    st_ref, nat_ref, dil_ref, p_ref, upd_ref, sb_ref = refs[n_in + 1:]
    nblk = tl // HG_SUB
    q_nat, v_nat, f_nat, qs_nat, kd_nat, o_nat = (nat_ref.at[n] for n in range(6))
    dq, dk, dv, dl, dcf, dcr = (dil_ref.at[n] for n in range(6))
    db, dother = (dcr, dcf) if rev else (dcf, dcr)
    rows = lambda i: pl.ds(i, nblk, stride=HG_SUB)
    lanes = lambda h: slice(h * HG_DK, (h + 1) * HG_DK)
    pairs = [list(range(i, HG_SUB)) if rev else list(range(i + 1)) for i in range(HG_SUB)]
    bases = [sum(len(p) for p in pairs[:i]) * nblk for i in range(HG_SUB)]

    @pl.when(pl.program_id(2) == 0)
    def _():
        st_ref[...] = jnp.zeros(st_ref.shape, F32)

    vb = v_ref[0]
    tots = []
    for h in range(hb):
        lb = lb_ref[:, lanes(h)]
        q_nat[h] = q_ref[0, :, lanes(h)].astype(F32)
        v_nat[h] = vb[:, lanes(h)].astype(F32)
        f_nat[h] = f_ref[0, :, lanes(h)]
        for i in range(HG_SUB):
            forget = lb + (1.0 - lb) * jax.nn.sigmoid(f_nat[h, rows(i), :])
            dl[h, i] = jnp.log(forget)
            dk[h, i] = 1.0 - forget
            dq[h, i] = q_nat[h, rows(i), :]
            dv[h, i] = v_nat[h, rows(i), :]
        run = dl[h, 0]
        dcf[h, 0] = run
        for i in range(1, HG_SUB):
            run = run + dl[h, i]
            dcf[h, i] = run
        tots.append(run)
        run = dl[h, HG_SUB - 1]
        dcr[h, HG_SUB - 1] = run
        for i in range(HG_SUB - 2, -1, -1):
            run = run + dl[h, i]
            dcr[h, i] = run
        for i in range(HG_SUB):
            nxt = i - 1 if rev else i + 1
            rest = dother[h, nxt] if 0 <= nxt < HG_SUB else jnp.zeros((nblk, HG_DK), F32)
            qs_nat[h, rows(i), :] = dq[h, i] * jnp.exp(db[h, i])
            kd_nat[h, rows(i), :] = dk[h, i] * jnp.exp(rest)
        for i in range(HG_SUB):
            qi = dq[h, i]
            bi = db[h, i]
            for n, j in enumerate(pairs[i]):
                p_ref[pl.ds(bases[i] + n * nblk, nblk), lanes(h)] = (
                    qi * dk[h, j] * jnp.exp(bi - db[h, j])).astype(BF16)
    ones = ones_ref[...]
    for i in range(HG_SUB):
        rs = jnp.dot(p_ref[pl.ds(bases[i], len(pairs[i]) * nblk), :], ones, preferred_element_type=F32)
        for h in range(hb):
            oi = rs[0:nblk, lanes(h)] * dv[h, pairs[i][0]]
            for n in range(1, len(pairs[i])):
                oi = oi + rs[n * nblk:(n + 1) * nblk, lanes(h)] * dv[h, pairs[i][n]]
            o_nat[h, rows(i), :] = oi

    blk = lambda j: pl.ds(j * HG_SUB, HG_SUB)
    for j in range(nblk):
        for h in range(hb):
            upd_ref[j, h] = lax.dot_general(vb[j * HG_SUB:(j + 1) * HG_SUB, lanes(h)],
                                            kd_nat[h, blk(j), :].astype(BF16),
                                            (((0,), (0,)), ((), ())), preferred_element_type=F32)
    for h in range(hb):
        st = st_ref[h]
        for j in (range(nblk - 1, -1, -1) if rev else range(nblk)):
            sb_ref[j, h] = st.astype(BF16)
            st = st * jnp.exp(tots[h][j:j + 1, :]) + upd_ref[j, h]
        st_ref[h] = st
    for j in range(nblk):
        for h in range(hb):
            oi = lax.dot_general(qs_nat[h, blk(j), :].astype(BF16), sb_ref[j, h],
                                 (((1,), (1,)), ((), ())), preferred_element_type=F32)
            if final:
                o_nat[h, blk(j), :] = o_nat[h, blk(j), :] + oi
            else:
                o_ref[0, blk(j), lanes(h)] = o_nat[h, blk(j), :] + oi

    if final:
        of_ref, g_ref, gw_ref = refs[5:8]
        for h in range(hb):
            oh = o_nat[h] + of_ref[0, :, lanes(h)]
            ms = jnp.mean(oh * oh, axis=-1, keepdims=True)
            o_ref[0, :, lanes(h)] = (oh * lax.rsqrt(ms + RMS_EPS) * gw_ref[:, lanes(h)]
                                     * g_ref[0, :, lanes(h)].astype(F32)).astype(o_ref.dtype)


def _hgrn_scan(q, fr, v, lb, rev, n_ctx, o_fwd=None, g=None, gnorm=None, hb=2, tl=256):
    bsz, l, _ = q.shape
    w = hb * HG_DK
    nl = l // tl
    nc = n_ctx // tl
    final = rev

    def lmap(i):
        if not rev:
            return i
        return jnp.where(i < nc, nc - 1 - i, nl - 1 - (i - nc))

    foff = (HG_W // w) if rev else 0
    tile = lambda b, h, i: (b, lmap(i), h)
    ones = (jnp.arange(w)[:, None] // HG_DK == jnp.arange(w)[None, :] // HG_DK).astype(BF16)
    in_specs = [
        pl.BlockSpec((1, tl, w), tile),
        pl.BlockSpec((1, tl, w), lambda b, h, i: (b, lmap(i), h + foff)),
        pl.BlockSpec((1, tl, w), tile),
        pl.BlockSpec((1, w), lambda b, h, i: (0, h)),
        pl.BlockSpec((w, w), lambda b, h, i: (0, 0)),
    ]
    args = [q, fr, v, lb.reshape(1, HG_W), ones]
    nblk = tl // HG_SUB
    n_pairs = HG_SUB * (HG_SUB + 1) // 2
    scratch = [pltpu.VMEM((hb, HG_DK, HG_DK), F32), pltpu.VMEM((6, hb, tl, HG_DK), F32),
               pltpu.VMEM((6, hb, HG_SUB, nblk, HG_DK), F32), pltpu.VMEM((n_pairs * nblk, w), BF16),
               pltpu.VMEM((nblk, hb, HG_DK, HG_DK), F32), pltpu.VMEM((nblk, hb, HG_DK, HG_DK), BF16)]
    if final:
        in_specs += [pl.BlockSpec((1, tl, w), tile), pl.BlockSpec((1, tl, w), tile),
                     pl.BlockSpec((1, w), lambda b, h, i: (0, h))]
        args += [o_fwd, g, gnorm.reshape(1, HG_W)]
    return pl.pallas_call(
        functools.partial(_scan_kernel, rev=rev, hb=hb, tl=tl, final=final),
        grid=(bsz, HG_W // w, nl),
        in_specs=in_specs,
        out_specs=pl.BlockSpec((1, tl, w), tile),
        out_shape=jax.ShapeDtypeStruct((bsz, l, HG_W), BF16 if final else F32),
        scratch_shapes=scratch,
        compiler_params=_cparams(("parallel", "parallel", "arbitrary")),
        name="hgrn_scan_bwd" if rev else "hgrn_scan_fwd",
    )(*args)


def _attn_kernel(q_ref, kn_ref, kp_ref, v_ref, o_ref, k_scr, v_scr):
    @pl.when(pl.program_id(2) == 0)
    def _():
        k_scr[:, 0:MLA_NOPE] = kn_ref[0]
        k_scr[:, MLA_NOPE:MLA_QK_PAD] = kp_ref[0]
        v_scr[:, 0:MLA_V] = v_ref[0]
        v_scr[:, MLA_V:] = jnp.ones((v_scr.shape[0], v_scr.shape[1] - MLA_V), v_scr.dtype)

    q = q_ref[0]
    l = k_scr.shape[0]
    step = -(-l // (ATTN_CHUNKS * MXU_N)) * MXU_N
    m = acc = None
    for c0 in range(0, l, step):
        rows = pl.ds(c0, min(step, l - c0))
        s = lax.dot_general(q, k_scr[rows, :], (((1,), (1,)), ((), ())), preferred_element_type=F32)
        mc = jnp.max(s, axis=-1, keepdims=True)
        m_new = mc if m is None else jnp.maximum(m, mc)
        pv = jnp.dot(jnp.exp2(s - m_new).astype(BF16), v_scr[rows, :], preferred_element_type=F32)
        acc = pv if m is None else acc * jnp.exp2(m - m_new) + pv
        m = m_new
    o_ref[0] = (acc[:, :MLA_V] / acc[:, MLA_V:MLA_V + 1]).astype(o_ref.dtype)


def _attention(qf, kv, kp, n_ctx, tq=256):
    bsz, l, _ = qf.shape
    s = l - n_ctx
    qoff = n_ctx // tq
    return pl.pallas_call(
        _attn_kernel,
        grid=(bsz, MLA_HEADS, s // tq),
        in_specs=[
            pl.BlockSpec((1, tq, MLA_QK_PAD), lambda b, h, i: (b, i + qoff, h)),
            pl.BlockSpec((1, l, MLA_NOPE), lambda b, h, i: (b, 0, 2 * h)),
            pl.BlockSpec((1, l, LANE), lambda b, h, i: (b, 0, 0)),
            pl.BlockSpec((1, l, MLA_V), lambda b, h, i: (b, 0, 2 * h + 1)),
        ],
        out_specs=pl.BlockSpec((1, tq, MLA_V), lambda b, h, i: (b, i, h)),
        out_shape=jax.ShapeDtypeStruct((bsz, s, MLA_HEADS * MLA_V), BF16),
        scratch_shapes=[pltpu.VMEM((l, MLA_QK_PAD), BF16), pltpu.VMEM((l, MXU_N), BF16)],
        compiler_params=_cparams(("parallel", "parallel", "arbitrary")),
        name="mla_attn",
    )(qf, kv, kp, kv)


def _dispatch_kernel(cnt_ref, pst_ref, dest_ref, h_ref, xs_hbm, sem, psem):
    i = pl.program_id(0)

    def row_copy(t, dst, s):
        return pltpu.make_async_copy(h_ref.at[pl.ds(t, 1), :], xs_hbm.at[pl.ds(dst, 1), :], s)

    def issue(t, c):
        for k in range(TOP_K):
            row_copy(t, dest_ref[0, 0, t * TOP_K + k], sem).start()
        return c

    lax.fori_loop(0, MOE_TOK, issue, 0)

    @pl.when(i == 0)
    def _():
        for e in range(N_EXPERTS):
            first = pst_ref[e] + cnt_ref[e]
            last = pst_ref[e + 1]
            lax.fori_loop(first, last, lambda r, c: (row_copy(0, r, psem).start(), c)[1], 0)
            lax.fori_loop(first, last, lambda r, c: (row_copy(0, r, psem).wait(), c)[1], 0)

        def block_copy(m):
            return pltpu.make_async_copy(h_ref, xs_hbm.at[pl.ds(m * MOE_TOK, MOE_TOK), :], psem)

        used = pst_ref[N_EXPERTS] // MOE_TOK
        total = xs_hbm.shape[0] // MOE_TOK
        lax.fori_loop(used, total, lambda m, c: (block_copy(m).start(), c)[1], 0)
        lax.fori_loop(used, total, lambda m, c: (block_copy(m).wait(), c)[1], 0)

    for _ in range(TOP_K):
        pltpu.make_async_copy(h_ref, xs_hbm.at[pl.ds(0, MOE_TOK), :], sem).wait()


def _dispatch(h, dest, counts, pad_start, n_rows):
    t = h.shape[0]
    nb = t // MOE_TOK
    return pl.pallas_call(
        _dispatch_kernel,
        grid_spec=pltpu.PrefetchScalarGridSpec(
            num_scalar_prefetch=2,
            grid=(nb,),
            in_specs=[
                pl.BlockSpec((1, 1, MOE_TOK * TOP_K), lambda i, c, p: (i, 0, 0), memory_space=pltpu.SMEM),
                pl.BlockSpec((MOE_TOK, D_MODEL), lambda i, c, p: (i, 0)),
            ],
            out_specs=pl.BlockSpec(memory_space=pl.ANY),
            scratch_shapes=[pltpu.SemaphoreType.DMA(()), pltpu.SemaphoreType.DMA(())],
        ),
        out_shape=jax.ShapeDtypeStruct((n_rows, D_MODEL), F32),
        compiler_params=_cparams(("arbitrary",)),
        name="moe_dispatch",
    )(counts, pad_start, dest.reshape(nb, 1, MOE_TOK * TOP_K), h)


def _expert_switch(be_ref, nu_ref):
    m = pl.program_id(1)
    live = m < nu_ref[0]
    fresh = jnp.logical_or(m == 0, be_ref[m] != be_ref[jnp.maximum(m - 1, 0)])
    return live, jnp.logical_and(live, fresh)


def _gmm1_kernel(be_ref, nu_ref, x_ref, w_ref, p_ref, bg_ref, bl_ref, o_ref, wg_scr, wl_scr):
    live, fresh = _expert_switch(be_ref, nu_ref)
    half = MXU_N // 2

    @pl.when(fresh)
    def _():
        for c in range(w_ref.shape[3] // MXU_N):
            blk = w_ref[0, 0, :, c * MXU_N:(c + 1) * MXU_N].astype(BF16)
            sp = jnp.dot(blk, p_ref[...], preferred_element_type=F32)
            wg_scr[:, c * half:(c + 1) * half] = sp[:, :half].astype(BF16)
            wl_scr[:, c * half:(c + 1) * half] = sp[:, half:].astype(BF16)

    @pl.when(live)
    def _():
        x = x_ref[...].astype(BF16)
        g = jnp.dot(x, wg_scr[...], preferred_element_type=F32) + bg_ref[0]
        u = jnp.dot(x, wl_scr[...], preferred_element_type=F32) + bl_ref[0]
        g = jnp.minimum(g, SWIGLU_LIMIT)
        u = jnp.clip(u, -SWIGLU_LIMIT, SWIGLU_LIMIT)
        o_ref[...] = (g * jax.nn.sigmoid(SWIGLU_ALPHA * g) * (u + 1.0)).astype(o_ref.dtype)

    @pl.when(jnp.logical_not(live))
    def _():
        o_ref[...] = jnp.zeros(o_ref.shape, o_ref.dtype)


def _gmm2_kernel(be_ref, nu_ref, x_ref, w_ref, b_ref, o_ref, w_scr):
    live, fresh = _expert_switch(be_ref, nu_ref)

    @pl.when(fresh)
    def _():
        w_scr[...] = w_ref[0, 0].astype(BF16)

    @pl.when(live)
    def _():
        o_ref[...] = jnp.dot(x_ref[...], w_scr[...], preferred_element_type=F32) + b_ref[0]

    @pl.when(jnp.logical_not(live))
    def _():
        o_ref[...] = jnp.zeros(o_ref.shape, o_ref.dtype)


def _experts(xs, block_e, n_used, layer, w_gu, bg, bl, w_dn, bd, tn=512, tn2=1024):
    n_rows = xs.shape[0]
    nb = n_rows // MOE_TM
    idx = jnp.arange(MXU_N)
    perm = (idx[:, None] == jnp.where(idx < MXU_N // 2, 2 * idx, 2 * (idx - MXU_N // 2) + 1)[None, :]).astype(BF16)

    def mrow(n, m, be, nu):
        return (jnp.minimum(m, nu[0] - 1), 0)

    def bmap(n, m, be, nu):
        return (be[jnp.minimum(m, nu[0] - 1)], 0, n)

    def wmap(n, m, be, nu):
        return (layer, be[jnp.minimum(m, nu[0] - 1)], 0, n)

    hid = pl.pallas_call(
        _gmm1_kernel,
        grid_spec=pltpu.PrefetchScalarGridSpec(
            num_scalar_prefetch=2,
            grid=(D_EXPERT // tn, nb),
            in_specs=[
                pl.BlockSpec((MOE_TM, D_MODEL), mrow),
                pl.BlockSpec((1, 1, D_MODEL, 2 * tn), wmap),
                pl.BlockSpec((MXU_N, MXU_N), lambda n, m, be, nu: (0, 0)),
                pl.BlockSpec((1, 1, tn), bmap),
                pl.BlockSpec((1, 1, tn), bmap),
            ],
            out_specs=pl.BlockSpec((MOE_TM, tn), lambda n, m, be, nu: (m, n)),
            scratch_shapes=[pltpu.VMEM((D_MODEL, tn), BF16), pltpu.VMEM((D_MODEL, tn), BF16)],
        ),
        out_shape=jax.ShapeDtypeStruct((n_rows, D_EXPERT), BF16),
        compiler_params=_cparams(("parallel", "arbitrary")),
        name="moe_gate_up",
    )(block_e, n_used, xs, w_gu, perm, bg, bl)
    return pl.pallas_call(
        _gmm2_kernel,
        grid_spec=pltpu.PrefetchScalarGridSpec(
            num_scalar_prefetch=2,
            grid=(D_MODEL // tn2, nb),
            in_specs=[
                pl.BlockSpec((MOE_TM, D_EXPERT), mrow),
                pl.BlockSpec((1, 1, D_EXPERT, tn2), wmap),
                pl.BlockSpec((1, 1, tn2), bmap),
            ],
            out_specs=pl.BlockSpec((MOE_TM, tn2), lambda n, m, be, nu: (m, n)),
            scratch_shapes=[pltpu.VMEM((D_EXPERT, tn2), BF16)],
        ),
        out_shape=jax.ShapeDtypeStruct((n_rows, D_MODEL), F32),
        compiler_params=_cparams(("parallel", "arbitrary")),
        name="moe_down",
    )(block_e, n_used, hid, w_dn, bd)


def _combine_kernel(dest_ref, gate_ref, y_hbm, x_ref, m_ref, g_ref, b_ref, o_ref, buf, sem):
    def issue(t, c):
        for k in range(TOP_K):
            pltpu.make_async_copy(y_hbm.at[pl.ds(dest_ref[0, 0, t * TOP_K + k], 1), :],
                                  buf.at[k, pl.ds(t, 1), :], sem).start()
        return c

    lax.fori_loop(0, MOE_TOK, issue, 0)
    for k in range(TOP_K):
        pltpu.make_async_copy(y_hbm.at[pl.ds(0, MOE_TOK), :], buf.at[k], sem).wait()
    gate = gate_ref[...]
    y = gate[:, 0:1] * buf[0]
    for k in range(1, TOP_K):
        y = y + gate[:, k:k + 1] * buf[k]
    m = m_ref[0, 0]
    z = DEEPNORM_ALPHA * x_ref[...] + m[5:6] * y
    o_ref[...] = _ln(z) * g_ref[...] + b_ref[...]


def _combine_pn(y_rows, dest, gate, x, modt, ln_g, ln_b, mod_map):
    t = x.shape[0]
    nb = t // MOE_TOK
    row = pl.BlockSpec((MOE_TOK, D_MODEL), lambda i: (i, 0))
    vec = pl.BlockSpec((1, D_MODEL), lambda i: (0, 0))
    return pl.pallas_call(
        _combine_kernel,
        grid=(nb,),
        in_specs=[
            pl.BlockSpec((1, 1, MOE_TOK * TOP_K), lambda i: (i, 0, 0), memory_space=pltpu.SMEM),
            pl.BlockSpec((MOE_TOK, LANE), lambda i: (i, 0)),
            pl.BlockSpec(memory_space=pl.ANY),
            row,
            pl.BlockSpec((1, 1, 8, D_MODEL), lambda i: mod_map(i) + (0, 0)),
            vec, vec,
        ],
        out_specs=row,
        out_shape=jax.ShapeDtypeStruct((t, D_MODEL), F32),
        scratch_shapes=[pltpu.VMEM((TOP_K, MOE_TOK, D_MODEL), F32), pltpu.SemaphoreType.DMA(())],
        compiler_params=_cparams(("arbitrary",)),
        name="moe_combine",
    )(dest.reshape(nb, 1, MOE_TOK * TOP_K), gate, y_rows, x, modt,
      ln_g.reshape(1, D_MODEL), ln_b.reshape(1, D_MODEL))


def _moe(h, top_e, gate, x, modt, mod_map, ln_g, ln_b, layer, w_gu, b_gu, w_dn, b_dn):
    t = h.shape[0]
    n_assign = t * TOP_K
    nb = n_assign // MOE_TM + N_EXPERTS
    e_flat = top_e[:, :TOP_K].reshape(-1)
    onehot = (e_flat[:, None] == jnp.arange(N_EXPERTS, dtype=I32)[None, :]).astype(I32)
    ranks = jnp.cumsum(onehot, axis=0)
    counts = ranks[-1]
    rank = jnp.sum(ranks * onehot, axis=1) - 1
    padded = (counts + MOE_TM - 1) // MOE_TM * MOE_TM
    pad_end = jnp.cumsum(padded)
    pad_start = jnp.concatenate([jnp.zeros((1,), I32), pad_end]).astype(I32)
    dest = (pad_start[e_flat] + rank).astype(I32).reshape(t, TOP_K)
    block_e = jnp.minimum(
        jnp.searchsorted(pad_end, jnp.arange(nb, dtype=I32) * MOE_TM, side="right"), N_EXPERTS - 1).astype(I32)
    n_used = (pad_end[-1:] // MOE_TM).astype(I32)

    xs = _dispatch(h, dest, counts.astype(I32), pad_start, nb * MOE_TM)
    bg = b_gu[layer, :, None, 0::2]
    bl = b_gu[layer, :, None, 1::2]
    y_rows = _experts(xs, block_e, n_used, layer, w_gu, bg, bl, w_dn, b_dn[layer, :, None, :])
    return _combine_pn(y_rows, dest, gate, x, modt, ln_g, ln_b, mod_map)


def _rope_tables(n_ctx, seq):
    pos = jnp.arange(seq)
    row = (pos // GRID_W).astype(F32)
    col = (pos % GRID_W).astype(F32)
    n_freq = MLA_ROPE // 4
    freqs = ROPE_BASE ** (-jnp.arange(n_freq, dtype=F32) / n_freq)
    ar = row[:, None] * freqs
    ac = col[:, None] * freqs
    cos = jnp.concatenate([jnp.cos(ar), jnp.cos(ar), jnp.cos(ac), jnp.cos(ac)], axis=-1)
    sin = jnp.concatenate([jnp.sin(ar), jnp.sin(ar), jnp.sin(ac), jnp.sin(ac)], axis=-1)
    cos = jnp.concatenate([jnp.ones((n_ctx, MLA_ROPE), F32), cos], axis=0)
    sin = jnp.concatenate([jnp.zeros((n_ctx, MLA_ROPE), F32), sin], axis=0)
    return cos, sin


def _rot_cols(w):
    q = MLA_ROPE // 4
    a, b, c, d = w[..., :q], w[..., q:2 * q], w[..., 2 * q:3 * q], w[..., 3 * q:]
    return jnp.concatenate([-b, a, -d, c], axis=-1)


def _rmsnorm_epilogue(accs, ex):
    x = accs[0]
    return x * lax.rsqrt(jnp.mean(x * x, axis=-1, keepdims=True) + RMS_EPS) * ex[0]


def _hgrn_layer(xs, modt, mod_map, w_in, gnorm, w_o, lbs, bsz, l, n_ctx):
    h = _ln_mod(xs, modt, mod_map, 0)
    wb = w_in.astype(BF16)
    silu_ep = lambda accs, ex: _silu(accs[0])
    q = _mm(h, [wb], HG_W, 1024, BF16, silu_ep, w_col0=0, name="hg_q")
    fr = _mm(h, [wb], 2 * HG_W, 1024, F32, w_col0=HG_W, name="hg_f")
    v = _mm(h, [wb], HG_W, 1024, BF16, w_col0=3 * HG_W, name="hg_v")
    g = _mm(h, [wb], HG_W, 1024, BF16, silu_ep, w_col0=4 * HG_W, name="hg_g")
    r3 = lambda a: a.reshape(bsz, l, a.shape[-1])
    o_f = _hgrn_scan(r3(q), r3(fr), r3(v), lbs[0], False, n_ctx)
    o = _hgrn_scan(r3(q), r3(fr), r3(v), lbs[1], True, n_ctx, o_fwd=o_f, g=r3(g), gnorm=gnorm)
    return _mm(o.reshape(bsz * l, HG_W), [w_o.astype(BF16)], D_MODEL, 1024, BF16, name="hg_o")


def _mla_layer(xs, modt, mod_map, w_in, q_norm, kv_norm, w_uq, w_ukv, w_o, bsz, l, n_ctx):
    h = _ln_mod(xs, modt, mod_map, 0)
    wb = w_in.astype(BF16)
    nl = l // ROW_TILE
    vec = lambda n: ((1, n), lambda j, i: (0, 0))
    cq = _mm(h, [wb], MLA_Q_RANK, 512, BF16, _rmsnorm_epilogue,
             extras=[(q_norm.reshape(1, -1),) + vec(MLA_Q_RANK)], tm=ROW_TILE, w_col0=0, name="mla_cq")
    ckv = _mm(h, [wb], MLA_KV_RANK, 512, BF16, _rmsnorm_epilogue,
              extras=[(kv_norm.reshape(1, -1),) + vec(MLA_KV_RANK)], tm=ROW_TILE, w_col0=MLA_Q_RANK, name="mla_ckv")
    cos, sin = _rope_tables(n_ctx, l - n_ctx)
    rope_ep = lambda accs, ex: accs[0] * ex[0] + accs[1] * ex[1]
    tab = lambda n: ((ROW_TILE, n), lambda j, i: (i % nl, 0))
    w_kp = w_in[:, MLA_Q_RANK + MLA_KV_RANK:]
    zk = jnp.zeros((D_MODEL, LANE - MLA_ROPE), F32)
    zt = jnp.zeros((l, LANE - MLA_ROPE), F32)
    kp = _mm(h, [jnp.concatenate([w_kp, zk], 1).astype(BF16), jnp.concatenate([_rot_cols(w_kp), zk], 1).astype(BF16)],
             LANE, LANE, BF16, rope_ep,
             extras=[(jnp.concatenate([cos, zt], 1),) + tab(LANE), (jnp.concatenate([sin, zt], 1),) + tab(LANE)],
             tm=ROW_TILE, name="mla_kp")
    wq = w_uq.reshape(MLA_Q_RANK, MLA_HEADS, MLA_NOPE + MLA_ROPE)
    zq = jnp.zeros((MLA_Q_RANK, MLA_HEADS, MLA_QK_PAD - MLA_NOPE - MLA_ROPE), F32)
    wqa = jnp.concatenate([wq, zq], -1).reshape(MLA_Q_RANK, -1).astype(BF16)
    wqb = jnp.concatenate([jnp.zeros_like(wq[..., :MLA_NOPE]), _rot_cols(wq[..., MLA_NOPE:]), zq], -1)
    wqb = wqb.reshape(MLA_Q_RANK, -1).astype(BF16)
    zt = jnp.zeros((l, MLA_QK_PAD - MLA_NOPE - MLA_ROPE), F32)
    q_scale = MLA_SCALE * LOG2_E
    cq_tab = jnp.concatenate([jnp.ones((l, MLA_NOPE), F32), cos, zt], 1) * q_scale
    sq_tab = jnp.concatenate([jnp.zeros((l, MLA_NOPE), F32), sin, zt], 1) * q_scale
    hq = 8
    qf = _mm(cq, [wqa, wqb], MLA_HEADS * MLA_QK_PAD, hq * MLA_QK_PAD, BF16, rope_ep,
             extras=[(jnp.tile(cq_tab, (1, hq)),) + tab(hq * MLA_QK_PAD),
                     (jnp.tile(sq_tab, (1, hq)),) + tab(hq * MLA_QK_PAD)], tm=ROW_TILE, name="mla_q")
    kv = _mm(ckv, [w_ukv.astype(BF16)], MLA_HEADS * (MLA_NOPE + MLA_V), 1024, BF16, name="mla_kv")
    r3 = lambda a: a.reshape(bsz, l, a.shape[-1])
    o = _attention(r3(qf), r3(kv), r3(kp), n_ctx)
    return _mm(o.reshape(bsz * (l - n_ctx), MLA_HEADS * MLA_V), [w_o.astype(BF16)], D_MODEL, 1024, BF16, name="mla_o")


def kernel(x, c, ctx, c_ctx, ada_w, ada_b, ln_g, ln_b, hg_w_in, hg_gnorm, hg_lb_logits, hg_w_o, mla_w_in,
           mla_q_norm, mla_kv_norm, mla_w_uq, mla_w_ukv, mla_w_o, router_w, router_b, exp_w_gu, exp_b_gu,
           exp_w_dn, exp_b_dn):
    bsz, seq, d = x.shape
    n_ctx = ctx.shape[1]
    l = n_ctx + seq
    nl = l // ROW_TILE
    ns = seq // ROW_TILE
    nc = n_ctx // ROW_TILE

    cc = jnp.zeros((8, d), F32).at[:bsz].set(c).at[bsz].set(c_ctx)
    mods = _ada(cc, ada_w, ada_b).reshape(DEPTH, 8, N_MOD, d)
    pad = jnp.zeros((DEPTH, bsz, 8 - N_MOD, d), F32)
    m_lat = jnp.concatenate([mods[:, :bsz], pad], axis=2)
    m_ctx = jnp.concatenate([jnp.broadcast_to(mods[:, bsz:bsz + 1], (DEPTH, bsz, N_MOD, d)), pad], axis=2)
    modt = jnp.stack([m_ctx, m_lat], axis=2)
    lower = jnp.cumsum(jax.nn.softmax(hg_lb_logits.astype(F32), axis=0), axis=0)

    xs = jnp.concatenate([ctx, x], axis=1).reshape(bsz * l, d)

    def map_all(r):
        per = l // r
        return lambda i: (i // per, ((i % per) >= n_ctx // r).astype(I32))

    def map_lat(r):
        per = seq // r
        return lambda i: (i // per, 1)

    y = _hgrn_layer(xs, modt[0], map_all(ROW_TILE), hg_w_in[0], hg_gnorm[0], hg_w_o[0], lower[0], bsz, l, n_ctx)
    xs, h, te, tg = _pn_router(xs, y, modt[0], ln_g[0, 0], ln_b[0, 0], router_w[0], router_b[0], bsz * l,
                               lambda i: i, map_all(ROW_TILE))
    xs = _moe(h, te, tg, xs, modt[0], map_all(MOE_TOK), ln_g[0, 1], ln_b[0, 1],
              0, exp_w_gu, exp_b_gu, exp_w_dn, exp_b_dn)

    y = _mla_layer(xs, modt[1], map_all(ROW_TILE), mla_w_in[0], mla_q_norm[0], mla_kv_norm[0], mla_w_uq[0],
                   mla_w_ukv[0], mla_w_o[0], bsz, l, n_ctx)
    xl, h, te, tg = _pn_router(xs, y, modt[1], ln_g[1, 0], ln_b[1, 0], router_w[1], router_b[1], bsz * seq,
                               lambda i: (i // ns) * nl + nc + i % ns, map_lat(ROW_TILE))
    out = _moe(h, te, tg, xl, modt[1], map_lat(MOE_TOK), ln_g[1, 1], ln_b[1, 1],
               1, exp_w_gu, exp_b_gu, exp_w_dn, exp_b_dn)
    return out.reshape(bsz, seq, d)
```

```python
import functools

import jax
import jax.numpy as jnp
from jax import lax
from jax.experimental import pallas as pl
from jax.experimental.pallas import tpu as pltpu

F32 = jnp.float32
BF16 = jnp.bfloat16
I32 = jnp.int32

D_MODEL = 2048
DEPTH = 2
GRID_W = 64
N_MOD = 6

HG_HEADS = 16
HG_DK = 128
HG_W = HG_HEADS * HG_DK
HG_SUB = 16

MLA_HEADS = 16
MLA_Q_RANK = 512
MLA_KV_RANK = 512
MLA_NOPE = 128
MLA_ROPE = 64
MLA_V = 128
MLA_QK_PAD = 256
MLA_SCALE = (MLA_NOPE + MLA_ROPE) ** -0.5
ROPE_BASE = 10000.0
LOG2_E = 1.4426950408889634

N_EXPERTS = 32
TOP_K = 4
D_EXPERT = 2048
SWIGLU_LIMIT = 7.0
SWIGLU_ALPHA = 1.702

DEEPNORM_ALPHA = (2 * DEPTH) ** 0.25
LN_EPS = 1e-5
RMS_EPS = 1e-6

LANE = 128
MXU_N = 256
ROW_TILE = 256
MOE_TM = 512
MOE_TOK = 128
ATTN_CHUNKS = 4
VMEM_LIMIT = 48 * 1024 * 1024

_HIGHEST = lax.Precision.HIGHEST


def _cparams(sem):
    return pltpu.CompilerParams(dimension_semantics=sem, vmem_limit_bytes=VMEM_LIMIT)


def _ln(x):
    mu = jnp.mean(x, axis=-1, keepdims=True)
    xc = x - mu
    var = jnp.mean(xc * xc, axis=-1, keepdims=True)
    return xc * lax.rsqrt(var + LN_EPS)


def _silu(x):
    return x * jax.nn.sigmoid(x)


def _ada_kernel(c_ref, w_ref, b_ref, o_ref):
    s = _silu(c_ref[...])
    o_ref[0] = jnp.dot(s, w_ref[0], precision=_HIGHEST, preferred_element_type=F32) + b_ref[0]


def _ada(cc, ada_w, ada_b):
    n = N_MOD * D_MODEL
    tn = 1024
    return pl.pallas_call(
        _ada_kernel,
        grid=(DEPTH, n // tn),
        in_specs=[
            pl.BlockSpec((8, D_MODEL), lambda i, j: (0, 0)),
            pl.BlockSpec((1, D_MODEL, tn), lambda i, j: (i, 0, j)),
            pl.BlockSpec((1, 1, tn), lambda i, j: (i, 0, j)),
        ],
        out_specs=pl.BlockSpec((1, 8, tn), lambda i, j: (i, 0, j)),
        out_shape=jax.ShapeDtypeStruct((DEPTH, 8, n), F32),
        compiler_params=_cparams(("parallel", "parallel")),
        name="ada",
    )(cc, ada_w, ada_b.reshape(DEPTH, 1, n))


def _lnmod_kernel(x_ref, m_ref, o_ref, *, si):
    m = m_ref[0, 0]
    o_ref[...] = (_ln(x_ref[...]) * (1.0 + m[si + 1:si + 2]) + m[si:si + 1]).astype(o_ref.dtype)


def _ln_mod(x, modt, mod_map, si):
    t = x.shape[0]
    return pl.pallas_call(
        functools.partial(_lnmod_kernel, si=si),
        grid=(t // ROW_TILE,),
        in_specs=[
            pl.BlockSpec((ROW_TILE, D_MODEL), lambda i: (i, 0)),
            pl.BlockSpec((1, 1, 8, D_MODEL), lambda i: mod_map(i) + (0, 0)),
        ],
        out_specs=pl.BlockSpec((ROW_TILE, D_MODEL), lambda i: (i, 0)),
        out_shape=jax.ShapeDtypeStruct((t, D_MODEL), BF16),
        compiler_params=_cparams(("parallel",)),
        name="ln_mod",
    )(x, modt)


def _pn_router_kernel(x_ref, y_ref, m_ref, g_ref, b_ref, rw_ref, rb_ref, xn_ref, h_ref, te_ref, tg_ref):
    m = m_ref[0, 0]
    z = DEEPNORM_ALPHA * x_ref[...] + m[2:3] * y_ref[...].astype(F32)
    xn = _ln(z) * g_ref[...] + b_ref[...]
    xn_ref[...] = xn
    h = _ln(xn) * (1.0 + m[4:5]) + m[3:4]
    h_ref[...] = h
    logits = jnp.dot(h, rw_ref[...], precision=_HIGHEST, preferred_element_type=F32) + rb_ref[...]
    lane = lax.broadcasted_iota(I32, logits.shape, 1).astype(F32)
    neg = jnp.float32(-jnp.inf)
    cur = jnp.where(lane < N_EXPERTS, logits, neg)
    vals, idxs = [], []
    for _ in range(TOP_K):
        mx = jnp.max(cur, axis=-1, keepdims=True)
        ix = jnp.min(jnp.where(cur == mx, lane, float(LANE)), axis=-1, keepdims=True)
        vals.append(mx)
        idxs.append(ix)
        cur = jnp.where(lane == ix, neg, cur)
    es = [jnp.exp(v - vals[0]) for v in vals]
    den = es[0] + es[1] + es[2] + es[3]
    te = jnp.zeros(logits.shape, F32)
    tg = jnp.zeros(logits.shape, F32)
    for k in range(TOP_K):
        te = jnp.where(lane == k, idxs[k], te)
        tg = jnp.where(lane == k, es[k] / den, tg)
    te_ref[...] = te.astype(I32)
    tg_ref[...] = tg


def _pn_router(x, y, modt, ln_g, ln_b, rw, rb, n_tok, x_map, mod_map):
    rwp = jnp.zeros((D_MODEL, LANE), F32).at[:, :N_EXPERTS].set(rw)
    rbp = jnp.zeros((1, LANE), F32).at[0, :N_EXPERTS].set(rb)
    row = pl.BlockSpec((ROW_TILE, D_MODEL), lambda i: (i, 0))
    vec = pl.BlockSpec((1, D_MODEL), lambda i: (0, 0))
    nar = pl.BlockSpec((ROW_TILE, LANE), lambda i: (i, 0))
    return pl.pallas_call(
        _pn_router_kernel,
        grid=(n_tok // ROW_TILE,),
        in_specs=[
            pl.BlockSpec((ROW_TILE, D_MODEL), lambda i: (x_map(i), 0)),
            row,
            pl.BlockSpec((1, 1, 8, D_MODEL), lambda i: mod_map(i) + (0, 0)),
            vec, vec,
            pl.BlockSpec((D_MODEL, LANE), lambda i: (0, 0)),
            pl.BlockSpec((1, LANE), lambda i: (0, 0)),
        ],
        out_specs=[row, row, nar, nar],
        out_shape=[
            jax.ShapeDtypeStruct((n_tok, D_MODEL), F32),
            jax.ShapeDtypeStruct((n_tok, D_MODEL), F32),
            jax.ShapeDtypeStruct((n_tok, LANE), I32),
            jax.ShapeDtypeStruct((n_tok, LANE), F32),
        ],
        compiler_params=_cparams(("parallel",)),
        name="pn_router",
    )(x, y, modt, ln_g.reshape(1, D_MODEL), ln_b.reshape(1, D_MODEL), rwp, rbp)


def _mm_kernel(*refs, n_w, n_x, epilogue):
    a = refs[0][...]
    accs = [jnp.dot(a, w[...], preferred_element_type=F32) for w in refs[1:1 + n_w]]
    extras = [e[...] for e in refs[1 + n_w:1 + n_w + n_x]]
    o_ref = refs[-1]
    o_ref[...] = epilogue(accs, extras).astype(o_ref.dtype)


def _mm(a, ws, n_out, tn, out_dtype, epilogue=None, extras=(), tm=512, w_col0=0, rows=None, name="mm"):
    m, k = a.shape
    tm = min(tm, m)
    a_map = lambda i: i
    if rows is not None:
        m, a_map = rows
    assert m % tm == 0 and n_out % tn == 0 and w_col0 % tn == 0
    if epilogue is None:
        epilogue = lambda accs, ex: accs[0]
    c0 = w_col0 // tn
    in_specs = [pl.BlockSpec((tm, k), lambda j, i: (a_map(i), 0))]
    in_specs += [pl.BlockSpec((k, tn), lambda j, i: (0, j + c0)) for _ in ws]
    in_specs += [pl.BlockSpec(bs, im) for _, bs, im in extras]
    return pl.pallas_call(
        functools.partial(_mm_kernel, n_w=len(ws), n_x=len(extras), epilogue=epilogue),
        grid=(n_out // tn, m // tm),
        in_specs=in_specs,
        out_specs=pl.BlockSpec((tm, tn), lambda j, i: (i, j)),
        out_shape=jax.ShapeDtypeStruct((m, n_out), out_dtype),
        compiler_params=_cparams(("parallel", "parallel")),
        name=name,
    )(a, *ws, *[e[0] for e in extras])


def _scan_kernel(*refs, rev, hb, tl, final):
    n_in = 8 if final else 5
    q_ref, f_ref, v_ref, lb_ref, ones_ref = refs[:5]
    o_ref = refs[n_in]
    st_ref, nat_ref, dil_ref, p_ref, upd_ref, sb_ref = refs[n_in + 1:]
    nblk = tl // HG_SUB
    q_nat, v_nat, f_nat, qs_nat, kd_nat, o_nat = (nat_ref.at[n] for n in range(6))
    dq, dk, dv, dl, dcf, dcr = (dil_ref.at[n] for n in range(6))
    db, dother = (dcr, dcf) if rev else (dcf, dcr)
    rows = lambda i: pl.ds(i, nblk, stride=HG_SUB)
    lanes = lambda h: slice(h * HG_DK, (h + 1) * HG_DK)
    pairs = [list(range(i, HG_SUB)) if rev else list(range(i + 1)) for i in range(HG_SUB)]
    bases = [sum(len(p) for p in pairs[:i]) * nblk for i in range(HG_SUB)]

    @pl.when(pl.program_id(2) == 0)
    def _():
        st_ref[...] = jnp.zeros(st_ref.shape, F32)

    vb = v_ref[0]
    tots = []
    for h in range(hb):
        lb = lb_ref[:, lanes(h)]
        q_nat[h] = q_ref[0, :, lanes(h)].astype(F32)
        v_nat[h] = vb[:, lanes(h)].astype(F32)
        f_nat[h] = f_ref[0, :, lanes(h)]
        for i in range(HG_SUB):
            forget = lb + (1.0 - lb) * jax.nn.sigmoid(f_nat[h, rows(i), :])
            dl[h, i] = jnp.log(forget)
            dk[h, i] = 1.0 - forget
            dq[h, i] = q_nat[h, rows(i), :]
            dv[h, i] = v_nat[h, rows(i), :]
        run = dl[h, 0]
        dcf[h, 0] = run
        for i in range(1, HG_SUB):
            run = run + dl[h, i]
            dcf[h, i] = run
        tots.append(run)
        run = dl[h, HG_SUB - 1]
        dcr[h, HG_SUB - 1] = run
        for i in range(HG_SUB - 2, -1, -1):
            run = run + dl[h, i]
            dcr[h, i] = run
        for i in range(HG_SUB):
            nxt = i - 1 if rev else i + 1
            rest = dother[h, nxt] if 0 <= nxt < HG_SUB else jnp.zeros((nblk, HG_DK), F32)
            qs_nat[h, rows(i), :] = dq[h, i] * jnp.exp(db[h, i])
            kd_nat[h, rows(i), :] = dk[h, i] * jnp.exp(rest)
        for i in range(HG_SUB):
            qi = dq[h, i]
            bi = db[h, i]
            for n, j in enumerate(pairs[i]):
                p_ref[pl.ds(bases[i] + n * nblk, nblk), lanes(h)] = (
                    qi * dk[h, j] * jnp.exp(bi - db[h, j])).astype(BF16)
    ones = ones_ref[...]
    hpg = MXU_N // HG_DK
    for i in range(HG_SUB):
        for g0 in range(0, hb, hpg):
            rs = jnp.dot(p_ref[pl.ds(bases[i], len(pairs[i]) * nblk), g0 * HG_DK:(g0 + hpg) * HG_DK], ones,
                         preferred_element_type=F32)
            for h in range(g0, g0 + hpg):
                oi = rs[0:nblk, lanes(h - g0)] * dv[h, pairs[i][0]]
                for n in range(1, len(pairs[i])):
                    oi = oi + rs[n * nblk:(n + 1) * nblk, lanes(h - g0)] * dv[h, pairs[i][n]]
                o_nat[h, rows(i), :] = oi

    blk = lambda j: pl.ds(j * HG_SUB, HG_SUB)
    for j in range(nblk):
        for h in range(hb):
            upd_ref[j, h] = lax.dot_general(vb[j * HG_SUB:(j + 1) * HG_SUB, lanes(h)],
                                            kd_nat[h, blk(j), :].astype(BF16),
                                            (((0,), (0,)), ((), ())), preferred_element_type=F32)
    for h in range(hb):
        st = st_ref[h]
        for j in (range(nblk - 1, -1, -1) if rev else range(nblk)):
            sb_ref[j, h] = st.astype(BF16)
            st = st * jnp.exp(tots[h][j:j + 1, :]) + upd_ref[j, h]
        st_ref[h] = st
    for j in range(nblk):
        for h in range(hb):
            oi = lax.dot_general(qs_nat[h, blk(j), :].astype(BF16), sb_ref[j, h],
                                 (((1,), (1,)), ((), ())), preferred_element_type=F32)
            if final:
                o_nat[h, blk(j), :] = o_nat[h, blk(j), :] + oi
            else:
                o_ref[0, blk(j), lanes(h)] = o_nat[h, blk(j), :] + oi

    if final:
        of_ref, g_ref, gw_ref = refs[5:8]
        for h in range(hb):
            oh = o_nat[h] + of_ref[0, :, lanes(h)]
            ms = jnp.mean(oh * oh, axis=-1, keepdims=True)
            o_ref[0, :, lanes(h)] = (oh * lax.rsqrt(ms + RMS_EPS) * gw_ref[:, lanes(h)]
                                     * g_ref[0, :, lanes(h)].astype(F32)).astype(o_ref.dtype)


def _hgrn_scan(q, fr, v, lb, rev, n_ctx, o_fwd=None, g=None, gnorm=None, hb=4, tl=256):
    bsz, l, _ = q.shape
    w = hb * HG_DK
    nl = l // tl
    nc = n_ctx // tl
    final = rev

    def lmap(i):
        if not rev:
            return i
        return jnp.where(i < nc, nc - 1 - i, nl - 1 - (i - nc))

    foff = (HG_W // w) if rev else 0
    tile = lambda b, h, i: (b, lmap(i), h)
    ones = (jnp.arange(MXU_N)[:, None] // HG_DK == jnp.arange(MXU_N)[None, :] // HG_DK).astype(BF16)
    in_specs = [
        pl.BlockSpec((1, tl, w), tile),
        pl.BlockSpec((1, tl, w), lambda b, h, i: (b, lmap(i), h + foff)),
        pl.BlockSpec((1, tl, w), tile),
        pl.BlockSpec((1, w), lambda b, h, i: (0, h)),
        pl.BlockSpec((MXU_N, MXU_N), lambda b, h, i: (0, 0)),
    ]
    args = [q, fr, v, lb.reshape(1, HG_W), ones]
    nblk = tl // HG_SUB
    n_pairs = HG_SUB * (HG_SUB + 1) // 2
    scratch = [pltpu.VMEM((hb, HG_DK, HG_DK), F32), pltpu.VMEM((6, hb, tl, HG_DK), F32),
               pltpu.VMEM((6, hb, HG_SUB, nblk, HG_DK), F32), pltpu.VMEM((n_pairs * nblk, w), BF16),
               pltpu.VMEM((nblk, hb, HG_DK, HG_DK), F32), pltpu.VMEM((nblk, hb, HG_DK, HG_DK), BF16)]
    if final:
        in_specs += [pl.BlockSpec((1, tl, w), tile), pl.BlockSpec((1, tl, w), tile),
                     pl.BlockSpec((1, w), lambda b, h, i: (0, h))]
        args += [o_fwd, g, gnorm.reshape(1, HG_W)]
    return pl.pallas_call(
        functools.partial(_scan_kernel, rev=rev, hb=hb, tl=tl, final=final),
        grid=(bsz, HG_W // w, nl),
        in_specs=in_specs,
        out_specs=pl.BlockSpec((1, tl, w), tile),
        out_shape=jax.ShapeDtypeStruct((bsz, l, HG_W), BF16 if final else F32),
        scratch_shapes=scratch,
        compiler_params=_cparams(("parallel", "parallel", "arbitrary")),
        name="hgrn_scan_bwd" if rev else "hgrn_scan_fwd",
    )(*args)


def _attn_kernel(q_ref, kn_ref, kp_ref, v_ref, o_ref, k_scr, v_scr):
    @pl.when(pl.program_id(2) == 0)
    def _():
        k_scr[:, 0:MLA_NOPE] = kn_ref[0]
        k_scr[:, MLA_NOPE:MLA_QK_PAD] = kp_ref[0]
        v_scr[:, 0:MLA_V] = v_ref[0]
        v_scr[:, MLA_V:] = jnp.ones((v_scr.shape[0], v_scr.shape[1] - MLA_V), v_scr.dtype)

    q = q_ref[0]
    l = k_scr.shape[0]
    step = -(-l // (ATTN_CHUNKS * MXU_N)) * MXU_N
    m = acc = None
    for c0 in range(0, l, step):
        rows = pl.ds(c0, min(step, l - c0))
        s = lax.dot_general(q, k_scr[rows, :], (((1,), (1,)), ((), ())), preferred_element_type=F32)
        mc = jnp.max(s, axis=-1, keepdims=True)
        m_new = mc if m is None else jnp.maximum(m, mc)
        pv = jnp.dot(jnp.exp2(s - m_new).astype(BF16), v_scr[rows, :], preferred_element_type=F32)
        acc = pv if m is None else acc * jnp.exp2(m - m_new) + pv
        m = m_new
    o_ref[0] = (acc[:, :MLA_V] / acc[:, MLA_V:MLA_V + 1]).astype(o_ref.dtype)


def _attention(qf, kv, kp, tq=512):
    bsz, s, _ = qf.shape
    l = kv.shape[1]
    tq = min(tq, s)
    assert s % tq == 0
    return pl.pallas_call(
        _attn_kernel,
        grid=(bsz, MLA_HEADS, s // tq),
        in_specs=[
            pl.BlockSpec((1, tq, MLA_QK_PAD), lambda b, h, i: (b, i, h)),
            pl.BlockSpec((1, l, MLA_NOPE), lambda b, h, i: (b, 0, 2 * h)),
            pl.BlockSpec((1, l, LANE), lambda b, h, i: (b, 0, 0)),
            pl.BlockSpec((1, l, MLA_V), lambda b, h, i: (b, 0, 2 * h + 1)),
        ],
        out_specs=pl.BlockSpec((1, tq, MLA_V), lambda b, h, i: (b, i, h)),
        out_shape=jax.ShapeDtypeStruct((bsz, s, MLA_HEADS * MLA_V), BF16),
        scratch_shapes=[pltpu.VMEM((l, MLA_QK_PAD), BF16), pltpu.VMEM((l, MXU_N), BF16)],
        compiler_params=_cparams(("parallel", "parallel", "arbitrary")),
        name="mla_attn",
    )(qf, kv, kp, kv)


def _dispatch_kernel(cnt_ref, pst_ref, dest_ref, h_ref, xs_hbm, sem, psem):
    i = pl.program_id(0)

    def row_copy(t, dst, s):
        return pltpu.make_async_copy(h_ref.at[pl.ds(t, 1), :], xs_hbm.at[pl.ds(dst, 1), :], s)

    def issue(t, c):
        for k in range(TOP_K):
            row_copy(t, dest_ref[0, 0, t * TOP_K + k], sem).start()
        return c

    lax.fori_loop(0, MOE_TOK, issue, 0)

    @pl.when(i == 0)
    def _():
        for e in range(N_EXPERTS):
            first = pst_ref[e] + cnt_ref[e]
            last = pst_ref[e + 1]
            lax.fori_loop(first, last, lambda r, c: (row_copy(0, r, psem).start(), c)[1], 0)
            lax.fori_loop(first, last, lambda r, c: (row_copy(0, r, psem).wait(), c)[1], 0)

        def block_copy(m):
            return pltpu.make_async_copy(h_ref, xs_hbm.at[pl.ds(m * MOE_TOK, MOE_TOK), :], psem)

        used = pst_ref[N_EXPERTS] // MOE_TOK
        total = xs_hbm.shape[0] // MOE_TOK
        lax.fori_loop(used, total, lambda m, c: (block_copy(m).start(), c)[1], 0)
        lax.fori_loop(used, total, lambda m, c: (block_copy(m).wait(), c)[1], 0)

    for _ in range(TOP_K):
        pltpu.make_async_copy(h_ref, xs_hbm.at[pl.ds(0, MOE_TOK), :], sem).wait()


def _dispatch(h, dest, counts, pad_start, n_rows):
    t = h.shape[0]
    nb = t // MOE_TOK
    return pl.pallas_call(
        _dispatch_kernel,
        grid_spec=pltpu.PrefetchScalarGridSpec(
            num_scalar_prefetch=2,
            grid=(nb,),
            in_specs=[
                pl.BlockSpec((1, 1, MOE_TOK * TOP_K), lambda i, c, p: (i, 0, 0), memory_space=pltpu.SMEM),
                pl.BlockSpec((MOE_TOK, D_MODEL), lambda i, c, p: (i, 0)),
            ],
            out_specs=pl.BlockSpec(memory_space=pl.ANY),
            scratch_shapes=[pltpu.SemaphoreType.DMA(()), pltpu.SemaphoreType.DMA(())],
        ),
        out_shape=jax.ShapeDtypeStruct((n_rows, D_MODEL), F32),
        compiler_params=_cparams(("arbitrary",)),
        name="moe_dispatch",
    )(counts, pad_start, dest.reshape(nb, 1, MOE_TOK * TOP_K), h)


def _expert_switch(be_ref, nu_ref):
    m = pl.program_id(1)
    live = m < nu_ref[0]
    fresh = jnp.logical_or(m == 0, be_ref[m] != be_ref[jnp.maximum(m - 1, 0)])
    return live, jnp.logical_and(live, fresh)


def _gmm1_kernel(be_ref, nu_ref, x_ref, w_ref, p_ref, bg_ref, bl_ref, o_ref, wg_scr, wl_scr):
    live, fresh = _expert_switch(be_ref, nu_ref)
    half = MXU_N // 2

    @pl.when(fresh)
    def _():
        for c in range(w_ref.shape[3] // MXU_N):
            blk = w_ref[0, 0, :, c * MXU_N:(c + 1) * MXU_N].astype(BF16)
            sp = jnp.dot(blk, p_ref[...], preferred_element_type=F32)
            wg_scr[:, c * half:(c + 1) * half] = sp[:, :half].astype(BF16)
            wl_scr[:, c * half:(c + 1) * half] = sp[:, half:].astype(BF16)

    @pl.when(live)
    def _():
        x = x_ref[...].astype(BF16)
        g = jnp.dot(x, wg_scr[...], preferred_element_type=F32) + bg_ref[0]
        u = jnp.dot(x, wl_scr[...], preferred_element_type=F32) + bl_ref[0]
        g = jnp.minimum(g, SWIGLU_LIMIT)
        u = jnp.clip(u, -SWIGLU_LIMIT, SWIGLU_LIMIT)
        o_ref[...] = (g * jax.nn.sigmoid(SWIGLU_ALPHA * g) * (u + 1.0)).astype(o_ref.dtype)

    @pl.when(jnp.logical_not(live))
    def _():
        o_ref[...] = jnp.zeros(o_ref.shape, o_ref.dtype)


def _gmm2_kernel(be_ref, nu_ref, x_ref, w_ref, b_ref, o_ref, w_scr):
    live, fresh = _expert_switch(be_ref, nu_ref)

    @pl.when(fresh)
    def _():
        w_scr[...] = w_ref[0, 0].astype(BF16)

    @pl.when(live)
    def _():
        o_ref[...] = jnp.dot(x_ref[...], w_scr[...], preferred_element_type=F32) + b_ref[0]

    @pl.when(jnp.logical_not(live))
    def _():
        o_ref[...] = jnp.zeros(o_ref.shape, o_ref.dtype)


def _experts(xs, block_e, n_used, layer, w_gu, bg, bl, w_dn, bd, tn=512, tn2=1024):
    n_rows = xs.shape[0]
    nb = n_rows // MOE_TM
    idx = jnp.arange(MXU_N)
    perm = (idx[:, None] == jnp.where(idx < MXU_N // 2, 2 * idx, 2 * (idx - MXU_N // 2) + 1)[None, :]).astype(BF16)

    def mrow(n, m, be, nu):
        return (jnp.minimum(m, nu[0] - 1), 0)

    def bmap(n, m, be, nu):
        return (be[jnp.minimum(m, nu[0] - 1)], 0, n)

    def wmap(n, m, be, nu):
        return (layer, be[jnp.minimum(m, nu[0] - 1)], 0, n)

    hid = pl.pallas_call(
        _gmm1_kernel,
        grid_spec=pltpu.PrefetchScalarGridSpec(
            num_scalar_prefetch=2,
            grid=(D_EXPERT // tn, nb),
            in_specs=[
                pl.BlockSpec((MOE_TM, D_MODEL), mrow),
                pl.BlockSpec((1, 1, D_MODEL, 2 * tn), wmap),
                pl.BlockSpec((MXU_N, MXU_N), lambda n, m, be, nu: (0, 0)),
                pl.BlockSpec((1, 1, tn), bmap),
                pl.BlockSpec((1, 1, tn), bmap),
            ],
            out_specs=pl.BlockSpec((MOE_TM, tn), lambda n, m, be, nu: (m, n)),
            scratch_shapes=[pltpu.VMEM((D_MODEL, tn), BF16), pltpu.VMEM((D_MODEL, tn), BF16)],
        ),
        out_shape=jax.ShapeDtypeStruct((n_rows, D_EXPERT), BF16),
        compiler_params=_cparams(("parallel", "arbitrary")),
        name="moe_gate_up",
    )(block_e, n_used, xs, w_gu, perm, bg, bl)
    return pl.pallas_call(
        _gmm2_kernel,
        grid_spec=pltpu.PrefetchScalarGridSpec(
            num_scalar_prefetch=2,
            grid=(D_MODEL // tn2, nb),
            in_specs=[
                pl.BlockSpec((MOE_TM, D_EXPERT), mrow),
                pl.BlockSpec((1, 1, D_EXPERT, tn2), wmap),
                pl.BlockSpec((1, 1, tn2), bmap),
            ],
            out_specs=pl.BlockSpec((MOE_TM, tn2), lambda n, m, be, nu: (m, n)),
            scratch_shapes=[pltpu.VMEM((D_EXPERT, tn2), BF16)],
        ),
        out_shape=jax.ShapeDtypeStruct((n_rows, D_MODEL), F32),
        compiler_params=_cparams(("parallel", "arbitrary")),
        name="moe_down",
    )(block_e, n_used, hid, w_dn, bd)


def _combine_kernel(dest_ref, next_ref, gate_ref, y_hbm, x_ref, m_ref, g_ref, b_ref, o_ref, buf, sems):
    i = pl.program_id(0)
    slot = i % 2

    def gather(idx_ref, s):
        def issue(t, c):
            for k in range(TOP_K):
                pltpu.make_async_copy(y_hbm.at[pl.ds(idx_ref[0, 0, t * TOP_K + k], 1), :],
                                      buf.at[s, k, pl.ds(t, 1), :], sems.at[s]).start()
            return c

        lax.fori_loop(0, MOE_TOK, issue, 0)

    @pl.when(i == 0)
    def _():
        gather(dest_ref, 0)

    @pl.when(i + 1 < pl.num_programs(0))
    def _():
        gather(next_ref, 1 - slot)

    for k in range(TOP_K):
        pltpu.make_async_copy(y_hbm.at[pl.ds(0, MOE_TOK), :], buf.at[slot, k], sems.at[slot]).wait()
    gate = gate_ref[...]
    y = gate[:, 0:1] * buf[slot, 0]
    for k in range(1, TOP_K):
        y = y + gate[:, k:k + 1] * buf[slot, k]
    m = m_ref[0, 0]
    z = DEEPNORM_ALPHA * x_ref[...] + m[5:6] * y
    o_ref[...] = _ln(z) * g_ref[...] + b_ref[...]


def _combine_pn(y_rows, dest, gate, x, modt, ln_g, ln_b, mod_map):
    t = x.shape[0]
    nb = t // MOE_TOK
    dest3 = dest.reshape(nb, 1, MOE_TOK * TOP_K)
    row = pl.BlockSpec((MOE_TOK, D_MODEL), lambda i: (i, 0))
    vec = pl.BlockSpec((1, D_MODEL), lambda i: (0, 0))
    return pl.pallas_call(
        _combine_kernel,
        grid=(nb,),
        in_specs=[
            pl.BlockSpec((1, 1, MOE_TOK * TOP_K), lambda i: (i, 0, 0), memory_space=pltpu.SMEM),
            pl.BlockSpec((1, 1, MOE_TOK * TOP_K), lambda i: (jnp.minimum(i + 1, nb - 1), 0, 0),
                         memory_space=pltpu.SMEM),
            pl.BlockSpec((MOE_TOK, LANE), lambda i: (i, 0)),
            pl.BlockSpec(memory_space=pl.ANY),
            row,
            pl.BlockSpec((1, 1, 8, D_MODEL), lambda i: mod_map(i) + (0, 0)),
            vec, vec,
        ],
        out_specs=row,
        out_shape=jax.ShapeDtypeStruct((t, D_MODEL), F32),
        scratch_shapes=[pltpu.VMEM((2, TOP_K, MOE_TOK, D_MODEL), F32), pltpu.SemaphoreType.DMA((2,))],
        compiler_params=_cparams(("arbitrary",)),
        name="moe_combine",
    )(dest3, dest3, gate, y_rows, x, modt, ln_g.reshape(1, D_MODEL), ln_b.reshape(1, D_MODEL))


def _rank_kernel(te_ref, tri_ref, rank_ref, cnt_ref, run_ref):
    @pl.when(pl.program_id(0) == 0)
    def _():
        run_ref[...] = jnp.zeros(run_ref.shape, F32)

    te = te_ref[...]
    lane = lax.broadcasted_iota(I32, te.shape, 1)
    base = run_ref[...]
    rank = jnp.zeros(te.shape, F32)
    for k in range(TOP_K):
        hit = te[:, k:k + 1] == lane
        onehot = jnp.where(hit, 1.0, 0.0).astype(BF16)
        before = jnp.dot(tri_ref[...], onehot, preferred_element_type=F32)
        rk = jnp.sum(jnp.where(hit, before + base, 0.0), axis=-1, keepdims=True)
        rank = jnp.where(lane == k, rk, rank)
        base = base + jnp.sum(onehot.astype(F32), axis=0, keepdims=True)
    run_ref[...] = base
    rank_ref[...] = rank.astype(I32)
    cnt_ref[...] = base.astype(I32)


def _ranks(top_e):
    t = top_e.shape[0]
    tri = (jnp.arange(MOE_TOK)[:, None] > jnp.arange(MOE_TOK)[None, :]).astype(BF16)
    return pl.pallas_call(
        _rank_kernel,
        grid=(t // MOE_TOK,),
        in_specs=[pl.BlockSpec((MOE_TOK, LANE), lambda i: (i, 0)), pl.BlockSpec((MOE_TOK, MOE_TOK), lambda i: (0, 0))],
        out_specs=[pl.BlockSpec((MOE_TOK, LANE), lambda i: (i, 0)), pl.BlockSpec((1, LANE), lambda i: (0, 0))],
        out_shape=[jax.ShapeDtypeStruct((t, LANE), I32), jax.ShapeDtypeStruct((1, LANE), I32)],
        scratch_shapes=[pltpu.VMEM((1, LANE), F32)],
        compiler_params=_cparams(("arbitrary",)),
        name="moe_rank",
    )(top_e, tri)


def _moe(h, top_e, gate, x, modt, mod_map, ln_g, ln_b, layer, w_gu, b_gu, w_dn, b_dn):
    t = h.shape[0]
    n_assign = t * TOP_K
    nb = n_assign // MOE_TM + N_EXPERTS
    rank, cnt = _ranks(top_e)
    counts = cnt[0, :N_EXPERTS]
    padded = (counts + MOE_TM - 1) // MOE_TM * MOE_TM
    pad_end = jnp.cumsum(padded)
    pad_start = jnp.concatenate([jnp.zeros((1,), I32), pad_end]).astype(I32)
    te4 = top_e[:, :TOP_K]
    hit = te4[:, :, None] == jnp.arange(N_EXPERTS, dtype=I32)[None, None, :]
    dest = (jnp.sum(jnp.where(hit, pad_start[None, None, :N_EXPERTS], 0), axis=-1) + rank[:, :TOP_K]).astype(I32)
    block_e = jnp.minimum(
        jnp.searchsorted(pad_end, jnp.arange(nb, dtype=I32) * MOE_TM, side="right"), N_EXPERTS - 1).astype(I32)
    n_used = (pad_end[-1:] // MOE_TM).astype(I32)

    xs = _dispatch(h, dest, counts.astype(I32), pad_start, nb * MOE_TM)
    bg = b_gu[layer, :, None, 0::2]
    bl = b_gu[layer, :, None, 1::2]
    y_rows = _experts(xs, block_e, n_used, layer, w_gu, bg, bl, w_dn, b_dn[layer, :, None, :])
    return _combine_pn(y_rows, dest, gate, x, modt, ln_g, ln_b, mod_map)


def _rope_tables(n_ctx, seq):
    pos = jnp.arange(seq)
    row = (pos // GRID_W).astype(F32)
    col = (pos % GRID_W).astype(F32)
    n_freq = MLA_ROPE // 4
    freqs = ROPE_BASE ** (-jnp.arange(n_freq, dtype=F32) / n_freq)
    ar = row[:, None] * freqs
    ac = col[:, None] * freqs
    cos = jnp.concatenate([jnp.cos(ar), jnp.cos(ar), jnp.cos(ac), jnp.cos(ac)], axis=-1)
    sin = jnp.concatenate([jnp.sin(ar), jnp.sin(ar), jnp.sin(ac), jnp.sin(ac)], axis=-1)
    cos = jnp.concatenate([jnp.ones((n_ctx, MLA_ROPE), F32), cos], axis=0)
    sin = jnp.concatenate([jnp.zeros((n_ctx, MLA_ROPE), F32), sin], axis=0)
    return cos, sin


def _rot_cols(w):
    q = MLA_ROPE // 4
    a, b, c, d = w[..., :q], w[..., q:2 * q], w[..., 2 * q:3 * q], w[..., 3 * q:]
    return jnp.concatenate([-b, a, -d, c], axis=-1)


def _rmsnorm_epilogue(accs, ex):
    x = accs[0]
    return x * lax.rsqrt(jnp.mean(x * x, axis=-1, keepdims=True) + RMS_EPS) * ex[0]


def _hgrn_layer(xs, modt, mod_map, w_in, gnorm, w_o, lbs, bsz, l, n_ctx):
    h = _ln_mod(xs, modt, mod_map, 0)
    wb = w_in.astype(BF16)
    silu_ep = lambda accs, ex: _silu(accs[0])
    q = _mm(h, [wb], HG_W, 1024, BF16, silu_ep, w_col0=0, name="hg_q")
    fr = _mm(h, [wb], 2 * HG_W, 1024, F32, w_col0=HG_W, name="hg_f")
    v = _mm(h, [wb], HG_W, 1024, BF16, w_col0=3 * HG_W, name="hg_v")
    g = _mm(h, [wb], HG_W, 1024, BF16, silu_ep, w_col0=4 * HG_W, name="hg_g")
    r3 = lambda a: a.reshape(bsz, l, a.shape[-1])
    o_f = _hgrn_scan(r3(q), r3(fr), r3(v), lbs[0], False, n_ctx)
    o = _hgrn_scan(r3(q), r3(fr), r3(v), lbs[1], True, n_ctx, o_fwd=o_f, g=r3(g), gnorm=gnorm)
    return _mm(o.reshape(bsz * l, HG_W), [w_o.astype(BF16)], D_MODEL, 1024, BF16, name="hg_o")


def _mla_layer(xs, modt, mod_map, w_in, q_norm, kv_norm, w_uq, w_ukv, w_o, bsz, l, n_ctx):
    h = _ln_mod(xs, modt, mod_map, 0)
    wb = w_in.astype(BF16)
    nl = l // ROW_TILE
    vec = lambda n: ((1, n), lambda j, i: (0, 0))
    cq = _mm(h, [wb], MLA_Q_RANK, 512, BF16, _rmsnorm_epilogue,
             extras=[(q_norm.reshape(1, -1),) + vec(MLA_Q_RANK)], tm=ROW_TILE, w_col0=0, name="mla_cq")
    ckv = _mm(h, [wb], MLA_KV_RANK, 512, BF16, _rmsnorm_epilogue,
              extras=[(kv_norm.reshape(1, -1),) + vec(MLA_KV_RANK)], tm=ROW_TILE, w_col0=MLA_Q_RANK, name="mla_ckv")
    cos, sin = _rope_tables(n_ctx, l - n_ctx)
    rope_ep = lambda accs, ex: accs[0] * ex[0] + accs[1] * ex[1]
    tab = lambda n: ((ROW_TILE, n), lambda j, i: (i % nl, 0))
    ns, nc = (l - n_ctx) // ROW_TILE, n_ctx // ROW_TILE
    qtab = lambda n: ((ROW_TILE, n), lambda j, i: (nc + i % ns, 0))
    lat_rows = (bsz * (l - n_ctx), lambda i: (i // ns) * nl + nc + i % ns)
    w_kp = w_in[:, MLA_Q_RANK + MLA_KV_RANK:]
    zk = jnp.zeros((D_MODEL, LANE - MLA_ROPE), F32)
    zt = jnp.zeros((l, LANE - MLA_ROPE), F32)
    kp = _mm(h, [jnp.concatenate([w_kp, zk], 1).astype(BF16), jnp.concatenate([_rot_cols(w_kp), zk], 1).astype(BF16)],
             LANE, LANE, BF16, rope_ep,
             extras=[(jnp.concatenate([cos, zt], 1),) + tab(LANE), (jnp.concatenate([sin, zt], 1),) + tab(LANE)],
             tm=ROW_TILE, name="mla_kp")
    wq = w_uq.reshape(MLA_Q_RANK, MLA_HEADS, MLA_NOPE + MLA_ROPE)
    zq = jnp.zeros((MLA_Q_RANK, MLA_HEADS, MLA_QK_PAD - MLA_NOPE - MLA_ROPE), F32)
    wqa = jnp.concatenate([wq, zq], -1).reshape(MLA_Q_RANK, -1).astype(BF16)
    wqb = jnp.concatenate([jnp.zeros_like(wq[..., :MLA_NOPE]), _rot_cols(wq[..., MLA_NOPE:]), zq], -1)
    wqb = wqb.reshape(MLA_Q_RANK, -1).astype(BF16)
    zt = jnp.zeros((l, MLA_QK_PAD - MLA_NOPE - MLA_ROPE), F32)
    q_scale = MLA_SCALE * LOG2_E
    cq_tab = jnp.concatenate([jnp.ones((l, MLA_NOPE), F32), cos, zt], 1) * q_scale
    sq_tab = jnp.concatenate([jnp.zeros((l, MLA_NOPE), F32), sin, zt], 1) * q_scale
    hq = 8
    qf = _mm(cq, [wqa, wqb], MLA_HEADS * MLA_QK_PAD, hq * MLA_QK_PAD, BF16, rope_ep,
             extras=[(jnp.tile(cq_tab, (1, hq)),) + qtab(hq * MLA_QK_PAD),
                     (jnp.tile(sq_tab, (1, hq)),) + qtab(hq * MLA_QK_PAD)], tm=ROW_TILE, rows=lat_rows,
             name="mla_q")
    kv = _mm(ckv, [w_ukv.astype(BF16)], MLA_HEADS * (MLA_NOPE + MLA_V), 1024, BF16, name="mla_kv")
    r3 = lambda a: a.reshape(bsz, l, a.shape[-1])
    o = _attention(qf.reshape(bsz, l - n_ctx, -1), r3(kv), r3(kp))
    return _mm(o.reshape(bsz * (l - n_ctx), MLA_HEADS * MLA_V), [w_o.astype(BF16)], D_MODEL, 1024, BF16, name="mla_o")


def kernel(x, c, ctx, c_ctx, ada_w, ada_b, ln_g, ln_b, hg_w_in, hg_gnorm, hg_lb_logits, hg_w_o, mla_w_in,
           mla_q_norm, mla_kv_norm, mla_w_uq, mla_w_ukv, mla_w_o, router_w, router_b, exp_w_gu, exp_b_gu,
           exp_w_dn, exp_b_dn):
    bsz, seq, d = x.shape
    n_ctx = ctx.shape[1]
    l = n_ctx + seq
    nl = l // ROW_TILE
    ns = seq // ROW_TILE
    nc = n_ctx // ROW_TILE

    cc = jnp.zeros((8, d), F32).at[:bsz].set(c).at[bsz].set(c_ctx)
    mods = _ada(cc, ada_w, ada_b).reshape(DEPTH, 8, N_MOD, d)
    pad = jnp.zeros((DEPTH, bsz, 8 - N_MOD, d), F32)
    m_lat = jnp.concatenate([mods[:, :bsz], pad], axis=2)
    m_ctx = jnp.concatenate([jnp.broadcast_to(mods[:, bsz:bsz + 1], (DEPTH, bsz, N_MOD, d)), pad], axis=2)
    modt = jnp.stack([m_ctx, m_lat], axis=2)
    lower = jnp.cumsum(jax.nn.softmax(hg_lb_logits.astype(F32), axis=0), axis=0)

    xs = jnp.concatenate([ctx, x], axis=1).reshape(bsz * l, d)

    def map_all(r):
        per = l // r
        return lambda i: (i // per, ((i % per) >= n_ctx // r).astype(I32))

    def map_lat(r):
        per = seq // r
        return lambda i: (i // per, 1)

    y = _hgrn_layer(xs, modt[0], map_all(ROW_TILE), hg_w_in[0], hg_gnorm[0], hg_w_o[0], lower[0], bsz, l, n_ctx)
    xs, h, te, tg = _pn_router(xs, y, modt[0], ln_g[0, 0], ln_b[0, 0], router_w[0], router_b[0], bsz * l,
                               lambda i: i, map_all(ROW_TILE))
    xs = _moe(h, te, tg, xs, modt[0], map_all(MOE_TOK), ln_g[0, 1], ln_b[0, 1],
              0, exp_w_gu, exp_b_gu, exp_w_dn, exp_b_dn)

    y = _mla_layer(xs, modt[1], map_all(ROW_TILE), mla_w_in[0], mla_q_norm[0], mla_kv_norm[0], mla_w_uq[0],
                   mla_w_ukv[0], mla_w_o[0], bsz, l, n_ctx)
    xl, h, te, tg = _pn_router(xs, y, modt[1], ln_g[1, 0], ln_b[1, 0], router_w[1], router_b[1], bsz * seq,
                               lambda i: (i // ns) * nl + nc + i % ns, map_lat(ROW_TILE))
    out = _moe(h, te, tg, xl, modt[1], map_lat(MOE_TOK), ln_g[1, 1], ln_b[1, 1],
               1, exp_w_gu, exp_b_gu, exp_w_dn, exp_b_dn)
    return out.reshape(bsz, seq, d)
```

```python
import functools

import jax
import jax.numpy as jnp
from jax import lax
from jax.experimental import pallas as pl
from jax.experimental.pallas import tpu as pltpu

F32 = jnp.float32
BF16 = jnp.bfloat16
I32 = jnp.int32
U32 = jnp.uint32

D_MODEL = 2048
DEPTH = 2
GRID_W = 64
N_MOD = 6

HG_HEADS = 16
HG_DK = 128
HG_W = HG_HEADS * HG_DK
HG_SUB = 16

MLA_HEADS = 16
MLA_Q_RANK = 512
MLA_KV_RANK = 512
MLA_NOPE = 128
MLA_ROPE = 64
MLA_V = 128
MLA_QK_PAD = 256
MLA_SCALE = (MLA_NOPE + MLA_ROPE) ** -0.5
ROPE_BASE = 10000.0
LOG2_E = 1.4426950408889634

N_EXPERTS = 32
TOP_K = 4
D_EXPERT = 2048
SWIGLU_LIMIT = 7.0
SWIGLU_ALPHA = 1.702

DEEPNORM_ALPHA = (2 * DEPTH) ** 0.25
LN_EPS = 1e-5
RMS_EPS = 1e-6

LANE = 128
MXU_N = 256
ROW_TILE = 256
MOE_TM = 512
MOE_TOK = 128
MOE_DOWN_TN = 1024
ATTN_CHUNKS = 4
VMEM_LIMIT = 48 * 1024 * 1024

_HIGHEST = lax.Precision.HIGHEST


def _cparams(sem):
    return pltpu.CompilerParams(dimension_semantics=sem, vmem_limit_bytes=VMEM_LIMIT)


def _ln(x):
    mu = jnp.mean(x, axis=-1, keepdims=True)
    xc = x - mu
    var = jnp.mean(xc * xc, axis=-1, keepdims=True)
    return xc * lax.rsqrt(var + LN_EPS)


def _silu(x):
    return x * jax.nn.sigmoid(x)


def _pack_bf16_pairs(x):
    n = x.shape[1] // 2
    bits = lax.bitcast_convert_type(x.astype(BF16).astype(F32), U32)
    return (bits[:, :n] >> 16) | (bits[:, n:] & jnp.uint32(0xFFFF0000))


def _unpack_bf16_pairs(w):
    lo = lax.bitcast_convert_type(w << 16, F32).astype(BF16)
    hi = lax.bitcast_convert_type(w & jnp.uint32(0xFFFF0000), F32).astype(BF16)
    return lo, hi


def _ada_kernel(c_ref, w_ref, b_ref, o_ref):
    s = _silu(c_ref[...])
    o_ref[0] = jnp.dot(s, w_ref[0], precision=_HIGHEST, preferred_element_type=F32) + b_ref[0]


def _ada(cc, ada_w, ada_b):
    n = N_MOD * D_MODEL
    tn = 1024
    return pl.pallas_call(
        _ada_kernel,
        grid=(DEPTH, n // tn),
        in_specs=[
            pl.BlockSpec((8, D_MODEL), lambda i, j: (0, 0)),
            pl.BlockSpec((1, D_MODEL, tn), lambda i, j: (i, 0, j)),
            pl.BlockSpec((1, 1, tn), lambda i, j: (i, 0, j)),
        ],
        out_specs=pl.BlockSpec((1, 8, tn), lambda i, j: (i, 0, j)),
        out_shape=jax.ShapeDtypeStruct((DEPTH, 8, n), F32),
        compiler_params=_cparams(("parallel", "parallel")),
        name="ada",
    )(cc, ada_w, ada_b.reshape(DEPTH, 1, n))


def _lnmod_kernel(x_ref, m_ref, o_ref, *, si):
    m = m_ref[0, 0]
    o_ref[...] = (_ln(x_ref[...]) * (1.0 + m[si + 1:si + 2]) + m[si:si + 1]).astype(o_ref.dtype)


def _ln_mod(x, modt, mod_map, si):
    t = x.shape[0]
    return pl.pallas_call(
        functools.partial(_lnmod_kernel, si=si),
        grid=(t // ROW_TILE,),
        in_specs=[
            pl.BlockSpec((ROW_TILE, D_MODEL), lambda i: (i, 0)),
            pl.BlockSpec((1, 1, 8, D_MODEL), lambda i: mod_map(i) + (0, 0)),
        ],
        out_specs=pl.BlockSpec((ROW_TILE, D_MODEL), lambda i: (i, 0)),
        out_shape=jax.ShapeDtypeStruct((t, D_MODEL), BF16),
        compiler_params=_cparams(("parallel",)),
        name="ln_mod",
    )(x, modt)


def _pn_router_kernel(x_ref, y_ref, m_ref, g_ref, b_ref, rw_ref, rb_ref, xn_ref, h_ref, te_ref, tg_ref):
    m = m_ref[0, 0]
    z = DEEPNORM_ALPHA * x_ref[...] + m[2:3] * y_ref[...].astype(F32)
    xn = _ln(z) * g_ref[...] + b_ref[...]
    xn_ref[...] = xn
    h = _ln(xn) * (1.0 + m[4:5]) + m[3:4]
    h_ref[...] = _pack_bf16_pairs(h)
    logits = jnp.dot(h, rw_ref[...], precision=_HIGHEST, preferred_element_type=F32) + rb_ref[...]
    lane = lax.broadcasted_iota(I32, logits.shape, 1).astype(F32)
    neg = jnp.float32(-jnp.inf)
    cur = jnp.where(lane < N_EXPERTS, logits, neg)
    vals, idxs = [], []
    for _ in range(TOP_K):
        mx = jnp.max(cur, axis=-1, keepdims=True)
        ix = jnp.min(jnp.where(cur == mx, lane, float(LANE)), axis=-1, keepdims=True)
        vals.append(mx)
        idxs.append(ix)
        cur = jnp.where(lane == ix, neg, cur)
    es = [jnp.exp(v - vals[0]) for v in vals]
    den = es[0] + es[1] + es[2] + es[3]
    te = jnp.zeros(logits.shape, F32)
    tg = jnp.zeros(logits.shape, F32)
    for k in range(TOP_K):
        te = jnp.where(lane == k, idxs[k], te)
        tg = jnp.where(lane == k, es[k] / den, tg)
    te_ref[...] = te.astype(I32)
    tg_ref[...] = tg


def _pn_router(x, y, modt, ln_g, ln_b, rw, rb, n_tok, x_map, mod_map):
    rwp = jnp.zeros((D_MODEL, LANE), F32).at[:, :N_EXPERTS].set(rw)
    rbp = jnp.zeros((1, LANE), F32).at[0, :N_EXPERTS].set(rb)
    row = pl.BlockSpec((ROW_TILE, D_MODEL), lambda i: (i, 0))
    vec = pl.BlockSpec((1, D_MODEL), lambda i: (0, 0))
    nar = pl.BlockSpec((ROW_TILE, LANE), lambda i: (i, 0))
    return pl.pallas_call(
        _pn_router_kernel,
        grid=(n_tok // ROW_TILE,),
        in_specs=[
            pl.BlockSpec((ROW_TILE, D_MODEL), lambda i: (x_map(i), 0)),
            row,
            pl.BlockSpec((1, 1, 8, D_MODEL), lambda i: mod_map(i) + (0, 0)),
            vec, vec,
            pl.BlockSpec((D_MODEL, LANE), lambda i: (0, 0)),
            pl.BlockSpec((1, LANE), lambda i: (0, 0)),
        ],
        out_specs=[row, pl.BlockSpec((ROW_TILE, D_MODEL // 2), lambda i: (i, 0)), nar, nar],
        out_shape=[
            jax.ShapeDtypeStruct((n_tok, D_MODEL), F32),
            jax.ShapeDtypeStruct((n_tok, D_MODEL // 2), U32),
            jax.ShapeDtypeStruct((n_tok, LANE), I32),
            jax.ShapeDtypeStruct((n_tok, LANE), F32),
        ],
        compiler_params=_cparams(("parallel",)),
        name="pn_router",
    )(x, y, modt, ln_g.reshape(1, D_MODEL), ln_b.reshape(1, D_MODEL), rwp, rbp)


def _mm_kernel(*refs, n_w, n_x, epilogue):
    a = refs[0][...]
    accs = [jnp.dot(a, w[...], preferred_element_type=F32) for w in refs[1:1 + n_w]]
    extras = [e[...] for e in refs[1 + n_w:1 + n_w + n_x]]
    o_ref = refs[-1]
    o_ref[...] = epilogue(accs, extras).astype(o_ref.dtype)


def _mm(a, ws, n_out, tn, out_dtype, epilogue=None, extras=(), tm=512, w_col0=0, rows=None, name="mm"):
    m, k = a.shape
    tm = min(tm, m)
    a_map = lambda i: i
    if rows is not None:
        m, a_map = rows
    assert m % tm == 0 and n_out % tn == 0 and w_col0 % tn == 0
    if epilogue is None:
        epilogue = lambda accs, ex: accs[0]
    c0 = w_col0 // tn
    in_specs = [pl.BlockSpec((tm, k), lambda j, i: (a_map(i), 0))]
    in_specs += [pl.BlockSpec((k, tn), lambda j, i: (0, j + c0)) for _ in ws]
    in_specs += [pl.BlockSpec(bs, im) for _, bs, im in extras]
    return pl.pallas_call(
        functools.partial(_mm_kernel, n_w=len(ws), n_x=len(extras), epilogue=epilogue),
        grid=(n_out // tn, m // tm),
        in_specs=in_specs,
        out_specs=pl.BlockSpec((tm, tn), lambda j, i: (i, j)),
        out_shape=jax.ShapeDtypeStruct((m, n_out), out_dtype),
        compiler_params=_cparams(("parallel", "parallel")),
        name=name,
    )(a, *ws, *[e[0] for e in extras])


def _scan_kernel(*refs, rev, hb, tl, final):
    n_in = 8 if final else 5
    q_ref, f_ref, v_ref, lb_ref, ones_ref = refs[:5]
    o_ref = refs[n_in]
    st_ref, nat_ref, dil_ref, p_ref, upd_ref, sb_ref = refs[n_in + 1:]
    nblk = tl // HG_SUB
    q_nat, v_nat, f_nat, qs_nat, kd_nat, o_nat = (nat_ref.at[n] for n in range(6))
    dq, dk, dv, dl, dcf, dcr = (dil_ref.at[n] for n in range(6))
    db, dother = (dcr, dcf) if rev else (dcf, dcr)
    rows = lambda i: pl.ds(i, nblk, stride=HG_SUB)
    lanes = lambda h: slice(h * HG_DK, (h + 1) * HG_DK)
    pairs = [list(range(i, HG_SUB)) if rev else list(range(i + 1)) for i in range(HG_SUB)]
    bases = [sum(len(p) for p in pairs[:i]) * nblk for i in range(HG_SUB)]

    @pl.when(pl.program_id(2) == 0)
    def _():
        st_ref[...] = jnp.zeros(st_ref.shape, F32)

    vb = v_ref[0]
    tots = []
    for h in range(hb):
        lb = lb_ref[:, lanes(h)]
        q_nat[h] = q_ref[0, :, lanes(h)].astype(F32)
        v_nat[h] = vb[:, lanes(h)].astype(F32)
        f_nat[h] = f_ref[0, :, lanes(h)]
        for i in range(HG_SUB):
            forget = lb + (1.0 - lb) * jax.nn.sigmoid(f_nat[h, rows(i), :])
            dl[h, i] = jnp.log(forget)
            dk[h, i] = 1.0 - forget
            dq[h, i] = q_nat[h, rows(i), :]
            dv[h, i] = v_nat[h, rows(i), :]
        run = dl[h, 0]
        dcf[h, 0] = run
        for i in range(1, HG_SUB):
            run = run + dl[h, i]
            dcf[h, i] = run
        tots.append(run)
        run = dl[h, HG_SUB - 1]
        dcr[h, HG_SUB - 1] = run
        for i in range(HG_SUB - 2, -1, -1):
            run = run + dl[h, i]
            dcr[h, i] = run
        for i in range(HG_SUB):
            nxt = i - 1 if rev else i + 1
            rest = dother[h, nxt] if 0 <= nxt < HG_SUB else jnp.zeros((nblk, HG_DK), F32)
            qs_nat[h, rows(i), :] = dq[h, i] * jnp.exp(db[h, i])
            kd_nat[h, rows(i), :] = dk[h, i] * jnp.exp(rest)
        for i in range(HG_SUB):
            qi = dq[h, i]
            bi = db[h, i]
            for n, j in enumerate(pairs[i]):
                p_ref[pl.ds(bases[i] + n * nblk, nblk), lanes(h)] = (
                    qi * dk[h, j] * jnp.exp(bi - db[h, j])).astype(BF16)
    ones = ones_ref[...]
    hpg = MXU_N // HG_DK
    for i in range(HG_SUB):
        for g0 in range(0, hb, hpg):
            rs = jnp.dot(p_ref[pl.ds(bases[i], len(pairs[i]) * nblk), g0 * HG_DK:(g0 + hpg) * HG_DK], ones,
                         preferred_element_type=F32)
            for h in range(g0, g0 + hpg):
                oi = rs[0:nblk, lanes(h - g0)] * dv[h, pairs[i][0]]
                for n in range(1, len(pairs[i])):
                    oi = oi + rs[n * nblk:(n + 1) * nblk, lanes(h - g0)] * dv[h, pairs[i][n]]
                o_nat[h, rows(i), :] = oi

    blk = lambda j: pl.ds(j * HG_SUB, HG_SUB)
    for j in range(nblk):
        for h in range(hb):
            upd_ref[j, h] = lax.dot_general(vb[j * HG_SUB:(j + 1) * HG_SUB, lanes(h)],
                                            kd_nat[h, blk(j), :].astype(BF16),
                                            (((0,), (0,)), ((), ())), preferred_element_type=F32)
    for h in range(hb):
        st = st_ref[h]
        for j in (range(nblk - 1, -1, -1) if rev else range(nblk)):
            sb_ref[j, h] = st.astype(BF16)
            st = st * jnp.exp(tots[h][j:j + 1, :]) + upd_ref[j, h]
        st_ref[h] = st
    for j in range(nblk):
        for h in range(hb):
            oi = lax.dot_general(qs_nat[h, blk(j), :].astype(BF16), sb_ref[j, h],
                                 (((1,), (1,)), ((), ())), preferred_element_type=F32)
            if final:
                o_nat[h, blk(j), :] = o_nat[h, blk(j), :] + oi
            else:
                o_ref[0, blk(j), lanes(h)] = o_nat[h, blk(j), :] + oi

    if final:
        of_ref, g_ref, gw_ref = refs[5:8]
        for h in range(hb):
            oh = o_nat[h] + of_ref[0, :, lanes(h)]
            ms = jnp.mean(oh * oh, axis=-1, keepdims=True)
            o_ref[0, :, lanes(h)] = (oh * lax.rsqrt(ms + RMS_EPS) * gw_ref[:, lanes(h)]
                                     * g_ref[0, :, lanes(h)].astype(F32)).astype(o_ref.dtype)


def _hgrn_scan(q, fr, v, lb, rev, n_ctx, o_fwd=None, g=None, gnorm=None, hb=4, tl=256):
    bsz, l, _ = q.shape
    w = hb * HG_DK
    nl = l // tl
    nc = n_ctx // tl
    final = rev

    def lmap(i):
        if not rev:
            return i
        return jnp.where(i < nc, nc - 1 - i, nl - 1 - (i - nc))

    foff = (HG_W // w) if rev else 0
    tile = lambda b, h, i: (b, lmap(i), h)
    ones = (jnp.arange(MXU_N)[:, None] // HG_DK == jnp.arange(MXU_N)[None, :] // HG_DK).astype(BF16)
    in_specs = [
        pl.BlockSpec((1, tl, w), tile),
        pl.BlockSpec((1, tl, w), lambda b, h, i: (b, lmap(i), h + foff)),
        pl.BlockSpec((1, tl, w), tile),
        pl.BlockSpec((1, w), lambda b, h, i: (0, h)),
        pl.BlockSpec((MXU_N, MXU_N), lambda b, h, i: (0, 0)),
    ]
    args = [q, fr, v, lb.reshape(1, HG_W), ones]
    nblk = tl // HG_SUB
    n_pairs = HG_SUB * (HG_SUB + 1) // 2
    scratch = [pltpu.VMEM((hb, HG_DK, HG_DK), F32), pltpu.VMEM((6, hb, tl, HG_DK), F32),
               pltpu.VMEM((6, hb, HG_SUB, nblk, HG_DK), F32), pltpu.VMEM((n_pairs * nblk, w), BF16),
               pltpu.VMEM((nblk, hb, HG_DK, HG_DK), F32), pltpu.VMEM((nblk, hb, HG_DK, HG_DK), BF16)]
    if final:
        in_specs += [pl.BlockSpec((1, tl, w), tile), pl.BlockSpec((1, tl, w), tile),
                     pl.BlockSpec((1, w), lambda b, h, i: (0, h))]
        args += [o_fwd, g, gnorm.reshape(1, HG_W)]
    return pl.pallas_call(
        functools.partial(_scan_kernel, rev=rev, hb=hb, tl=tl, final=final),
        grid=(bsz, HG_W // w, nl),
        in_specs=in_specs,
        out_specs=pl.BlockSpec((1, tl, w), tile),
        out_shape=jax.ShapeDtypeStruct((bsz, l, HG_W), BF16 if final else F32),
        scratch_shapes=scratch,
        compiler_params=_cparams(("parallel", "parallel", "arbitrary")),
        name="hgrn_scan_bwd" if rev else "hgrn_scan_fwd",
    )(*args)


def _attn_kernel(q_ref, kn_ref, kp_ref, v_ref, o_ref, k_scr, v_scr):
    @pl.when(pl.program_id(2) == 0)
    def _():
        k_scr[:, 0:MLA_NOPE] = kn_ref[0]
        k_scr[:, MLA_NOPE:MLA_QK_PAD] = kp_ref[0]
        v_scr[:, 0:MLA_V] = v_ref[0]
        v_scr[:, MLA_V:] = jnp.ones((v_scr.shape[0], v_scr.shape[1] - MLA_V), v_scr.dtype)

    q = q_ref[0]
    l = k_scr.shape[0]
    step = -(-l // (ATTN_CHUNKS * MXU_N)) * MXU_N
    m = acc = None
    for c0 in range(0, l, step):
        rows = pl.ds(c0, min(step, l - c0))
        s = lax.dot_general(q, k_scr[rows, :], (((1,), (1,)), ((), ())), preferred_element_type=F32)
        mc = jnp.max(s, axis=-1, keepdims=True)
        m_new = mc if m is None else jnp.maximum(m, mc)
        pv = jnp.dot(jnp.exp2(s - m_new).astype(BF16), v_scr[rows, :], preferred_element_type=F32)
        acc = pv if m is None else acc * jnp.exp2(m - m_new) + pv
        m = m_new
    o_ref[0] = (acc[:, :MLA_V] / acc[:, MLA_V:MLA_V + 1]).astype(o_ref.dtype)


def _attention(qf, kv, kp, tq=512):
    bsz, s, _ = qf.shape
    l = kv.shape[1]
    tq = min(tq, s)
    assert s % tq == 0
    return pl.pallas_call(
        _attn_kernel,
        grid=(bsz, MLA_HEADS, s // tq),
        in_specs=[
            pl.BlockSpec((1, tq, MLA_QK_PAD), lambda b, h, i: (b, i, h)),
            pl.BlockSpec((1, l, MLA_NOPE), lambda b, h, i: (b, 0, 2 * h)),
            pl.BlockSpec((1, l, LANE), lambda b, h, i: (b, 0, 0)),
            pl.BlockSpec((1, l, MLA_V), lambda b, h, i: (b, 0, 2 * h + 1)),
        ],
        out_specs=pl.BlockSpec((1, tq, MLA_V), lambda b, h, i: (b, i, h)),
        out_shape=jax.ShapeDtypeStruct((bsz, s, MLA_HEADS * MLA_V), BF16),
        scratch_shapes=[pltpu.VMEM((l, MLA_QK_PAD), BF16), pltpu.VMEM((l, MXU_N), BF16)],
        compiler_params=_cparams(("parallel", "parallel", "arbitrary")),
        name="mla_attn",
    )(qf, kv, kp, kv)


def _dispatch_kernel(cnt_ref, pst_ref, dest_ref, h_hbm, xs_hbm, hbuf, lsem, rsem, psem, *, nb):
    i = pl.program_id(0)
    slot = i % 3

    def load(j, s):
        return pltpu.make_async_copy(h_hbm.at[pl.ds(j * MOE_TOK, MOE_TOK), :], hbuf.at[s], lsem.at[s])

    def row_copy(s, t, dst, sem):
        return pltpu.make_async_copy(hbuf.at[s, pl.ds(t, 1), :], xs_hbm.at[pl.ds(dst, 1), :], sem)

    def wait_rows(sem):
        for _ in range(TOP_K):
            pltpu.make_async_copy(hbuf.at[0], xs_hbm.at[pl.ds(0, MOE_TOK), :], sem).wait()

    @pl.when(i == 0)
    def _():
        for j in range(min(2, nb)):
            load(j, j).start()

    load(i, slot).wait()

    def issue(t, c):
        for k in range(TOP_K):
            row_copy(slot, t, dest_ref[0, 0, t * TOP_K + k], rsem.at[i % 2]).start()
        return c

    lax.fori_loop(0, MOE_TOK, issue, 0)

    @pl.when(i == 0)
    def _():
        for e in range(N_EXPERTS):
            first = pst_ref[e] + cnt_ref[e]
            last = pst_ref[e + 1]
            lax.fori_loop(first, last, lambda r, c: (row_copy(0, 0, r, psem).start(), c)[1], 0)
            lax.fori_loop(first, last, lambda r, c: (row_copy(0, 0, r, psem).wait(), c)[1], 0)

        def block_copy(m):
            return pltpu.make_async_copy(hbuf.at[0], xs_hbm.at[pl.ds(m * MOE_TOK, MOE_TOK), :], psem)

        used = pst_ref[N_EXPERTS] // MOE_TOK
        total = xs_hbm.shape[0] // MOE_TOK
        lax.fori_loop(used, total, lambda m, c: (block_copy(m).start(), c)[1], 0)
        lax.fori_loop(used, total, lambda m, c: (block_copy(m).wait(), c)[1], 0)

    @pl.when(i > 0)
    def _():
        wait_rows(rsem.at[(i - 1) % 2])

    @pl.when(i + 2 < nb)
    def _():
        load(i + 2, (i + 2) % 3).start()

    @pl.when(i == nb - 1)
    def _():
        wait_rows(rsem.at[i % 2])


def _dispatch(h, dest, counts, pad_start, n_rows):
    t = h.shape[0]
    nb = t // MOE_TOK
    return pl.pallas_call(
        functools.partial(_dispatch_kernel, nb=nb),
        grid_spec=pltpu.PrefetchScalarGridSpec(
            num_scalar_prefetch=2,
            grid=(nb,),
            in_specs=[
                pl.BlockSpec((1, 1, MOE_TOK * TOP_K), lambda i, c, p: (i, 0, 0), memory_space=pltpu.SMEM),
                pl.BlockSpec(memory_space=pl.ANY),
            ],
            out_specs=pl.BlockSpec(memory_space=pl.ANY),
            scratch_shapes=[pltpu.VMEM((3, MOE_TOK, h.shape[1]), h.dtype), pltpu.SemaphoreType.DMA((3,)),
                            pltpu.SemaphoreType.DMA((2,)), pltpu.SemaphoreType.DMA(())],
        ),
        out_shape=jax.ShapeDtypeStruct((n_rows, h.shape[1]), h.dtype),
        compiler_params=_cparams(("arbitrary",)),
        name="moe_dispatch",
    )(counts, pad_start, dest.reshape(nb, 1, MOE_TOK * TOP_K), h)


def _expert_switch(be_ref, nu_ref):
    m = pl.program_id(1)
    live = m < nu_ref[0]
    fresh = jnp.logical_or(m == 0, be_ref[m] != be_ref[jnp.maximum(m - 1, 0)])
    return live, jnp.logical_and(live, fresh)


def _gmm1_kernel(be_ref, nu_ref, x_ref, w_ref, p_ref, bg_ref, bl_ref, o_ref, wg_scr, wl_scr):
    live, fresh = _expert_switch(be_ref, nu_ref)
    half = MXU_N // 2

    @pl.when(fresh)
    def _():
        for c in range(w_ref.shape[3] // MXU_N):
            blk = w_ref[0, 0, :, c * MXU_N:(c + 1) * MXU_N].astype(BF16)
            sp = jnp.dot(blk, p_ref[...], preferred_element_type=F32)
            wg_scr[:, c * half:(c + 1) * half] = sp[:, :half].astype(BF16)
            wl_scr[:, c * half:(c + 1) * half] = sp[:, half:].astype(BF16)

    @pl.when(live)
    def _():
        x_lo, x_hi = _unpack_bf16_pairs(x_ref[...])
        kh = x_lo.shape[1]
        g = (jnp.dot(x_lo, wg_scr[:kh, :], preferred_element_type=F32)
             + jnp.dot(x_hi, wg_scr[kh:, :], preferred_element_type=F32) + bg_ref[0])
        u = (jnp.dot(x_lo, wl_scr[:kh, :], preferred_element_type=F32)
             + jnp.dot(x_hi, wl_scr[kh:, :], preferred_element_type=F32) + bl_ref[0])
        g = jnp.minimum(g, SWIGLU_LIMIT)
        u = jnp.clip(u, -SWIGLU_LIMIT, SWIGLU_LIMIT)
        o_ref[...] = (g * jax.nn.sigmoid(SWIGLU_ALPHA * g) * (u + 1.0)).astype(o_ref.dtype)

    @pl.when(jnp.logical_not(live))
    def _():
        o_ref[...] = jnp.zeros(o_ref.shape, o_ref.dtype)


def _gmm2_kernel(be_ref, nu_ref, x_ref, w_ref, b_ref, o_ref, w_scr):
    live, fresh = _expert_switch(be_ref, nu_ref)

    @pl.when(fresh)
    def _():
        w_scr[...] = w_ref[0, 0].astype(BF16)

    @pl.when(live)
    def _():
        o_ref[...] = _pack_bf16_pairs(jnp.dot(x_ref[...], w_scr[...], preferred_element_type=F32) + b_ref[0])

    @pl.when(jnp.logical_not(live))
    def _():
        o_ref[...] = jnp.zeros(o_ref.shape, o_ref.dtype)


def _experts(xs, block_e, n_used, layer, w_gu, bg, bl, w_dn, bd, tn=512, tn2=MOE_DOWN_TN):
    n_rows = xs.shape[0]
    nb = n_rows // MOE_TM
    idx = jnp.arange(MXU_N)
    perm = (idx[:, None] == jnp.where(idx < MXU_N // 2, 2 * idx, 2 * (idx - MXU_N // 2) + 1)[None, :]).astype(BF16)

    def mrow(n, m, be, nu):
        return (jnp.minimum(m, nu[0] - 1), 0)

    def bmap(n, m, be, nu):
        return (be[jnp.minimum(m, nu[0] - 1)], 0, n)

    def wmap(n, m, be, nu):
        return (layer, be[jnp.minimum(m, nu[0] - 1)], 0, n)

    hid = pl.pallas_call(
        _gmm1_kernel,
        grid_spec=pltpu.PrefetchScalarGridSpec(
            num_scalar_prefetch=2,
            grid=(D_EXPERT // tn, nb),
            in_specs=[
                pl.BlockSpec((MOE_TM, D_MODEL // 2), mrow),
                pl.BlockSpec((1, 1, D_MODEL, 2 * tn), wmap),
                pl.BlockSpec((MXU_N, MXU_N), lambda n, m, be, nu: (0, 0)),
                pl.BlockSpec((1, 1, tn), bmap),
                pl.BlockSpec((1, 1, tn), bmap),
            ],
            out_specs=pl.BlockSpec((MOE_TM, tn), lambda n, m, be, nu: (m, n)),
            scratch_shapes=[pltpu.VMEM((D_MODEL, tn), BF16), pltpu.VMEM((D_MODEL, tn), BF16)],
        ),
        out_shape=jax.ShapeDtypeStruct((n_rows, D_EXPERT), BF16),
        compiler_params=_cparams(("parallel", "arbitrary")),
        name="moe_gate_up",
    )(block_e, n_used, xs, w_gu, perm, bg, bl)
    return pl.pallas_call(
        _gmm2_kernel,
        grid_spec=pltpu.PrefetchScalarGridSpec(
            num_scalar_prefetch=2,
            grid=(D_MODEL // tn2, nb),
            in_specs=[
                pl.BlockSpec((MOE_TM, D_EXPERT), mrow),
                pl.BlockSpec((1, 1, D_EXPERT, tn2), wmap),
                pl.BlockSpec((1, 1, tn2), bmap),
            ],
            out_specs=pl.BlockSpec((MOE_TM, tn2 // 2), lambda n, m, be, nu: (m, n)),
            scratch_shapes=[pltpu.VMEM((D_EXPERT, tn2), BF16)],
        ),
        out_shape=jax.ShapeDtypeStruct((n_rows, D_MODEL // 2), U32),
        compiler_params=_cparams(("parallel", "arbitrary")),
        name="moe_down",
    )(block_e, n_used, hid, w_dn, bd)


def _combine_kernel(dest_ref, next_ref, gate_ref, y_hbm, x_ref, m_ref, g_ref, b_ref, o_ref, buf, sems):
    i = pl.program_id(0)
    slot = i % 2

    def gather(idx_ref, s):
        def issue(t, c):
            for k in range(TOP_K):
                pltpu.make_async_copy(y_hbm.at[pl.ds(idx_ref[0, 0, t * TOP_K + k], 1), :],
                                      buf.at[s, k, pl.ds(t, 1), :], sems.at[s]).start()
            return c

        lax.fori_loop(0, MOE_TOK, issue, 0)

    @pl.when(i == 0)
    def _():
        gather(dest_ref, 0)

    @pl.when(i + 1 < pl.num_programs(0))
    def _():
        gather(next_ref, 1 - slot)

    for k in range(TOP_K):
        pltpu.make_async_copy(y_hbm.at[pl.ds(0, MOE_TOK), :], buf.at[slot, k], sems.at[slot]).wait()
    gate = gate_ref[...]
    y = None
    for k in range(TOP_K):
        lo, hi = _unpack_bf16_pairs(buf[slot, k])
        hw = MOE_DOWN_TN // 2
        parts = []
        for c in range(lo.shape[1] // hw):
            parts += [lo[:, c * hw:(c + 1) * hw], hi[:, c * hw:(c + 1) * hw]]
        yk = gate[:, k:k + 1] * jnp.concatenate(parts, axis=1).astype(F32)
        y = yk if y is None else y + yk
    m = m_ref[0, 0]
    z = DEEPNORM_ALPHA * x_ref[...] + m[5:6] * y
    o_ref[...] = _ln(z) * g_ref[...] + b_ref[...]


def _combine_pn(y_rows, dest, gate, x, modt, ln_g, ln_b, mod_map):
    t = x.shape[0]
    nb = t // MOE_TOK
    dest3 = dest.reshape(nb, 1, MOE_TOK * TOP_K)
    row = pl.BlockSpec((MOE_TOK, D_MODEL), lambda i: (i, 0))
    vec = pl.BlockSpec((1, D_MODEL), lambda i: (0, 0))
    return pl.pallas_call(
        _combine_kernel,
        grid=(nb,),
        in_specs=[
            pl.BlockSpec((1, 1, MOE_TOK * TOP_K), lambda i: (i, 0, 0), memory_space=pltpu.SMEM),
            pl.BlockSpec((1, 1, MOE_TOK * TOP_K), lambda i: (jnp.minimum(i + 1, nb - 1), 0, 0),
                         memory_space=pltpu.SMEM),
            pl.BlockSpec((MOE_TOK, LANE), lambda i: (i, 0)),
            pl.BlockSpec(memory_space=pl.ANY),
            row,
            pl.BlockSpec((1, 1, 8, D_MODEL), lambda i: mod_map(i) + (0, 0)),
            vec, vec,
        ],
        out_specs=row,
        out_shape=jax.ShapeDtypeStruct((t, D_MODEL), F32),
        scratch_shapes=[pltpu.VMEM((2, TOP_K, MOE_TOK, D_MODEL // 2), U32), pltpu.SemaphoreType.DMA((2,))],
        compiler_params=_cparams(("arbitrary",)),
        name="moe_combine",
    )(dest3, dest3, gate, y_rows, x, modt, ln_g.reshape(1, D_MODEL), ln_b.reshape(1, D_MODEL))


def _rank_kernel(te_ref, tri_ref, rank_ref, cnt_ref, run_ref):
    @pl.when(pl.program_id(0) == 0)
    def _():
        run_ref[...] = jnp.zeros(run_ref.shape, F32)

    te = te_ref[...]
    lane = lax.broadcasted_iota(I32, te.shape, 1)
    base = run_ref[...]
    rank = jnp.zeros(te.shape, F32)
    for k in range(TOP_K):
        hit = te[:, k:k + 1] == lane
        onehot = jnp.where(hit, 1.0, 0.0).astype(BF16)
        before = jnp.dot(tri_ref[...], onehot, preferred_element_type=F32)
        rk = jnp.sum(jnp.where(hit, before + base, 0.0), axis=-1, keepdims=True)
        rank = jnp.where(lane == k, rk, rank)
        base = base + jnp.sum(onehot.astype(F32), axis=0, keepdims=True)
    run_ref[...] = base
    rank_ref[...] = rank.astype(I32)
    cnt_ref[...] = base.astype(I32)


def _ranks(top_e):
    t = top_e.shape[0]
    tri = (jnp.arange(MOE_TOK)[:, None] > jnp.arange(MOE_TOK)[None, :]).astype(BF16)
    return pl.pallas_call(
        _rank_kernel,
        grid=(t // MOE_TOK,),
        in_specs=[pl.BlockSpec((MOE_TOK, LANE), lambda i: (i, 0)), pl.BlockSpec((MOE_TOK, MOE_TOK), lambda i: (0, 0))],
        out_specs=[pl.BlockSpec((MOE_TOK, LANE), lambda i: (i, 0)), pl.BlockSpec((1, LANE), lambda i: (0, 0))],
        out_shape=[jax.ShapeDtypeStruct((t, LANE), I32), jax.ShapeDtypeStruct((1, LANE), I32)],
        scratch_shapes=[pltpu.VMEM((1, LANE), F32)],
        compiler_params=_cparams(("arbitrary",)),
        name="moe_rank",
    )(top_e, tri)


def _moe(h, top_e, gate, x, modt, mod_map, ln_g, ln_b, layer, w_gu, b_gu, w_dn, b_dn):
    t = h.shape[0]
    n_assign = t * TOP_K
    nb = n_assign // MOE_TM + N_EXPERTS
    rank, cnt = _ranks(top_e)
    counts = cnt[0, :N_EXPERTS]
    padded = (counts + MOE_TM - 1) // MOE_TM * MOE_TM
    pad_end = jnp.cumsum(padded)
    pad_start = jnp.concatenate([jnp.zeros((1,), I32), pad_end]).astype(I32)
    te4 = top_e[:, :TOP_K]
    hit = te4[:, :, None] == jnp.arange(N_EXPERTS, dtype=I32)[None, None, :]
    dest = (jnp.sum(jnp.where(hit, pad_start[None, None, :N_EXPERTS], 0), axis=-1) + rank[:, :TOP_K]).astype(I32)
    starts = jnp.arange(nb, dtype=I32) * MOE_TM
    block_e = jnp.minimum(jnp.sum((pad_end[None, :] <= starts[:, None]).astype(I32), axis=1), N_EXPERTS - 1)
    n_used = (pad_end[-1:] // MOE_TM).astype(I32)

    xs = _dispatch(h, dest, counts.astype(I32), pad_start, nb * MOE_TM)
    bg = b_gu[layer, :, None, 0::2]
    bl = b_gu[layer, :, None, 1::2]
    y_rows = _experts(xs, block_e, n_used, layer, w_gu, bg, bl, w_dn, b_dn[layer, :, None, :])
    return _combine_pn(y_rows, dest, gate, x, modt, ln_g, ln_b, mod_map)


def _rope_tables(n_ctx, seq):
    pos = jnp.arange(seq)
    row = (pos // GRID_W).astype(F32)
    col = (pos % GRID_W).astype(F32)
    n_freq = MLA_ROPE // 4
    freqs = ROPE_BASE ** (-jnp.arange(n_freq, dtype=F32) / n_freq)
    ar = row[:, None] * freqs
    ac = col[:, None] * freqs
    cos = jnp.concatenate([jnp.cos(ar), jnp.cos(ar), jnp.cos(ac), jnp.cos(ac)], axis=-1)
    sin = jnp.concatenate([jnp.sin(ar), jnp.sin(ar), jnp.sin(ac), jnp.sin(ac)], axis=-1)
    cos = jnp.concatenate([jnp.ones((n_ctx, MLA_ROPE), F32), cos], axis=0)
    sin = jnp.concatenate([jnp.zeros((n_ctx, MLA_ROPE), F32), sin], axis=0)
    return cos, sin


def _rot_cols(w):
    q = MLA_ROPE // 4
    a, b, c, d = w[..., :q], w[..., q:2 * q], w[..., 2 * q:3 * q], w[..., 3 * q:]
    return jnp.concatenate([-b, a, -d, c], axis=-1)


def _rmsnorm_epilogue(accs, ex):
    x = accs[0]
    return x * lax.rsqrt(jnp.mean(x * x, axis=-1, keepdims=True) + RMS_EPS) * ex[0]


def _hgrn_layer(xs, modt, mod_map, w_in, gnorm, w_o, lbs, bsz, l, n_ctx):
    h = _ln_mod(xs, modt, mod_map, 0)
    wb = w_in.astype(BF16)
    silu_ep = lambda accs, ex: _silu(accs[0])
    q = _mm(h, [wb], HG_W, 1024, BF16, silu_ep, w_col0=0, name="hg_q")
    fr = _mm(h, [wb], 2 * HG_W, 1024, F32, w_col0=HG_W, name="hg_f")
    v = _mm(h, [wb], HG_W, 1024, BF16, w_col0=3 * HG_W, name="hg_v")
    g = _mm(h, [wb], HG_W, 1024, BF16, silu_ep, w_col0=4 * HG_W, name="hg_g")
    r3 = lambda a: a.reshape(bsz, l, a.shape[-1])
    o_f = _hgrn_scan(r3(q), r3(fr), r3(v), lbs[0], False, n_ctx)
    o = _hgrn_scan(r3(q), r3(fr), r3(v), lbs[1], True, n_ctx, o_fwd=o_f, g=r3(g), gnorm=gnorm)
    return _mm(o.reshape(bsz * l, HG_W), [w_o.astype(BF16)], D_MODEL, 1024, BF16, name="hg_o")


def _mla_layer(xs, modt, mod_map, w_in, q_norm, kv_norm, w_uq, w_ukv, w_o, bsz, l, n_ctx):
    h = _ln_mod(xs, modt, mod_map, 0)
    wb = w_in.astype(BF16)
    nl = l // ROW_TILE
    vec = lambda n: ((1, n), lambda j, i: (0, 0))
    cq = _mm(h, [wb], MLA_Q_RANK, 512, BF16, _rmsnorm_epilogue,
             extras=[(q_norm.reshape(1, -1),) + vec(MLA_Q_RANK)], tm=ROW_TILE, w_col0=0, name="mla_cq")
    ckv = _mm(h, [wb], MLA_KV_RANK, 512, BF16, _rmsnorm_epilogue,
              extras=[(kv_norm.reshape(1, -1),) + vec(MLA_KV_RANK)], tm=ROW_TILE, w_col0=MLA_Q_RANK, name="mla_ckv")
    cos, sin = _rope_tables(n_ctx, l - n_ctx)
    rope_ep = lambda accs, ex: accs[0] * ex[0] + accs[1] * ex[1]
    tab = lambda n: ((ROW_TILE, n), lambda j, i: (i % nl, 0))
    ns, nc = (l - n_ctx) // ROW_TILE, n_ctx // ROW_TILE
    qtab = lambda n: ((ROW_TILE, n), lambda j, i: (nc + i % ns, 0))
    lat_rows = (bsz * (l - n_ctx), lambda i: (i // ns) * nl + nc + i % ns)
    w_kp = w_in[:, MLA_Q_RANK + MLA_KV_RANK:]
    zk = jnp.zeros((D_MODEL, LANE - MLA_ROPE), F32)
    zt = jnp.zeros((l, LANE - MLA_ROPE), F32)
    kp = _mm(h, [jnp.concatenate([w_kp, zk], 1).astype(BF16), jnp.concatenate([_rot_cols(w_kp), zk], 1).astype(BF16)],
             LANE, LANE, BF16, rope_ep,
             extras=[(jnp.concatenate([cos, zt], 1),) + tab(LANE), (jnp.concatenate([sin, zt], 1),) + tab(LANE)],
             tm=ROW_TILE, name="mla_kp")
    wq = w_uq.reshape(MLA_Q_RANK, MLA_HEADS, MLA_NOPE + MLA_ROPE)
    zq = jnp.zeros((MLA_Q_RANK, MLA_HEADS, MLA_QK_PAD - MLA_NOPE - MLA_ROPE), F32)
    wqa = jnp.concatenate([wq, zq], -1).reshape(MLA_Q_RANK, -1).astype(BF16)
    wqb = jnp.concatenate([jnp.zeros_like(wq[..., :MLA_NOPE]), _rot_cols(wq[..., MLA_NOPE:]), zq], -1)
    wqb = wqb.reshape(MLA_Q_RANK, -1).astype(BF16)
    zt = jnp.zeros((l, MLA_QK_PAD - MLA_NOPE - MLA_ROPE), F32)
    q_scale = MLA_SCALE * LOG2_E
    cq_tab = jnp.concatenate([jnp.ones((l, MLA_NOPE), F32), cos, zt], 1) * q_scale
    sq_tab = jnp.concatenate([jnp.zeros((l, MLA_NOPE), F32), sin, zt], 1) * q_scale
    hq = 8
    qf = _mm(cq, [wqa, wqb], MLA_HEADS * MLA_QK_PAD, hq * MLA_QK_PAD, BF16, rope_ep,
             extras=[(jnp.tile(cq_tab, (1, hq)),) + qtab(hq * MLA_QK_PAD),
                     (jnp.tile(sq_tab, (1, hq)),) + qtab(hq * MLA_QK_PAD)], tm=ROW_TILE, rows=lat_rows,
             name="mla_q")
    kv = _mm(ckv, [w_ukv.astype(BF16)], MLA_HEADS * (MLA_NOPE + MLA_V), 1024, BF16, name="mla_kv")
    r3 = lambda a: a.reshape(bsz, l, a.shape[-1])
    o = _attention(qf.reshape(bsz, l - n_ctx, -1), r3(kv), r3(kp))
    return _mm(o.reshape(bsz * (l - n_ctx), MLA_HEADS * MLA_V), [w_o.astype(BF16)], D_MODEL, 1024, BF16, name="mla_o")


def kernel(x, c, ctx, c_ctx, ada_w, ada_b, ln_g, ln_b, hg_w_in, hg_gnorm, hg_lb_logits, hg_w_o, mla_w_in,
           mla_q_norm, mla_kv_norm, mla_w_uq, mla_w_ukv, mla_w_o, router_w, router_b, exp_w_gu, exp_b_gu,
           exp_w_dn, exp_b_dn):
    bsz, seq, d = x.shape
    n_ctx = ctx.shape[1]
    l = n_ctx + seq
    nl = l // ROW_TILE
    ns = seq // ROW_TILE
    nc = n_ctx // ROW_TILE

    cc = jnp.zeros((8, d), F32).at[:bsz].set(c).at[bsz].set(c_ctx)
    mods = _ada(cc, ada_w, ada_b).reshape(DEPTH, 8, N_MOD, d)
    pad = jnp.zeros((DEPTH, bsz, 8 - N_MOD, d), F32)
    m_lat = jnp.concatenate([mods[:, :bsz], pad], axis=2)
    m_ctx = jnp.concatenate([jnp.broadcast_to(mods[:, bsz:bsz + 1], (DEPTH, bsz, N_MOD, d)), pad], axis=2)
    modt = jnp.stack([m_ctx, m_lat], axis=2)
    lower = jnp.cumsum(jax.nn.softmax(hg_lb_logits.astype(F32), axis=0), axis=0)

    xs = jnp.concatenate([ctx, x], axis=1).reshape(bsz * l, d)

    def map_all(r):
        per = l // r
        return lambda i: (i // per, ((i % per) >= n_ctx // r).astype(I32))

    def map_lat(r):
        per = seq // r
        return lambda i: (i // per, 1)

    y = _hgrn_layer(xs, modt[0], map_all(ROW_TILE), hg_w_in[0], hg_gnorm[0], hg_w_o[0], lower[0], bsz, l, n_ctx)
    xs, h, te, tg = _pn_router(xs, y, modt[0], ln_g[0, 0], ln_b[0, 0], router_w[0], router_b[0], bsz * l,
                               lambda i: i, map_all(ROW_TILE))
    xs = _moe(h, te, tg, xs, modt[0], map_all(MOE_TOK), ln_g[0, 1], ln_b[0, 1],
              0, exp_w_gu, exp_b_gu, exp_w_dn, exp_b_dn)

    y = _mla_layer(xs, modt[1], map_all(ROW_TILE), mla_w_in[0], mla_q_norm[0], mla_kv_norm[0], mla_w_uq[0],
                   mla_w_ukv[0], mla_w_o[0], bsz, l, n_ctx)
    xl, h, te, tg = _pn_router(xs, y, modt[1], ln_g[1, 0], ln_b[1, 0], router_w[1], router_b[1], bsz * seq,
                               lambda i: (i // ns) * nl + nc + i % ns, map_lat(ROW_TILE))
    out = _moe(h, te, tg, xl, modt[1], map_lat(MOE_TOK), ln_g[1, 1], ln_b[1, 1],
               1, exp_w_gu, exp_b_gu, exp_w_dn, exp_b_dn)
    return out.reshape(bsz, seq, d)
```

```python
import functools

import jax
import jax.numpy as jnp
from jax import lax
from jax.experimental import pallas as pl
from jax.experimental.pallas import tpu as pltpu

F32 = jnp.float32
BF16 = jnp.bfloat16
I32 = jnp.int32
U32 = jnp.uint32

D_MODEL = 2048
DEPTH = 2
GRID_W = 64
N_MOD = 6

HG_HEADS = 16
HG_DK = 128
HG_W = HG_HEADS * HG_DK
HG_SUB = 16

MLA_HEADS = 16
MLA_Q_RANK = 512
MLA_KV_RANK = 512
MLA_NOPE = 128
MLA_ROPE = 64
MLA_V = 128
MLA_QK_PAD = 256
MLA_SCALE = (MLA_NOPE + MLA_ROPE) ** -0.5
ROPE_BASE = 10000.0
LOG2_E = 1.4426950408889634

N_EXPERTS = 32
TOP_K = 4
D_EXPERT = 2048
SWIGLU_LIMIT = 7.0
SWIGLU_ALPHA = 1.702

DEEPNORM_ALPHA = (2 * DEPTH) ** 0.25
LN_EPS = 1e-5
RMS_EPS = 1e-6

LANE = 128
MXU_N = 256
ROW_TILE = 256
MOE_TM = 512
MOE_TOK = 128
MOE_DOWN_TN = 2048
ATTN_CHUNKS = 4
VMEM_LIMIT = 48 * 1024 * 1024
VMEM_LIMIT_EXPERTS = 60 * 1024 * 1024

_HIGHEST = lax.Precision.HIGHEST


def _cparams(sem, vmem=VMEM_LIMIT):
    return pltpu.CompilerParams(dimension_semantics=sem, vmem_limit_bytes=vmem)


def _ln(x):
    mu = jnp.mean(x, axis=-1, keepdims=True)
    xc = x - mu
    var = jnp.mean(xc * xc, axis=-1, keepdims=True)
    return xc * lax.rsqrt(var + LN_EPS)


def _silu(x):
    return x * jax.nn.sigmoid(x)


def _pack_bf16_pairs(x):
    n = x.shape[1] // 2
    bits = lax.bitcast_convert_type(x.astype(BF16).astype(F32), U32)
    return (bits[:, :n] >> 16) | (bits[:, n:] & jnp.uint32(0xFFFF0000))


def _unpack_bf16_pairs(w):
    lo = lax.bitcast_convert_type(w << 16, F32).astype(BF16)
    hi = lax.bitcast_convert_type(w & jnp.uint32(0xFFFF0000), F32).astype(BF16)
    return lo, hi


def _ada_kernel(c_ref, w_ref, b_ref, o_ref):
    s = _silu(c_ref[...])
    o_ref[0] = jnp.dot(s, w_ref[0], precision=_HIGHEST, preferred_element_type=F32) + b_ref[0]


def _ada(cc, ada_w, ada_b):
    n = N_MOD * D_MODEL
    tn = 1024
    return pl.pallas_call(
        _ada_kernel,
        grid=(DEPTH, n // tn),
        in_specs=[
            pl.BlockSpec((8, D_MODEL), lambda i, j: (0, 0)),
            pl.BlockSpec((1, D_MODEL, tn), lambda i, j: (i, 0, j)),
            pl.BlockSpec((1, 1, tn), lambda i, j: (i, 0, j)),
        ],
        out_specs=pl.BlockSpec((1, 8, tn), lambda i, j: (i, 0, j)),
        out_shape=jax.ShapeDtypeStruct((DEPTH, 8, n), F32),
        compiler_params=_cparams(("parallel", "parallel")),
        name="ada",
    )(cc, ada_w, ada_b.reshape(DEPTH, 1, n))


def _lnmod_kernel(x_ref, m_ref, o_ref, *, si):
    m = m_ref[0, 0]
    o_ref[...] = (_ln(x_ref[...]) * (1.0 + m[si + 1:si + 2]) + m[si:si + 1]).astype(o_ref.dtype)


def _ln_mod(x, modt, mod_map, si):
    t = x.shape[0]
    return pl.pallas_call(
        functools.partial(_lnmod_kernel, si=si),
        grid=(t // ROW_TILE,),
        in_specs=[
            pl.BlockSpec((ROW_TILE, D_MODEL), lambda i: (i, 0)),
            pl.BlockSpec((1, 1, 8, D_MODEL), lambda i: mod_map(i) + (0, 0)),
        ],
        out_specs=pl.BlockSpec((ROW_TILE, D_MODEL), lambda i: (i, 0)),
        out_shape=jax.ShapeDtypeStruct((t, D_MODEL), BF16),
        compiler_params=_cparams(("parallel",)),
        name="ln_mod",
    )(x, modt)


def _pn_router_kernel(x_ref, y_ref, m_ref, g_ref, b_ref, rw_ref, rb_ref, xn_ref, h_ref, te_ref, tg_ref):
    m = m_ref[0, 0]
    z = DEEPNORM_ALPHA * x_ref[...] + m[2:3] * y_ref[...].astype(F32)
    xn = _ln(z) * g_ref[...] + b_ref[...]
    xn_ref[...] = xn
    h = _ln(xn) * (1.0 + m[4:5]) + m[3:4]
    h_ref[...] = _pack_bf16_pairs(h)
    logits = jnp.dot(h, rw_ref[...], precision=_HIGHEST, preferred_element_type=F32) + rb_ref[...]
    lane = lax.broadcasted_iota(I32, logits.shape, 1).astype(F32)
    neg = jnp.float32(-jnp.inf)
    cur = jnp.where(lane < N_EXPERTS, logits, neg)
    vals, idxs = [], []
    for _ in range(TOP_K):
        mx = jnp.max(cur, axis=-1, keepdims=True)
        ix = jnp.min(jnp.where(cur == mx, lane, float(LANE)), axis=-1, keepdims=True)
        vals.append(mx)
        idxs.append(ix)
        cur = jnp.where(lane == ix, neg, cur)
    es = [jnp.exp(v - vals[0]) for v in vals]
    den = es[0] + es[1] + es[2] + es[3]
    te = jnp.zeros(logits.shape, F32)
    tg = jnp.zeros(logits.shape, F32)
    for k in range(TOP_K):
        te = jnp.where(lane == k, idxs[k], te)
        tg = jnp.where(lane == k, es[k] / den, tg)
    te_ref[...] = te.astype(I32)
    tg_ref[...] = tg


def _pn_router(x, y, modt, ln_g, ln_b, rw, rb, n_tok, x_map, mod_map):
    rwp = jnp.zeros((D_MODEL, LANE), F32).at[:, :N_EXPERTS].set(rw)
    rbp = jnp.zeros((1, LANE), F32).at[0, :N_EXPERTS].set(rb)
    row = pl.BlockSpec((ROW_TILE, D_MODEL), lambda i: (i, 0))
    vec = pl.BlockSpec((1, D_MODEL), lambda i: (0, 0))
    nar = pl.BlockSpec((ROW_TILE, LANE), lambda i: (i, 0))
    return pl.pallas_call(
        _pn_router_kernel,
        grid=(n_tok // ROW_TILE,),
        in_specs=[
            pl.BlockSpec((ROW_TILE, D_MODEL), lambda i: (x_map(i), 0)),
            row,
            pl.BlockSpec((1, 1, 8, D_MODEL), lambda i: mod_map(i) + (0, 0)),
            vec, vec,
            pl.BlockSpec((D_MODEL, LANE), lambda i: (0, 0)),
            pl.BlockSpec((1, LANE), lambda i: (0, 0)),
        ],
        out_specs=[row, pl.BlockSpec((ROW_TILE, D_MODEL // 2), lambda i: (i, 0)), nar, nar],
        out_shape=[
            jax.ShapeDtypeStruct((n_tok, D_MODEL), F32),
            jax.ShapeDtypeStruct((n_tok, D_MODEL // 2), U32),
            jax.ShapeDtypeStruct((n_tok, LANE), I32),
            jax.ShapeDtypeStruct((n_tok, LANE), F32),
        ],
        compiler_params=_cparams(("parallel",)),
        name="pn_router",
    )(x, y, modt, ln_g.reshape(1, D_MODEL), ln_b.reshape(1, D_MODEL), rwp, rbp)


def _mm_kernel(*refs, n_w, n_x, epilogue):
    a = refs[0][...]
    accs = [jnp.dot(a, w[...], preferred_element_type=F32) for w in refs[1:1 + n_w]]
    extras = [e[...] for e in refs[1 + n_w:1 + n_w + n_x]]
    o_ref = refs[-1]
    o_ref[...] = epilogue(accs, extras).astype(o_ref.dtype)


def _mm(a, ws, n_out, tn, out_dtype, epilogue=None, extras=(), tm=512, w_col0=0, rows=None, name="mm"):
    m, k = a.shape
    tm = min(tm, m)
    a_map = lambda i: i
    if rows is not None:
        m, a_map = rows
    assert m % tm == 0 and n_out % tn == 0 and w_col0 % tn == 0
    if epilogue is None:
        epilogue = lambda accs, ex: accs[0]
    c0 = w_col0 // tn
    in_specs = [pl.BlockSpec((tm, k), lambda j, i: (a_map(i), 0))]
    in_specs += [pl.BlockSpec((k, tn), lambda j, i: (0, j + c0)) for _ in ws]
    in_specs += [pl.BlockSpec(bs, im) for _, bs, im in extras]
    return pl.pallas_call(
        functools.partial(_mm_kernel, n_w=len(ws), n_x=len(extras), epilogue=epilogue),
        grid=(n_out // tn, m // tm),
        in_specs=in_specs,
        out_specs=pl.BlockSpec((tm, tn), lambda j, i: (i, j)),
        out_shape=jax.ShapeDtypeStruct((m, n_out), out_dtype),
        compiler_params=_cparams(("parallel", "parallel")),
        name=name,
    )(a, *ws, *[e[0] for e in extras])


def _scan_kernel(*refs, rev, hb, tl, final):
    n_in = 8 if final else 5
    q_ref, f_ref, v_ref, lb_ref, ones_ref = refs[:5]
    o_ref = refs[n_in]
    st_ref, nat_ref, dil_ref, p_ref, upd_ref, sb_ref = refs[n_in + 1:]
    nblk = tl // HG_SUB
    q_nat, v_nat, f_nat, qs_nat, kd_nat, o_nat = (nat_ref.at[n] for n in range(6))
    dq, dk, dv, dl, dcf, dcr = (dil_ref.at[n] for n in range(6))
    db, dother = (dcr, dcf) if rev else (dcf, dcr)
    rows = lambda i: pl.ds(i, nblk, stride=HG_SUB)
    lanes = lambda h: slice(h * HG_DK, (h + 1) * HG_DK)
    pairs = [list(range(i, HG_SUB)) if rev else list(range(i + 1)) for i in range(HG_SUB)]
    bases = [sum(len(p) for p in pairs[:i]) * nblk for i in range(HG_SUB)]

    @pl.when(pl.program_id(2) == 0)
    def _():
        st_ref[...] = jnp.zeros(st_ref.shape, F32)

    vb = v_ref[0]
    tots = []
    for h in range(hb):
        lb = lb_ref[:, lanes(h)]
        q_nat[h] = q_ref[0, :, lanes(h)].astype(F32)
        v_nat[h] = vb[:, lanes(h)].astype(F32)
        f_nat[h] = f_ref[0, :, lanes(h)]
        for i in range(HG_SUB):
            forget = lb + (1.0 - lb) * jax.nn.sigmoid(f_nat[h, rows(i), :])
            dl[h, i] = jnp.log(forget)
            dk[h, i] = 1.0 - forget
            dq[h, i] = q_nat[h, rows(i), :]
            dv[h, i] = v_nat[h, rows(i), :]
        run = dl[h, 0]
        dcf[h, 0] = run
        for i in range(1, HG_SUB):
            run = run + dl[h, i]
            dcf[h, i] = run
        tots.append(run)
        run = dl[h, HG_SUB - 1]
        dcr[h, HG_SUB - 1] = run
        for i in range(HG_SUB - 2, -1, -1):
            run = run + dl[h, i]
            dcr[h, i] = run
        for i in range(HG_SUB):
            nxt = i - 1 if rev else i + 1
            rest = dother[h, nxt] if 0 <= nxt < HG_SUB else jnp.zeros((nblk, HG_DK), F32)
            qs_nat[h, rows(i), :] = dq[h, i] * jnp.exp(db[h, i])
            kd_nat[h, rows(i), :] = dk[h, i] * jnp.exp(rest)
        for i in range(HG_SUB):
            qi = dq[h, i]
            bi = db[h, i]
            for n, j in enumerate(pairs[i]):
                p_ref[pl.ds(bases[i] + n * nblk, nblk), lanes(h)] = (
                    qi * dk[h, j] * jnp.exp(bi - db[h, j])).astype(BF16)
    ones = ones_ref[...]
    hpg = MXU_N // HG_DK
    for i in range(HG_SUB):
        for g0 in range(0, hb, hpg):
            rs = jnp.dot(p_ref[pl.ds(bases[i], len(pairs[i]) * nblk), g0 * HG_DK:(g0 + hpg) * HG_DK], ones,
                         preferred_element_type=F32)
            for h in range(g0, g0 + hpg):
                oi = rs[0:nblk, lanes(h - g0)] * dv[h, pairs[i][0]]
                for n in range(1, len(pairs[i])):
                    oi = oi + rs[n * nblk:(n + 1) * nblk, lanes(h - g0)] * dv[h, pairs[i][n]]
                o_nat[h, rows(i), :] = oi

    blk = lambda j: pl.ds(j * HG_SUB, HG_SUB)
    for j in range(nblk):
        for h in range(hb):
            upd_ref[j, h] = lax.dot_general(vb[j * HG_SUB:(j + 1) * HG_SUB, lanes(h)],
                                            kd_nat[h, blk(j), :].astype(BF16),
                                            (((0,), (0,)), ((), ())), preferred_element_type=F32)
    for h in range(hb):
        st = st_ref[h]
        for j in (range(nblk - 1, -1, -1) if rev else range(nblk)):
            sb_ref[j, h] = st.astype(BF16)
            st = st * jnp.exp(tots[h][j:j + 1, :]) + upd_ref[j, h]
        st_ref[h] = st
    for j in range(nblk):
        for h in range(hb):
            oi = lax.dot_general(qs_nat[h, blk(j), :].astype(BF16), sb_ref[j, h],
                                 (((1,), (1,)), ((), ())), preferred_element_type=F32)
            if final:
                o_nat[h, blk(j), :] = o_nat[h, blk(j), :] + oi
            else:
                o_ref[0, blk(j), lanes(h)] = o_nat[h, blk(j), :] + oi

    if final:
        of_ref, g_ref, gw_ref = refs[5:8]
        for h in range(hb):
            oh = o_nat[h] + of_ref[0, :, lanes(h)]
            ms = jnp.mean(oh * oh, axis=-1, keepdims=True)
            o_ref[0, :, lanes(h)] = (oh * lax.rsqrt(ms + RMS_EPS) * gw_ref[:, lanes(h)]
                                     * g_ref[0, :, lanes(h)].astype(F32)).astype(o_ref.dtype)


def _hgrn_scan(q, fr, v, lb, rev, n_ctx, o_fwd=None, g=None, gnorm=None, hb=4, tl=256):
    bsz, l, _ = q.shape
    w = hb * HG_DK
    nl = l // tl
    nc = n_ctx // tl
    final = rev

    def lmap(i):
        if not rev:
            return i
        return jnp.where(i < nc, nc - 1 - i, nl - 1 - (i - nc))

    foff = (HG_W // w) if rev else 0
    tile = lambda b, h, i: (b, lmap(i), h)
    ones = (jnp.arange(MXU_N)[:, None] // HG_DK == jnp.arange(MXU_N)[None, :] // HG_DK).astype(BF16)
    in_specs = [
        pl.BlockSpec((1, tl, w), tile),
        pl.BlockSpec((1, tl, w), lambda b, h, i: (b, lmap(i), h + foff)),
        pl.BlockSpec((1, tl, w), tile),
        pl.BlockSpec((1, w), lambda b, h, i: (0, h)),
        pl.BlockSpec((MXU_N, MXU_N), lambda b, h, i: (0, 0)),
    ]
    args = [q, fr, v, lb.reshape(1, HG_W), ones]
    nblk = tl // HG_SUB
    n_pairs = HG_SUB * (HG_SUB + 1) // 2
    scratch = [pltpu.VMEM((hb, HG_DK, HG_DK), F32), pltpu.VMEM((6, hb, tl, HG_DK), F32),
               pltpu.VMEM((6, hb, HG_SUB, nblk, HG_DK), F32), pltpu.VMEM((n_pairs * nblk, w), BF16),
               pltpu.VMEM((nblk, hb, HG_DK, HG_DK), F32), pltpu.VMEM((nblk, hb, HG_DK, HG_DK), BF16)]
    if final:
        in_specs += [pl.BlockSpec((1, tl, w), tile), pl.BlockSpec((1, tl, w), tile),
                     pl.BlockSpec((1, w), lambda b, h, i: (0, h))]
        args += [o_fwd, g, gnorm.reshape(1, HG_W)]
    return pl.pallas_call(
        functools.partial(_scan_kernel, rev=rev, hb=hb, tl=tl, final=final),
        grid=(bsz, HG_W // w, nl),
        in_specs=in_specs,
        out_specs=pl.BlockSpec((1, tl, w), tile),
        out_shape=jax.ShapeDtypeStruct((bsz, l, HG_W), BF16 if final else F32),
        scratch_shapes=scratch,
        compiler_params=_cparams(("parallel", "parallel", "arbitrary")),
        name="hgrn_scan_bwd" if rev else "hgrn_scan_fwd",
    )(*args)


def _attn_kernel(q_ref, kn_ref, kp_ref, v_ref, o_ref, k_scr, v_scr):
    @pl.when(pl.program_id(2) == 0)
    def _():
        k_scr[:, 0:MLA_NOPE] = kn_ref[0]
        k_scr[:, MLA_NOPE:MLA_QK_PAD] = kp_ref[0]
        v_scr[:, 0:MLA_V] = v_ref[0]
        v_scr[:, MLA_V:] = jnp.ones((v_scr.shape[0], v_scr.shape[1] - MLA_V), v_scr.dtype)

    q = q_ref[0]
    l = k_scr.shape[0]
    step = -(-l // (ATTN_CHUNKS * MXU_N)) * MXU_N
    m = acc = None
    for c0 in range(0, l, step):
        rows = pl.ds(c0, min(step, l - c0))
        s = lax.dot_general(q, k_scr[rows, :], (((1,), (1,)), ((), ())), preferred_element_type=F32)
        mc = jnp.max(s, axis=-1, keepdims=True)
        m_new = mc if m is None else jnp.maximum(m, mc)
        pv = jnp.dot(jnp.exp2(s - m_new).astype(BF16), v_scr[rows, :], preferred_element_type=F32)
        acc = pv if m is None else acc * jnp.exp2(m - m_new) + pv
        m = m_new
    o_ref[0] = (acc[:, :MLA_V] / acc[:, MLA_V:MLA_V + 1]).astype(o_ref.dtype)


def _attention(qf, kv, kp, tq=512):
    bsz, s, _ = qf.shape
    l = kv.shape[1]
    tq = min(tq, s)
    assert s % tq == 0
    return pl.pallas_call(
        _attn_kernel,
        grid=(bsz, MLA_HEADS, s // tq),
        in_specs=[
            pl.BlockSpec((1, tq, MLA_QK_PAD), lambda b, h, i: (b, i, h)),
            pl.BlockSpec((1, l, MLA_NOPE), lambda b, h, i: (b, 0, 2 * h)),
            pl.BlockSpec((1, l, LANE), lambda b, h, i: (b, 0, 0)),
            pl.BlockSpec((1, l, MLA_V), lambda b, h, i: (b, 0, 2 * h + 1)),
        ],
        out_specs=pl.BlockSpec((1, tq, MLA_V), lambda b, h, i: (b, i, h)),
        out_shape=jax.ShapeDtypeStruct((bsz, s, MLA_HEADS * MLA_V), BF16),
        scratch_shapes=[pltpu.VMEM((l, MLA_QK_PAD), BF16), pltpu.VMEM((l, MXU_N), BF16)],
        compiler_params=_cparams(("parallel", "parallel", "arbitrary")),
        name="mla_attn",
    )(qf, kv, kp, kv)


def _dispatch_kernel(cnt_ref, pst_ref, dest_ref, h_hbm, xs_hbm, hbuf, lsem, rsem, psem, *, nb):
    i = pl.program_id(0)
    slot = i % 3

    def load(j, s):
        return pltpu.make_async_copy(h_hbm.at[pl.ds(j * MOE_TOK, MOE_TOK), :], hbuf.at[s], lsem.at[s])

    def row_copy(s, t, dst, sem):
        return pltpu.make_async_copy(hbuf.at[s, pl.ds(t, 1), :], xs_hbm.at[pl.ds(dst, 1), :], sem)

    def wait_rows(sem):
        for _ in range(TOP_K):
            pltpu.make_async_copy(hbuf.at[0], xs_hbm.at[pl.ds(0, MOE_TOK), :], sem).wait()

    @pl.when(i == 0)
    def _():
        for j in range(min(2, nb)):
            load(j, j).start()

    load(i, slot).wait()

    def issue(t, c):
        for k in range(TOP_K):
            row_copy(slot, t, dest_ref[0, 0, t * TOP_K + k], rsem.at[i % 2]).start()
        return c

    lax.fori_loop(0, MOE_TOK, issue, 0)

    @pl.when(i == 0)
    def _():
        for e in range(N_EXPERTS):
            first = pst_ref[e] + cnt_ref[e]
            last = pst_ref[e + 1]
            lax.fori_loop(first, last, lambda r, c: (row_copy(0, 0, r, psem).start(), c)[1], 0)
            lax.fori_loop(first, last, lambda r, c: (row_copy(0, 0, r, psem).wait(), c)[1], 0)

        def block_copy(m):
            return pltpu.make_async_copy(hbuf.at[0], xs_hbm.at[pl.ds(m * MOE_TOK, MOE_TOK), :], psem)

        used = pst_ref[N_EXPERTS] // MOE_TOK
        total = xs_hbm.shape[0] // MOE_TOK
        lax.fori_loop(used, total, lambda m, c: (block_copy(m).start(), c)[1], 0)
        lax.fori_loop(used, total, lambda m, c: (block_copy(m).wait(), c)[1], 0)

    @pl.when(i > 0)
    def _():
        wait_rows(rsem.at[(i - 1) % 2])

    @pl.when(i + 2 < nb)
    def _():
        load(i + 2, (i + 2) % 3).start()

    @pl.when(i == nb - 1)
    def _():
        wait_rows(rsem.at[i % 2])


def _dispatch(h, dest, counts, pad_start, n_rows):
    t = h.shape[0]
    nb = t // MOE_TOK
    return pl.pallas_call(
        functools.partial(_dispatch_kernel, nb=nb),
        grid_spec=pltpu.PrefetchScalarGridSpec(
            num_scalar_prefetch=2,
            grid=(nb,),
            in_specs=[
                pl.BlockSpec((1, 1, MOE_TOK * TOP_K), lambda i, c, p: (i, 0, 0), memory_space=pltpu.SMEM),
                pl.BlockSpec(memory_space=pl.ANY),
            ],
            out_specs=pl.BlockSpec(memory_space=pl.ANY),
            scratch_shapes=[pltpu.VMEM((3, MOE_TOK, h.shape[1]), h.dtype), pltpu.SemaphoreType.DMA((3,)),
                            pltpu.SemaphoreType.DMA((2,)), pltpu.SemaphoreType.DMA(())],
        ),
        out_shape=jax.ShapeDtypeStruct((n_rows, h.shape[1]), h.dtype),
        compiler_params=_cparams(("arbitrary",)),
        name="moe_dispatch",
    )(counts, pad_start, dest.reshape(nb, 1, MOE_TOK * TOP_K), h)


def _expert_switch(be_ref, nu_ref):
    m = pl.program_id(1)
    live = m < nu_ref[0]
    fresh = jnp.logical_or(m == 0, be_ref[m] != be_ref[jnp.maximum(m - 1, 0)])
    return live, jnp.logical_and(live, fresh)


def _gmm1_kernel(be_ref, nu_ref, x_ref, w_ref, p_ref, bg_ref, bl_ref, o_ref, wg_scr, wl_scr):
    live, fresh = _expert_switch(be_ref, nu_ref)
    half = MXU_N // 2

    @pl.when(fresh)
    def _():
        for c in range(w_ref.shape[3] // MXU_N):
            blk = w_ref[0, 0, :, c * MXU_N:(c + 1) * MXU_N].astype(BF16)
            sp = jnp.dot(blk, p_ref[...], preferred_element_type=F32)
            wg_scr[:, c * half:(c + 1) * half] = sp[:, :half].astype(BF16)
            wl_scr[:, c * half:(c + 1) * half] = sp[:, half:].astype(BF16)

    @pl.when(live)
    def _():
        x_lo, x_hi = _unpack_bf16_pairs(x_ref[...])
        kh = x_lo.shape[1]
        g = (jnp.dot(x_lo, wg_scr[:kh, :], preferred_element_type=F32)
             + jnp.dot(x_hi, wg_scr[kh:, :], preferred_element_type=F32) + bg_ref[0])
        u = (jnp.dot(x_lo, wl_scr[:kh, :], preferred_element_type=F32)
             + jnp.dot(x_hi, wl_scr[kh:, :], preferred_element_type=F32) + bl_ref[0])
        g = jnp.minimum(g, SWIGLU_LIMIT)
        u = jnp.clip(u, -SWIGLU_LIMIT, SWIGLU_LIMIT)
        o_ref[...] = (g * jax.nn.sigmoid(SWIGLU_ALPHA * g) * (u + 1.0)).astype(o_ref.dtype)

    @pl.when(jnp.logical_not(live))
    def _():
        o_ref[...] = jnp.zeros(o_ref.shape, o_ref.dtype)


def _gmm2_kernel(be_ref, nu_ref, x_ref, w_ref, b_ref, o_ref, w_scr):
    live, fresh = _expert_switch(be_ref, nu_ref)

    @pl.when(fresh)
    def _():
        w_scr[...] = w_ref[0, 0].astype(BF16)

    @pl.when(live)
    def _():
        o_ref[...] = _pack_bf16_pairs(jnp.dot(x_ref[...], w_scr[...], preferred_element_type=F32) + b_ref[0])

    @pl.when(jnp.logical_not(live))
    def _():
        o_ref[...] = jnp.zeros(o_ref.shape, o_ref.dtype)


def _experts(xs, block_e, n_used, layer, w_gu, bg, bl, w_dn, bd, tn=1024, tn2=MOE_DOWN_TN):
    n_rows = xs.shape[0]
    nb = n_rows // MOE_TM
    idx = jnp.arange(MXU_N)
    perm = (idx[:, None] == jnp.where(idx < MXU_N // 2, 2 * idx, 2 * (idx - MXU_N // 2) + 1)[None, :]).astype(BF16)

    def mrow(n, m, be, nu):
        return (jnp.minimum(m, nu[0] - 1), 0)

    def bmap(n, m, be, nu):
        return (be[jnp.minimum(m, nu[0] - 1)], 0, n)

    def wmap(n, m, be, nu):
        return (layer, be[jnp.minimum(m, nu[0] - 1)], 0, n)

    hid = pl.pallas_call(
        _gmm1_kernel,
        grid_spec=pltpu.PrefetchScalarGridSpec(
            num_scalar_prefetch=2,
            grid=(D_EXPERT // tn, nb),
            in_specs=[
                pl.BlockSpec((MOE_TM, D_MODEL // 2), mrow),
                pl.BlockSpec((1, 1, D_MODEL, 2 * tn), wmap),
                pl.BlockSpec((MXU_N, MXU_N), lambda n, m, be, nu: (0, 0)),
                pl.BlockSpec((1, 1, tn), bmap),
                pl.BlockSpec((1, 1, tn), bmap),
            ],
            out_specs=pl.BlockSpec((MOE_TM, tn), lambda n, m, be, nu: (m, n)),
            scratch_shapes=[pltpu.VMEM((D_MODEL, tn), BF16), pltpu.VMEM((D_MODEL, tn), BF16)],
        ),
        out_shape=jax.ShapeDtypeStruct((n_rows, D_EXPERT), BF16),
        compiler_params=_cparams(("parallel", "arbitrary"), VMEM_LIMIT_EXPERTS),
        name="moe_gate_up",
    )(block_e, n_used, xs, w_gu, perm, bg, bl)
    return pl.pallas_call(
        _gmm2_kernel,
        grid_spec=pltpu.PrefetchScalarGridSpec(
            num_scalar_prefetch=2,
            grid=(D_MODEL // tn2, nb),
            in_specs=[
                pl.BlockSpec((MOE_TM, D_EXPERT), mrow),
                pl.BlockSpec((1, 1, D_EXPERT, tn2), wmap),
                pl.BlockSpec((1, 1, tn2), bmap),
            ],
            out_specs=pl.BlockSpec((MOE_TM, tn2 // 2), lambda n, m, be, nu: (m, n)),
            scratch_shapes=[pltpu.VMEM((D_EXPERT, tn2), BF16)],
        ),
        out_shape=jax.ShapeDtypeStruct((n_rows, D_MODEL // 2), U32),
        compiler_params=_cparams(("parallel", "arbitrary"), VMEM_LIMIT_EXPERTS),
        name="moe_down",
    )(block_e, n_used, hid, w_dn, bd)


def _combine_kernel(dest_ref, next_ref, gate_ref, y_hbm, x_ref, m_ref, g_ref, b_ref, o_ref, buf, sems):
    i = pl.program_id(0)
    slot = i % 2

    def gather(idx_ref, s):
        def issue(t, c):
            for k in range(TOP_K):
                pltpu.make_async_copy(y_hbm.at[pl.ds(idx_ref[0, 0, t * TOP_K + k], 1), :],
                                      buf.at[s, k, pl.ds(t, 1), :], sems.at[s]).start()
            return c

        lax.fori_loop(0, MOE_TOK, issue, 0)

    @pl.when(i == 0)
    def _():
        gather(dest_ref, 0)

    @pl.when(i + 1 < pl.num_programs(0))
    def _():
        gather(next_ref, 1 - slot)

    for k in range(TOP_K):
        pltpu.make_async_copy(y_hbm.at[pl.ds(0, MOE_TOK), :], buf.at[slot, k], sems.at[slot]).wait()
    gate = gate_ref[...]
    y = None
    for k in range(TOP_K):
        lo, hi = _unpack_bf16_pairs(buf[slot, k])
        hw = MOE_DOWN_TN // 2
        parts = []
        for c in range(lo.shape[1] // hw):
            parts += [lo[:, c * hw:(c + 1) * hw], hi[:, c * hw:(c + 1) * hw]]
        yk = gate[:, k:k + 1] * jnp.concatenate(parts, axis=1).astype(F32)
        y = yk if y is None else y + yk
    m = m_ref[0, 0]
    z = DEEPNORM_ALPHA * x_ref[...] + m[5:6] * y
    o_ref[...] = _ln(z) * g_ref[...] + b_ref[...]


def _combine_pn(y_rows, dest, gate, x, modt, ln_g, ln_b, mod_map):
    t = x.shape[0]
    nb = t // MOE_TOK
    dest3 = dest.reshape(nb, 1, MOE_TOK * TOP_K)
    row = pl.BlockSpec((MOE_TOK, D_MODEL), lambda i: (i, 0))
    vec = pl.BlockSpec((1, D_MODEL), lambda i: (0, 0))
    return pl.pallas_call(
        _combine_kernel,
        grid=(nb,),
        in_specs=[
            pl.BlockSpec((1, 1, MOE_TOK * TOP_K), lambda i: (i, 0, 0), memory_space=pltpu.SMEM),
            pl.BlockSpec((1, 1, MOE_TOK * TOP_K), lambda i: (jnp.minimum(i + 1, nb - 1), 0, 0),
                         memory_space=pltpu.SMEM),
            pl.BlockSpec((MOE_TOK, LANE), lambda i: (i, 0)),
            pl.BlockSpec(memory_space=pl.ANY),
            row,
            pl.BlockSpec((1, 1, 8, D_MODEL), lambda i: mod_map(i) + (0, 0)),
            vec, vec,
        ],
        out_specs=row,
        out_shape=jax.ShapeDtypeStruct((t, D_MODEL), F32),
        scratch_shapes=[pltpu.VMEM((2, TOP_K, MOE_TOK, D_MODEL // 2), U32), pltpu.SemaphoreType.DMA((2,))],
        compiler_params=_cparams(("arbitrary",)),
        name="moe_combine",
    )(dest3, dest3, gate, y_rows, x, modt, ln_g.reshape(1, D_MODEL), ln_b.reshape(1, D_MODEL))


def _rank_kernel(te_ref, tri_ref, rank_ref, cnt_ref, run_ref):
    @pl.when(pl.program_id(0) == 0)
    def _():
        run_ref[...] = jnp.zeros(run_ref.shape, F32)

    te = te_ref[...]
    lane = lax.broadcasted_iota(I32, te.shape, 1)
    base = run_ref[...]
    rank = jnp.zeros(te.shape, F32)
    for k in range(TOP_K):
        hit = te[:, k:k + 1] == lane
        onehot = jnp.where(hit, 1.0, 0.0).astype(BF16)
        before = jnp.dot(tri_ref[...], onehot, preferred_element_type=F32)
        rk = jnp.sum(jnp.where(hit, before + base, 0.0), axis=-1, keepdims=True)
        rank = jnp.where(lane == k, rk, rank)
        base = base + jnp.sum(onehot.astype(F32), axis=0, keepdims=True)
    run_ref[...] = base
    rank_ref[...] = rank.astype(I32)
    cnt_ref[...] = base.astype(I32)


def _ranks(top_e):
    t = top_e.shape[0]
    tri = (jnp.arange(MOE_TOK)[:, None] > jnp.arange(MOE_TOK)[None, :]).astype(BF16)
    return pl.pallas_call(
        _rank_kernel,
        grid=(t // MOE_TOK,),
        in_specs=[pl.BlockSpec((MOE_TOK, LANE), lambda i: (i, 0)), pl.BlockSpec((MOE_TOK, MOE_TOK), lambda i: (0, 0))],
        out_specs=[pl.BlockSpec((MOE_TOK, LANE), lambda i: (i, 0)), pl.BlockSpec((1, LANE), lambda i: (0, 0))],
        out_shape=[jax.ShapeDtypeStruct((t, LANE), I32), jax.ShapeDtypeStruct((1, LANE), I32)],
        scratch_shapes=[pltpu.VMEM((1, LANE), F32)],
        compiler_params=_cparams(("arbitrary",)),
        name="moe_rank",
    )(top_e, tri)


def _moe(h, top_e, gate, x, modt, mod_map, ln_g, ln_b, layer, w_gu, b_gu, w_dn, b_dn):
    t = h.shape[0]
    n_assign = t * TOP_K
    nb = n_assign // MOE_TM + N_EXPERTS
    rank, cnt = _ranks(top_e)
    counts = cnt[0, :N_EXPERTS]
    padded = (counts + MOE_TM - 1) // MOE_TM * MOE_TM
    pad_end = jnp.cumsum(padded)
    pad_start = jnp.concatenate([jnp.zeros((1,), I32), pad_end]).astype(I32)
    te4 = top_e[:, :TOP_K]
    hit = te4[:, :, None] == jnp.arange(N_EXPERTS, dtype=I32)[None, None, :]
    dest = (jnp.sum(jnp.where(hit, pad_start[None, None, :N_EXPERTS], 0), axis=-1) + rank[:, :TOP_K]).astype(I32)
    starts = jnp.arange(nb, dtype=I32) * MOE_TM
    block_e = jnp.minimum(jnp.sum((pad_end[None, :] <= starts[:, None]).astype(I32), axis=1), N_EXPERTS - 1)
    n_used = (pad_end[-1:] // MOE_TM).astype(I32)

    xs = _dispatch(h, dest, counts.astype(I32), pad_start, nb * MOE_TM)
    bg = b_gu[layer, :, None, 0::2]
    bl = b_gu[layer, :, None, 1::2]
    y_rows = _experts(xs, block_e, n_used, layer, w_gu, bg, bl, w_dn, b_dn[layer, :, None, :])
    return _combine_pn(y_rows, dest, gate, x, modt, ln_g, ln_b, mod_map)


def _rope_tables(n_ctx, seq):
    pos = jnp.arange(seq)
    row = (pos // GRID_W).astype(F32)
    col = (pos % GRID_W).astype(F32)
    n_freq = MLA_ROPE // 4
    freqs = ROPE_BASE ** (-jnp.arange(n_freq, dtype=F32) / n_freq)
    ar = row[:, None] * freqs
    ac = col[:, None] * freqs
    cos = jnp.concatenate([jnp.cos(ar), jnp.cos(ar), jnp.cos(ac), jnp.cos(ac)], axis=-1)
    sin = jnp.concatenate([jnp.sin(ar), jnp.sin(ar), jnp.sin(ac), jnp.sin(ac)], axis=-1)
    cos = jnp.concatenate([jnp.ones((n_ctx, MLA_ROPE), F32), cos], axis=0)
    sin = jnp.concatenate([jnp.zeros((n_ctx, MLA_ROPE), F32), sin], axis=0)
    return cos, sin


def _rot_cols(w):
    q = MLA_ROPE // 4
    a, b, c, d = w[..., :q], w[..., q:2 * q], w[..., 2 * q:3 * q], w[..., 3 * q:]
    return jnp.concatenate([-b, a, -d, c], axis=-1)


def _rmsnorm_epilogue(accs, ex):
    x = accs[0]
    return x * lax.rsqrt(jnp.mean(x * x, axis=-1, keepdims=True) + RMS_EPS) * ex[0]


def _hgrn_layer(xs, modt, mod_map, w_in, gnorm, w_o, lbs, bsz, l, n_ctx):
    h = _ln_mod(xs, modt, mod_map, 0)
    wb = w_in.astype(BF16)
    silu_ep = lambda accs, ex: _silu(accs[0])
    q = _mm(h, [wb], HG_W, 1024, BF16, silu_ep, w_col0=0, name="hg_q")
    fr = _mm(h, [wb], 2 * HG_W, 1024, F32, w_col0=HG_W, name="hg_f")
    v = _mm(h, [wb], HG_W, 1024, BF16, w_col0=3 * HG_W, name="hg_v")
    g = _mm(h, [wb], HG_W, 1024, BF16, silu_ep, w_col0=4 * HG_W, name="hg_g")
    r3 = lambda a: a.reshape(bsz, l, a.shape[-1])
    o_f = _hgrn_scan(r3(q), r3(fr), r3(v), lbs[0], False, n_ctx)
    o = _hgrn_scan(r3(q), r3(fr), r3(v), lbs[1], True, n_ctx, o_fwd=o_f, g=r3(g), gnorm=gnorm)
    return _mm(o.reshape(bsz * l, HG_W), [w_o.astype(BF16)], D_MODEL, 1024, BF16, name="hg_o")


def _mla_layer(xs, modt, mod_map, w_in, q_norm, kv_norm, w_uq, w_ukv, w_o, bsz, l, n_ctx):
    h = _ln_mod(xs, modt, mod_map, 0)
    wb = w_in.astype(BF16)
    nl = l // ROW_TILE
    vec = lambda n: ((1, n), lambda j, i: (0, 0))
    cq = _mm(h, [wb], MLA_Q_RANK, 512, BF16, _rmsnorm_epilogue,
             extras=[(q_norm.reshape(1, -1),) + vec(MLA_Q_RANK)], tm=ROW_TILE, w_col0=0, name="mla_cq")
    ckv = _mm(h, [wb], MLA_KV_RANK, 512, BF16, _rmsnorm_epilogue,
              extras=[(kv_norm.reshape(1, -1),) + vec(MLA_KV_RANK)], tm=ROW_TILE, w_col0=MLA_Q_RANK, name="mla_ckv")
    cos, sin = _rope_tables(n_ctx, l - n_ctx)
    rope_ep = lambda accs, ex: accs[0] * ex[0] + accs[1] * ex[1]
    tab = lambda n: ((ROW_TILE, n), lambda j, i: (i % nl, 0))
    ns, nc = (l - n_ctx) // ROW_TILE, n_ctx // ROW_TILE
    qtab = lambda n: ((ROW_TILE, n), lambda j, i: (nc + i % ns, 0))
    lat_rows = (bsz * (l - n_ctx), lambda i: (i // ns) * nl + nc + i % ns)
    w_kp = w_in[:, MLA_Q_RANK + MLA_KV_RANK:]
    zk = jnp.zeros((D_MODEL, LANE - MLA_ROPE), F32)
    zt = jnp.zeros((l, LANE - MLA_ROPE), F32)
    kp = _mm(h, [jnp.concatenate([w_kp, zk], 1).astype(BF16), jnp.concatenate([_rot_cols(w_kp), zk], 1).astype(BF16)],
             LANE, LANE, BF16, rope_ep,
             extras=[(jnp.concatenate([cos, zt], 1),) + tab(LANE), (jnp.concatenate([sin, zt], 1),) + tab(LANE)],
             tm=ROW_TILE, name="mla_kp")
    wq = w_uq.reshape(MLA_Q_RANK, MLA_HEADS, MLA_NOPE + MLA_ROPE)
    zq = jnp.zeros((MLA_Q_RANK, MLA_HEADS, MLA_QK_PAD - MLA_NOPE - MLA_ROPE), F32)
    wqa = jnp.concatenate([wq, zq], -1).reshape(MLA_Q_RANK, -1).astype(BF16)
    wqb = jnp.concatenate([jnp.zeros_like(wq[..., :MLA_NOPE]), _rot_cols(wq[..., MLA_NOPE:]), zq], -1)
    wqb = wqb.reshape(MLA_Q_RANK, -1).astype(BF16)
    zt = jnp.zeros((l, MLA_QK_PAD - MLA_NOPE - MLA_ROPE), F32)
    q_scale = MLA_SCALE * LOG2_E
    cq_tab = jnp.concatenate([jnp.ones((l, MLA_NOPE), F32), cos, zt], 1) * q_scale
    sq_tab = jnp.concatenate([jnp.zeros((l, MLA_NOPE), F32), sin, zt], 1) * q_scale
    hq = 8
    qf = _mm(cq, [wqa, wqb], MLA_HEADS * MLA_QK_PAD, hq * MLA_QK_PAD, BF16, rope_ep,
             extras=[(jnp.tile(cq_tab, (1, hq)),) + qtab(hq * MLA_QK_PAD),
                     (jnp.tile(sq_tab, (1, hq)),) + qtab(hq * MLA_QK_PAD)], tm=ROW_TILE, rows=lat_rows,
             name="mla_q")
    kv = _mm(ckv, [w_ukv.astype(BF16)], MLA_HEADS * (MLA_NOPE + MLA_V), 1024, BF16, name="mla_kv")
    r3 = lambda a: a.reshape(bsz, l, a.shape[-1])
    o = _attention(qf.reshape(bsz, l - n_ctx, -1), r3(kv), r3(kp))
    return _mm(o.reshape(bsz * (l - n_ctx), MLA_HEADS * MLA_V), [w_o.astype(BF16)], D_MODEL, 1024, BF16, name="mla_o")


def kernel(x, c, ctx, c_ctx, ada_w, ada_b, ln_g, ln_b, hg_w_in, hg_gnorm, hg_lb_logits, hg_w_o, mla_w_in,
           mla_q_norm, mla_kv_norm, mla_w_uq, mla_w_ukv, mla_w_o, router_w, router_b, exp_w_gu, exp_b_gu,
           exp_w_dn, exp_b_dn):
    bsz, seq, d = x.shape
    n_ctx = ctx.shape[1]
    l = n_ctx + seq
    nl = l // ROW_TILE
    ns = seq // ROW_TILE
    nc = n_ctx // ROW_TILE

    cc = jnp.zeros((8, d), F32).at[:bsz].set(c).at[bsz].set(c_ctx)
    mods = _ada(cc, ada_w, ada_b).reshape(DEPTH, 8, N_MOD, d)
    pad = jnp.zeros((DEPTH, bsz, 8 - N_MOD, d), F32)
    m_lat = jnp.concatenate([mods[:, :bsz], pad], axis=2)
    m_ctx = jnp.concatenate([jnp.broadcast_to(mods[:, bsz:bsz + 1], (DEPTH, bsz, N_MOD, d)), pad], axis=2)
    modt = jnp.stack([m_ctx, m_lat], axis=2)
    lower = jnp.cumsum(jax.nn.softmax(hg_lb_logits.astype(F32), axis=0), axis=0)

    xs = jnp.concatenate([ctx, x], axis=1).reshape(bsz * l, d)

    def map_all(r):
        per = l // r
        return lambda i: (i // per, ((i % per) >= n_ctx // r).astype(I32))

    def map_lat(r):
        per = seq // r
        return lambda i: (i // per, 1)

    y = _hgrn_layer(xs, modt[0], map_all(ROW_TILE), hg_w_in[0], hg_gnorm[0], hg_w_o[0], lower[0], bsz, l, n_ctx)
    xs, h, te, tg = _pn_router(xs, y, modt[0], ln_g[0, 0], ln_b[0, 0], router_w[0], router_b[0], bsz * l,
                               lambda i: i, map_all(ROW_TILE))
    xs = _moe(h, te, tg, xs, modt[0], map_all(MOE_TOK), ln_g[0, 1], ln_b[0, 1],
              0, exp_w_gu, exp_b_gu, exp_w_dn, exp_b_dn)

    y = _mla_layer(xs, modt[1], map_all(ROW_TILE), mla_w_in[0], mla_q_norm[0], mla_kv_norm[0], mla_w_uq[0],
                   mla_w_ukv[0], mla_w_o[0], bsz, l, n_ctx)
    xl, h, te, tg = _pn_router(xs, y, modt[1], ln_g[1, 0], ln_b[1, 0], router_w[1], router_b[1], bsz * seq,
                               lambda i: (i // ns) * nl + nc + i % ns, map_lat(ROW_TILE))
    out = _moe(h, te, tg, xl, modt[1], map_lat(MOE_TOK), ln_g[1, 1], ln_b[1, 1],
               1, exp_w_gu, exp_b_gu, exp_w_dn, exp_b_dn)
    return out.reshape(bsz, seq, d)
```

```python
import functools

import jax
import jax.numpy as jnp
from jax import lax
from jax.experimental import pallas as pl
from jax.experimental.pallas import tpu as pltpu

F32 = jnp.float32
BF16 = jnp.bfloat16
I32 = jnp.int32
U32 = jnp.uint32

D_MODEL = 2048
DEPTH = 2
GRID_W = 64
N_MOD = 6

HG_HEADS = 16
HG_DK = 128
HG_W = HG_HEADS * HG_DK
HG_SUB = 16

MLA_HEADS = 16
MLA_Q_RANK = 512
MLA_KV_RANK = 512
MLA_NOPE = 128
MLA_ROPE = 64
MLA_V = 128
MLA_QK_PAD = 256
MLA_SCALE = (MLA_NOPE + MLA_ROPE) ** -0.5
ROPE_BASE = 10000.0
LOG2_E = 1.4426950408889634

N_EXPERTS = 32
TOP_K = 4
D_EXPERT = 2048
SWIGLU_LIMIT = 7.0
SWIGLU_ALPHA = 1.702

DEEPNORM_ALPHA = (2 * DEPTH) ** 0.25
LN_EPS = 1e-5
RMS_EPS = 1e-6

LANE = 128
MXU_N = 256
ROW_TILE = 256
MOE_TM = 512
MOE_TOK = 128
MOE_DOWN_TN = 2048
ATTN_CHUNKS = 4
VMEM_LIMIT = 48 * 1024 * 1024
VMEM_LIMIT_EXPERTS = 60 * 1024 * 1024

_HIGHEST = lax.Precision.HIGHEST


def _cparams(sem, vmem=VMEM_LIMIT):
    return pltpu.CompilerParams(dimension_semantics=sem, vmem_limit_bytes=vmem)


def _ln(x):
    mu = jnp.mean(x, axis=-1, keepdims=True)
    xc = x - mu
    var = jnp.mean(xc * xc, axis=-1, keepdims=True)
    return xc * lax.rsqrt(var + LN_EPS)


def _silu(x):
    return x * jax.nn.sigmoid(x)


def _pack_bf16_pairs(x):
    n = x.shape[1] // 2
    bits = lax.bitcast_convert_type(x.astype(BF16).astype(F32), U32)
    return (bits[:, :n] >> 16) | (bits[:, n:] & jnp.uint32(0xFFFF0000))


def _unpack_bf16_pairs(w):
    lo = lax.bitcast_convert_type(w << 16, F32).astype(BF16)
    hi = lax.bitcast_convert_type(w & jnp.uint32(0xFFFF0000), F32).astype(BF16)
    return lo, hi


def _ada_kernel(c_ref, w_ref, b_ref, o_ref):
    s = _silu(c_ref[...])
    o_ref[0] = jnp.dot(s, w_ref[0], precision=_HIGHEST, preferred_element_type=F32) + b_ref[0]


def _ada(cc, ada_w, ada_b):
    n = N_MOD * D_MODEL
    tn = 1024
    return pl.pallas_call(
        _ada_kernel,
        grid=(DEPTH, n // tn),
        in_specs=[
            pl.BlockSpec((8, D_MODEL), lambda i, j: (0, 0)),
            pl.BlockSpec((1, D_MODEL, tn), lambda i, j: (i, 0, j)),
            pl.BlockSpec((1, 1, tn), lambda i, j: (i, 0, j)),
        ],
        out_specs=pl.BlockSpec((1, 8, tn), lambda i, j: (i, 0, j)),
        out_shape=jax.ShapeDtypeStruct((DEPTH, 8, n), F32),
        compiler_params=_cparams(("parallel", "parallel")),
        name="ada",
    )(cc, ada_w, ada_b.reshape(DEPTH, 1, n))


def _lnmod_kernel(x_ref, m_ref, o_ref, *, si):
    m = m_ref[0, 0]
    o_ref[...] = (_ln(x_ref[...]) * (1.0 + m[si + 1:si + 2]) + m[si:si + 1]).astype(o_ref.dtype)


def _ln_mod(x, modt, mod_map, si):
    t = x.shape[0]
    return pl.pallas_call(
        functools.partial(_lnmod_kernel, si=si),
        grid=(t // ROW_TILE,),
        in_specs=[
            pl.BlockSpec((ROW_TILE, D_MODEL), lambda i: (i, 0)),
            pl.BlockSpec((1, 1, 8, D_MODEL), lambda i: mod_map(i) + (0, 0)),
        ],
        out_specs=pl.BlockSpec((ROW_TILE, D_MODEL), lambda i: (i, 0)),
        out_shape=jax.ShapeDtypeStruct((t, D_MODEL), BF16),
        compiler_params=_cparams(("parallel",)),
        name="ln_mod",
    )(x, modt)


def _pn_router_kernel(x_ref, y_ref, m_ref, g_ref, b_ref, rw_ref, rb_ref, xn_ref, h_ref, te_ref, tg_ref):
    m = m_ref[0, 0]
    z = DEEPNORM_ALPHA * x_ref[...] + m[2:3] * y_ref[...].astype(F32)
    xn = _ln(z) * g_ref[...] + b_ref[...]
    xn_ref[...] = xn
    h = _ln(xn) * (1.0 + m[4:5]) + m[3:4]
    h_ref[...] = _pack_bf16_pairs(h)
    logits = jnp.dot(h, rw_ref[...], precision=_HIGHEST, preferred_element_type=F32) + rb_ref[...]
    lane = lax.broadcasted_iota(I32, logits.shape, 1).astype(F32)
    neg = jnp.float32(-jnp.inf)
    cur = jnp.where(lane < N_EXPERTS, logits, neg)
    vals, idxs = [], []
    for _ in range(TOP_K):
        mx = jnp.max(cur, axis=-1, keepdims=True)
        ix = jnp.min(jnp.where(cur == mx, lane, float(LANE)), axis=-1, keepdims=True)
        vals.append(mx)
        idxs.append(ix)
        cur = jnp.where(lane == ix, neg, cur)
    es = [jnp.exp(v - vals[0]) for v in vals]
    den = es[0] + es[1] + es[2] + es[3]
    te = jnp.zeros(logits.shape, F32)
    tg = jnp.zeros(logits.shape, F32)
    for k in range(TOP_K):
        te = jnp.where(lane == k, idxs[k], te)
        tg = jnp.where(lane == k, es[k] / den, tg)
    te_ref[...] = te.astype(I32)
    tg_ref[...] = tg


def _pn_router(x, y, modt, ln_g, ln_b, rw, rb, n_tok, x_map, mod_map):
    rwp = jnp.zeros((D_MODEL, LANE), F32).at[:, :N_EXPERTS].set(rw)
    rbp = jnp.zeros((1, LANE), F32).at[0, :N_EXPERTS].set(rb)
    row = pl.BlockSpec((ROW_TILE, D_MODEL), lambda i: (i, 0))
    vec = pl.BlockSpec((1, D_MODEL), lambda i: (0, 0))
    nar = pl.BlockSpec((ROW_TILE, LANE), lambda i: (i, 0))
    return pl.pallas_call(
        _pn_router_kernel,
        grid=(n_tok // ROW_TILE,),
        in_specs=[
            pl.BlockSpec((ROW_TILE, D_MODEL), lambda i: (x_map(i), 0)),
            row,
            pl.BlockSpec((1, 1, 8, D_MODEL), lambda i: mod_map(i) + (0, 0)),
            vec, vec,
            pl.BlockSpec((D_MODEL, LANE), lambda i: (0, 0)),
            pl.BlockSpec((1, LANE), lambda i: (0, 0)),
        ],
        out_specs=[row, pl.BlockSpec((ROW_TILE, D_MODEL // 2), lambda i: (i, 0)), nar, nar],
        out_shape=[
            jax.ShapeDtypeStruct((n_tok, D_MODEL), F32),
            jax.ShapeDtypeStruct((n_tok, D_MODEL // 2), U32),
            jax.ShapeDtypeStruct((n_tok, LANE), I32),
            jax.ShapeDtypeStruct((n_tok, LANE), F32),
        ],
        compiler_params=_cparams(("parallel",)),
        name="pn_router",
    )(x, y, modt, ln_g.reshape(1, D_MODEL), ln_b.reshape(1, D_MODEL), rwp, rbp)


def _mm_kernel(*refs, n_w, n_x, epilogue):
    a = refs[0][...]
    accs = [jnp.dot(a, w[...], preferred_element_type=F32) for w in refs[1:1 + n_w]]
    extras = [e[...] for e in refs[1 + n_w:1 + n_w + n_x]]
    o_ref = refs[-1]
    o_ref[...] = epilogue(accs, extras).astype(o_ref.dtype)


def _mm(a, ws, n_out, tn, out_dtype, epilogue=None, extras=(), tm=512, w_col0=0, rows=None, name="mm"):
    m, k = a.shape
    tm = min(tm, m)
    a_map = lambda i: i
    if rows is not None:
        m, a_map = rows
    assert m % tm == 0 and n_out % tn == 0 and w_col0 % tn == 0
    if epilogue is None:
        epilogue = lambda accs, ex: accs[0]
    c0 = w_col0 // tn
    in_specs = [pl.BlockSpec((tm, k), lambda j, i: (a_map(i), 0))]
    in_specs += [pl.BlockSpec((k, tn), lambda j, i: (0, j + c0)) for _ in ws]
    in_specs += [pl.BlockSpec(bs, im) for _, bs, im in extras]
    return pl.pallas_call(
        functools.partial(_mm_kernel, n_w=len(ws), n_x=len(extras), epilogue=epilogue),
        grid=(n_out // tn, m // tm),
        in_specs=in_specs,
        out_specs=pl.BlockSpec((tm, tn), lambda j, i: (i, j)),
        out_shape=jax.ShapeDtypeStruct((m, n_out), out_dtype),
        compiler_params=_cparams(("parallel", "parallel")),
        name=name,
    )(a, *ws, *[e[0] for e in extras])


def _scan_kernel(*refs, rev, hb, tl, final):
    n_in = 8 if final else 5
    q_ref, f_ref, v_ref, lb_ref, ones_ref = refs[:5]
    o_ref = refs[n_in]
    st_ref, nat_ref, dil_ref, p_ref, upd_ref, sb_ref = refs[n_in + 1:]
    nblk = tl // HG_SUB
    q_nat, v_nat, f_nat, qs_nat, kd_nat, o_nat = (nat_ref.at[n] for n in range(6))
    dq, dk, dv, dl, dcf, dcr = (dil_ref.at[n] for n in range(6))
    db, dother = (dcr, dcf) if rev else (dcf, dcr)
    rows = lambda i: pl.ds(i, nblk, stride=HG_SUB)
    lanes = lambda h: slice(h * HG_DK, (h + 1) * HG_DK)
    pairs = [list(range(i, HG_SUB)) if rev else list(range(i + 1)) for i in range(HG_SUB)]
    bases = [sum(len(p) for p in pairs[:i]) * nblk for i in range(HG_SUB)]

    @pl.when(pl.program_id(2) == 0)
    def _():
        st_ref[...] = jnp.zeros(st_ref.shape, F32)

    vb = v_ref[0]
    tots = []
    for h in range(hb):
        lb = lb_ref[:, lanes(h)]
        q_nat[h] = q_ref[0, :, lanes(h)].astype(F32)
        v_nat[h] = vb[:, lanes(h)].astype(F32)
        f_nat[h] = f_ref[0, :, lanes(h)]
        for i in range(HG_SUB):
            forget = lb + (1.0 - lb) * jax.nn.sigmoid(f_nat[h, rows(i), :])
            dl[h, i] = jnp.log2(forget)
            dk[h, i] = 1.0 - forget
            dq[h, i] = q_nat[h, rows(i), :]
            dv[h, i] = v_nat[h, rows(i), :]
        run = dl[h, 0]
        dcf[h, 0] = run
        for i in range(1, HG_SUB):
            run = run + dl[h, i]
            dcf[h, i] = run
        tots.append(run)
        run = dl[h, HG_SUB - 1]
        dcr[h, HG_SUB - 1] = run
        for i in range(HG_SUB - 2, -1, -1):
            run = run + dl[h, i]
            dcr[h, i] = run
        for i in range(HG_SUB):
            nxt = i - 1 if rev else i + 1
            rest = dother[h, nxt] if 0 <= nxt < HG_SUB else jnp.zeros((nblk, HG_DK), F32)
            qs_nat[h, rows(i), :] = dq[h, i] * jnp.exp2(db[h, i])
            kd_nat[h, rows(i), :] = dk[h, i] * jnp.exp2(rest)
        for i in range(HG_SUB):
            qi = dq[h, i]
            bi = db[h, i]
            for n, j in enumerate(pairs[i]):
                p_ref[pl.ds(bases[i] + n * nblk, nblk), lanes(h)] = (
                    qi * dk[h, j] * jnp.exp2(bi - db[h, j])).astype(BF16)
    ones = ones_ref[...]
    hpg = MXU_N // HG_DK
    for i in range(HG_SUB):
        for g0 in range(0, hb, hpg):
            rs = jnp.dot(p_ref[pl.ds(bases[i], len(pairs[i]) * nblk), g0 * HG_DK:(g0 + hpg) * HG_DK], ones,
                         preferred_element_type=F32)
            for h in range(g0, g0 + hpg):
                oi = rs[0:nblk, lanes(h - g0)] * dv[h, pairs[i][0]]
                for n in range(1, len(pairs[i])):
                    oi = oi + rs[n * nblk:(n + 1) * nblk, lanes(h - g0)] * dv[h, pairs[i][n]]
                o_nat[h, rows(i), :] = oi

    blk = lambda j: pl.ds(j * HG_SUB, HG_SUB)
    for j in range(nblk):
        for h in range(hb):
            upd_ref[j, h] = lax.dot_general(vb[j * HG_SUB:(j + 1) * HG_SUB, lanes(h)],
                                            kd_nat[h, blk(j), :].astype(BF16),
                                            (((0,), (0,)), ((), ())), preferred_element_type=F32)
    for h in range(hb):
        st = st_ref[h]
        for j in (range(nblk - 1, -1, -1) if rev else range(nblk)):
            sb_ref[j, h] = st.astype(BF16)
            st = st * jnp.exp2(tots[h][j:j + 1, :]) + upd_ref[j, h]
        st_ref[h] = st
    for j in range(nblk):
        for h in range(hb):
            oi = lax.dot_general(qs_nat[h, blk(j), :].astype(BF16), sb_ref[j, h],
                                 (((1,), (1,)), ((), ())), preferred_element_type=F32)
            if final:
                o_nat[h, blk(j), :] = o_nat[h, blk(j), :] + oi
            else:
                o_ref[0, blk(j), lanes(h)] = o_nat[h, blk(j), :] + oi

    if final:
        of_ref, g_ref, gw_ref = refs[5:8]
        for h in range(hb):
            oh = o_nat[h] + of_ref[0, :, lanes(h)]
            ms = jnp.mean(oh * oh, axis=-1, keepdims=True)
            o_ref[0, :, lanes(h)] = (oh * lax.rsqrt(ms + RMS_EPS) * gw_ref[:, lanes(h)]
                                     * g_ref[0, :, lanes(h)].astype(F32)).astype(o_ref.dtype)


def _hgrn_scan(q, fr, v, lb, rev, n_ctx, o_fwd=None, g=None, gnorm=None, hb=4, tl=256):
    bsz, l, _ = q.shape
    w = hb * HG_DK
    nl = l // tl
    nc = n_ctx // tl
    final = rev

    def lmap(i):
        if not rev:
            return i
        return jnp.where(i < nc, nc - 1 - i, nl - 1 - (i - nc))

    foff = (HG_W // w) if rev else 0
    tile = lambda b, h, i: (b, lmap(i), h)
    ones = (jnp.arange(MXU_N)[:, None] // HG_DK == jnp.arange(MXU_N)[None, :] // HG_DK).astype(BF16)
    in_specs = [
        pl.BlockSpec((1, tl, w), tile),
        pl.BlockSpec((1, tl, w), lambda b, h, i: (b, lmap(i), h + foff)),
        pl.BlockSpec((1, tl, w), tile),
        pl.BlockSpec((1, w), lambda b, h, i: (0, h)),
        pl.BlockSpec((MXU_N, MXU_N), lambda b, h, i: (0, 0)),
    ]
    args = [q, fr, v, lb.reshape(1, HG_W), ones]
    nblk = tl // HG_SUB
    n_pairs = HG_SUB * (HG_SUB + 1) // 2
    scratch = [pltpu.VMEM((hb, HG_DK, HG_DK), F32), pltpu.VMEM((6, hb, tl, HG_DK), F32),
               pltpu.VMEM((6, hb, HG_SUB, nblk, HG_DK), F32), pltpu.VMEM((n_pairs * nblk, w), BF16),
               pltpu.VMEM((nblk, hb, HG_DK, HG_DK), F32), pltpu.VMEM((nblk, hb, HG_DK, HG_DK), BF16)]
    if final:
        in_specs += [pl.BlockSpec((1, tl, w), tile), pl.BlockSpec((1, tl, w), tile),
                     pl.BlockSpec((1, w), lambda b, h, i: (0, h))]
        args += [o_fwd, g, gnorm.reshape(1, HG_W)]
    return pl.pallas_call(
        functools.partial(_scan_kernel, rev=rev, hb=hb, tl=tl, final=final),
        grid=(bsz, HG_W // w, nl),
        in_specs=in_specs,
        out_specs=pl.BlockSpec((1, tl, w), tile),
        out_shape=jax.ShapeDtypeStruct((bsz, l, HG_W), BF16 if final else F32),
        scratch_shapes=scratch,
        compiler_params=_cparams(("parallel", "parallel", "arbitrary")),
        name="hgrn_scan_bwd" if rev else "hgrn_scan_fwd",
    )(*args)


def _attn_kernel(q_ref, kn_ref, kp_ref, v_ref, o_ref, k_scr, v_scr):
    @pl.when(pl.program_id(2) == 0)
    def _():
        k_scr[:, 0:MLA_NOPE] = kn_ref[0]
        k_scr[:, MLA_NOPE:MLA_QK_PAD] = kp_ref[0]
        v_scr[:, 0:MLA_V] = v_ref[0]
        v_scr[:, MLA_V:] = jnp.ones((v_scr.shape[0], v_scr.shape[1] - MLA_V), v_scr.dtype)

    q = q_ref[0]
    l = k_scr.shape[0]
    step = -(-l // (ATTN_CHUNKS * MXU_N)) * MXU_N
    m = acc = None
    for c0 in range(0, l, step):
        rows = pl.ds(c0, min(step, l - c0))
        s = lax.dot_general(q, k_scr[rows, :], (((1,), (1,)), ((), ())), preferred_element_type=F32)
        mc = jnp.max(s, axis=-1, keepdims=True)
        m_new = mc if m is None else jnp.maximum(m, mc)
        pv = jnp.dot(jnp.exp2(s - m_new).astype(BF16), v_scr[rows, :], preferred_element_type=F32)
        acc = pv if m is None else acc * jnp.exp2(m - m_new) + pv
        m = m_new
    o_ref[0] = (acc[:, :MLA_V] / acc[:, MLA_V:MLA_V + 1]).astype(o_ref.dtype)


def _attention(qf, kv, kp, tq=1024):
    bsz, s, _ = qf.shape
    l = kv.shape[1]
    tq = min(tq, s)
    assert s % tq == 0
    return pl.pallas_call(
        _attn_kernel,
        grid=(bsz, MLA_HEADS, s // tq),
        in_specs=[
            pl.BlockSpec((1, tq, MLA_QK_PAD), lambda b, h, i: (b, i, h)),
            pl.BlockSpec((1, l, MLA_NOPE), lambda b, h, i: (b, 0, 2 * h)),
            pl.BlockSpec((1, l, LANE), lambda b, h, i: (b, 0, 0)),
            pl.BlockSpec((1, l, MLA_V), lambda b, h, i: (b, 0, 2 * h + 1)),
        ],
        out_specs=pl.BlockSpec((1, tq, MLA_V), lambda b, h, i: (b, i, h)),
        out_shape=jax.ShapeDtypeStruct((bsz, s, MLA_HEADS * MLA_V), BF16),
        scratch_shapes=[pltpu.VMEM((l, MLA_QK_PAD), BF16), pltpu.VMEM((l, MXU_N), BF16)],
        compiler_params=_cparams(("parallel", "parallel", "arbitrary")),
        name="mla_attn",
    )(qf, kv, kp, kv)


def _dispatch_kernel(cnt_ref, pst_ref, dest_ref, h_hbm, xs_hbm, hbuf, lsem, rsem, psem, *, nb):
    i = pl.program_id(0)
    slot = i % 3

    def load(j, s):
        return pltpu.make_async_copy(h_hbm.at[pl.ds(j * MOE_TOK, MOE_TOK), :], hbuf.at[s], lsem.at[s])

    def row_copy(s, t, dst, sem):
        return pltpu.make_async_copy(hbuf.at[s, pl.ds(t, 1), :], xs_hbm.at[pl.ds(dst, 1), :], sem)

    def wait_rows(sem):
        for _ in range(TOP_K):
            pltpu.make_async_copy(hbuf.at[0], xs_hbm.at[pl.ds(0, MOE_TOK), :], sem).wait()

    @pl.when(i == 0)
    def _():
        for j in range(min(2, nb)):
            load(j, j).start()

    load(i, slot).wait()

    def issue(t, c):
        for k in range(TOP_K):
            row_copy(slot, t, dest_ref[0, 0, t * TOP_K + k], rsem.at[i % 2]).start()
        return c

    lax.fori_loop(0, MOE_TOK, issue, 0)

    @pl.when(i == 0)
    def _():
        for e in range(N_EXPERTS):
            first = pst_ref[e] + cnt_ref[e]
            last = pst_ref[e + 1]
            lax.fori_loop(first, last, lambda r, c: (row_copy(0, 0, r, psem).start(), c)[1], 0)
            lax.fori_loop(first, last, lambda r, c: (row_copy(0, 0, r, psem).wait(), c)[1], 0)

        def block_copy(m):
            return pltpu.make_async_copy(hbuf.at[0], xs_hbm.at[pl.ds(m * MOE_TOK, MOE_TOK), :], psem)

        used = pst_ref[N_EXPERTS] // MOE_TOK
        total = xs_hbm.shape[0] // MOE_TOK
        lax.fori_loop(used, total, lambda m, c: (block_copy(m).start(), c)[1], 0)
        lax.fori_loop(used, total, lambda m, c: (block_copy(m).wait(), c)[1], 0)

    @pl.when(i > 0)
    def _():
        wait_rows(rsem.at[(i - 1) % 2])

    @pl.when(i + 2 < nb)
    def _():
        load(i + 2, (i + 2) % 3).start()

    @pl.when(i == nb - 1)
    def _():
        wait_rows(rsem.at[i % 2])


def _dispatch(h, dest, counts, pad_start, n_rows):
    t = h.shape[0]
    nb = t // MOE_TOK
    return pl.pallas_call(
        functools.partial(_dispatch_kernel, nb=nb),
        grid_spec=pltpu.PrefetchScalarGridSpec(
            num_scalar_prefetch=2,
            grid=(nb,),
            in_specs=[
                pl.BlockSpec((1, 1, MOE_TOK * TOP_K), lambda i, c, p: (i, 0, 0), memory_space=pltpu.SMEM),
                pl.BlockSpec(memory_space=pl.ANY),
            ],
            out_specs=pl.BlockSpec(memory_space=pl.ANY),
            scratch_shapes=[pltpu.VMEM((3, MOE_TOK, h.shape[1]), h.dtype), pltpu.SemaphoreType.DMA((3,)),
                            pltpu.SemaphoreType.DMA((2,)), pltpu.SemaphoreType.DMA(())],
        ),
        out_shape=jax.ShapeDtypeStruct((n_rows, h.shape[1]), h.dtype),
        compiler_params=_cparams(("arbitrary",)),
        name="moe_dispatch",
    )(counts, pad_start, dest.reshape(nb, 1, MOE_TOK * TOP_K), h)


def _expert_switch(be_ref, nu_ref):
    m = pl.program_id(1)
    live = m < nu_ref[0]
    fresh = jnp.logical_or(m == 0, be_ref[m] != be_ref[jnp.maximum(m - 1, 0)])
    return live, jnp.logical_and(live, fresh)


def _gmm1_kernel(be_ref, nu_ref, x_ref, w_ref, p_ref, bg_ref, bl_ref, o_ref, wg_scr, wl_scr):
    live, fresh = _expert_switch(be_ref, nu_ref)
    half = MXU_N // 2

    @pl.when(fresh)
    def _():
        for c in range(w_ref.shape[3] // MXU_N):
            blk = w_ref[0, 0, :, c * MXU_N:(c + 1) * MXU_N].astype(BF16)
            sp = jnp.dot(blk, p_ref[...], preferred_element_type=F32)
            wg_scr[:, c * half:(c + 1) * half] = sp[:, :half].astype(BF16)
            wl_scr[:, c * half:(c + 1) * half] = sp[:, half:].astype(BF16)

    @pl.when(live)
    def _():
        x_lo, x_hi = _unpack_bf16_pairs(x_ref[...])
        kh = x_lo.shape[1]
        g = (jnp.dot(x_lo, wg_scr[:kh, :], preferred_element_type=F32)
             + jnp.dot(x_hi, wg_scr[kh:, :], preferred_element_type=F32) + bg_ref[0])
        u = (jnp.dot(x_lo, wl_scr[:kh, :], preferred_element_type=F32)
             + jnp.dot(x_hi, wl_scr[kh:, :], preferred_element_type=F32) + bl_ref[0])
        g = jnp.minimum(g, SWIGLU_LIMIT)
        u = jnp.clip(u, -SWIGLU_LIMIT, SWIGLU_LIMIT)
        o_ref[...] = (g * jax.nn.sigmoid(SWIGLU_ALPHA * g) * (u + 1.0)).astype(o_ref.dtype)

    @pl.when(jnp.logical_not(live))
    def _():
        o_ref[...] = jnp.zeros(o_ref.shape, o_ref.dtype)


def _gmm2_kernel(be_ref, nu_ref, x_ref, w_ref, b_ref, o_ref, w_scr):
    live, fresh = _expert_switch(be_ref, nu_ref)

    @pl.when(fresh)
    def _():
        w_scr[...] = w_ref[0, 0].astype(BF16)

    @pl.when(live)
    def _():
        o_ref[...] = _pack_bf16_pairs(jnp.dot(x_ref[...], w_scr[...], preferred_element_type=F32) + b_ref[0])

    @pl.when(jnp.logical_not(live))
    def _():
        o_ref[...] = jnp.zeros(o_ref.shape, o_ref.dtype)


def _experts(xs, block_e, n_used, layer, w_gu, bg, bl, w_dn, bd, tn=1024, tn2=MOE_DOWN_TN):
    n_rows = xs.shape[0]
    nb = n_rows // MOE_TM
    idx = jnp.arange(MXU_N)
    perm = (idx[:, None] == jnp.where(idx < MXU_N // 2, 2 * idx, 2 * (idx - MXU_N // 2) + 1)[None, :]).astype(BF16)

    def mrow(n, m, be, nu):
        return (jnp.minimum(m, nu[0] - 1), 0)

    def bmap(n, m, be, nu):
        return (be[jnp.minimum(m, nu[0] - 1)], 0, n)

    def wmap(n, m, be, nu):
        return (layer, be[jnp.minimum(m, nu[0] - 1)], 0, n)

    hid = pl.pallas_call(
        _gmm1_kernel,
        grid_spec=pltpu.PrefetchScalarGridSpec(
            num_scalar_prefetch=2,
            grid=(D_EXPERT // tn, nb),
            in_specs=[
                pl.BlockSpec((MOE_TM, D_MODEL // 2), mrow),
                pl.BlockSpec((1, 1, D_MODEL, 2 * tn), wmap),
                pl.BlockSpec((MXU_N, MXU_N), lambda n, m, be, nu: (0, 0)),
                pl.BlockSpec((1, 1, tn), bmap),
                pl.BlockSpec((1, 1, tn), bmap),
            ],
            out_specs=pl.BlockSpec((MOE_TM, tn), lambda n, m, be, nu: (m, n)),
            scratch_shapes=[pltpu.VMEM((D_MODEL, tn), BF16), pltpu.VMEM((D_MODEL, tn), BF16)],
        ),
        out_shape=jax.ShapeDtypeStruct((n_rows, D_EXPERT), BF16),
        compiler_params=_cparams(("parallel", "arbitrary"), VMEM_LIMIT_EXPERTS),
        name="moe_gate_up",
    )(block_e, n_used, xs, w_gu, perm, bg, bl)
    return pl.pallas_call(
        _gmm2_kernel,
        grid_spec=pltpu.PrefetchScalarGridSpec(
            num_scalar_prefetch=2,
            grid=(D_MODEL // tn2, nb),
            in_specs=[
                pl.BlockSpec((MOE_TM, D_EXPERT), mrow),
                pl.BlockSpec((1, 1, D_EXPERT, tn2), wmap),
                pl.BlockSpec((1, 1, tn2), bmap),
            ],
            out_specs=pl.BlockSpec((MOE_TM, tn2 // 2), lambda n, m, be, nu: (m, n)),
            scratch_shapes=[pltpu.VMEM((D_EXPERT, tn2), BF16)],
        ),
        out_shape=jax.ShapeDtypeStruct((n_rows, D_MODEL // 2), U32),
        compiler_params=_cparams(("parallel", "arbitrary"), VMEM_LIMIT_EXPERTS),
        name="moe_down",
    )(block_e, n_used, hid, w_dn, bd)


def _combine_kernel(dest_ref, next_ref, gate_ref, y_hbm, x_ref, m_ref, g_ref, b_ref, o_ref, buf, sems):
    i = pl.program_id(0)
    slot = i % 2

    def gather(idx_ref, s):
        def issue(t, c):
            for k in range(TOP_K):
                pltpu.make_async_copy(y_hbm.at[pl.ds(idx_ref[0, 0, t * TOP_K + k], 1), :],
                                      buf.at[s, k, pl.ds(t, 1), :], sems.at[s]).start()
            return c

        lax.fori_loop(0, MOE_TOK, issue, 0)

    @pl.when(i == 0)
    def _():
        gather(dest_ref, 0)

    @pl.when(i + 1 < pl.num_programs(0))
    def _():
        gather(next_ref, 1 - slot)

    for k in range(TOP_K):
        pltpu.make_async_copy(y_hbm.at[pl.ds(0, MOE_TOK), :], buf.at[slot, k], sems.at[slot]).wait()
    gate = gate_ref[...]
    y = None
    for k in range(TOP_K):
        lo, hi = _unpack_bf16_pairs(buf[slot, k])
        hw = MOE_DOWN_TN // 2
        parts = []
        for c in range(lo.shape[1] // hw):
            parts += [lo[:, c * hw:(c + 1) * hw], hi[:, c * hw:(c + 1) * hw]]
        yk = gate[:, k:k + 1] * jnp.concatenate(parts, axis=1).astype(F32)
        y = yk if y is None else y + yk
    m = m_ref[0, 0]
    z = DEEPNORM_ALPHA * x_ref[...] + m[5:6] * y
    o_ref[...] = _ln(z) * g_ref[...] + b_ref[...]


def _combine_pn(y_rows, dest, gate, x, modt, ln_g, ln_b, mod_map):
    t = x.shape[0]
    nb = t // MOE_TOK
    dest3 = dest.reshape(nb, 1, MOE_TOK * TOP_K)
    row = pl.BlockSpec((MOE_TOK, D_MODEL), lambda i: (i, 0))
    vec = pl.BlockSpec((1, D_MODEL), lambda i: (0, 0))
    return pl.pallas_call(
        _combine_kernel,
        grid=(nb,),
        in_specs=[
            pl.BlockSpec((1, 1, MOE_TOK * TOP_K), lambda i: (i, 0, 0), memory_space=pltpu.SMEM),
            pl.BlockSpec((1, 1, MOE_TOK * TOP_K), lambda i: (jnp.minimum(i + 1, nb - 1), 0, 0),
                         memory_space=pltpu.SMEM),
            pl.BlockSpec((MOE_TOK, LANE), lambda i: (i, 0)),
            pl.BlockSpec(memory_space=pl.ANY),
            row,
            pl.BlockSpec((1, 1, 8, D_MODEL), lambda i: mod_map(i) + (0, 0)),
            vec, vec,
        ],
        out_specs=row,
        out_shape=jax.ShapeDtypeStruct((t, D_MODEL), F32),
        scratch_shapes=[pltpu.VMEM((2, TOP_K, MOE_TOK, D_MODEL // 2), U32), pltpu.SemaphoreType.DMA((2,))],
        compiler_params=_cparams(("arbitrary",)),
        name="moe_combine",
    )(dest3, dest3, gate, y_rows, x, modt, ln_g.reshape(1, D_MODEL), ln_b.reshape(1, D_MODEL))


def _rank_kernel(te_ref, tri_ref, rank_ref, cnt_ref, run_ref):
    @pl.when(pl.program_id(0) == 0)
    def _():
        run_ref[...] = jnp.zeros(run_ref.shape, F32)

    te = te_ref[...]
    lane = lax.broadcasted_iota(I32, te.shape, 1)
    base = run_ref[...]
    rank = jnp.zeros(te.shape, F32)
    for k in range(TOP_K):
        hit = te[:, k:k + 1] == lane
        onehot = jnp.where(hit, 1.0, 0.0).astype(BF16)
        before = jnp.dot(tri_ref[...], onehot, preferred_element_type=F32)
        rk = jnp.sum(jnp.where(hit, before + base, 0.0), axis=-1, keepdims=True)
        rank = jnp.where(lane == k, rk, rank)
        base = base + jnp.sum(onehot.astype(F32), axis=0, keepdims=True)
    run_ref[...] = base
    rank_ref[...] = rank.astype(I32)
    cnt_ref[...] = base.astype(I32)


def _ranks(top_e):
    t = top_e.shape[0]
    tri = (jnp.arange(MOE_TOK)[:, None] > jnp.arange(MOE_TOK)[None, :]).astype(BF16)
    return pl.pallas_call(
        _rank_kernel,
        grid=(t // MOE_TOK,),
        in_specs=[pl.BlockSpec((MOE_TOK, LANE), lambda i: (i, 0)), pl.BlockSpec((MOE_TOK, MOE_TOK), lambda i: (0, 0))],
        out_specs=[pl.BlockSpec((MOE_TOK, LANE), lambda i: (i, 0)), pl.BlockSpec((1, LANE), lambda i: (0, 0))],
        out_shape=[jax.ShapeDtypeStruct((t, LANE), I32), jax.ShapeDtypeStruct((1, LANE), I32)],
        scratch_shapes=[pltpu.VMEM((1, LANE), F32)],
        compiler_params=_cparams(("arbitrary",)),
        name="moe_rank",
    )(top_e, tri)


def _moe(h, top_e, gate, x, modt, mod_map, ln_g, ln_b, layer, w_gu, b_gu, w_dn, b_dn):
    t = h.shape[0]
    n_assign = t * TOP_K
    nb = n_assign // MOE_TM + N_EXPERTS
    rank, cnt = _ranks(top_e)
    counts = cnt[0, :N_EXPERTS]
    padded = (counts + MOE_TM - 1) // MOE_TM * MOE_TM
    pad_end = jnp.cumsum(padded)
    pad_start = jnp.concatenate([jnp.zeros((1,), I32), pad_end]).astype(I32)
    te4 = top_e[:, :TOP_K]
    hit = te4[:, :, None] == jnp.arange(N_EXPERTS, dtype=I32)[None, None, :]
    dest = (jnp.sum(jnp.where(hit, pad_start[None, None, :N_EXPERTS], 0), axis=-1) + rank[:, :TOP_K]).astype(I32)
    starts = jnp.arange(nb, dtype=I32) * MOE_TM
    block_e = jnp.minimum(jnp.sum((pad_end[None, :] <= starts[:, None]).astype(I32), axis=1), N_EXPERTS - 1)
    n_used = (pad_end[-1:] // MOE_TM).astype(I32)

    xs = _dispatch(h, dest, counts.astype(I32), pad_start, nb * MOE_TM)
    bg = b_gu[layer, :, None, 0::2]
    bl = b_gu[layer, :, None, 1::2]
    y_rows = _experts(xs, block_e, n_used, layer, w_gu, bg, bl, w_dn, b_dn[layer, :, None, :])
    return _combine_pn(y_rows, dest, gate, x, modt, ln_g, ln_b, mod_map)


def _rope_tables(n_ctx, seq):
    pos = jnp.arange(seq)
    row = (pos // GRID_W).astype(F32)
    col = (pos % GRID_W).astype(F32)
    n_freq = MLA_ROPE // 4
    freqs = ROPE_BASE ** (-jnp.arange(n_freq, dtype=F32) / n_freq)
    ar = row[:, None] * freqs
    ac = col[:, None] * freqs
    cos = jnp.concatenate([jnp.cos(ar), jnp.cos(ar), jnp.cos(ac), jnp.cos(ac)], axis=-1)
    sin = jnp.concatenate([jnp.sin(ar), jnp.sin(ar), jnp.sin(ac), jnp.sin(ac)], axis=-1)
    cos = jnp.concatenate([jnp.ones((n_ctx, MLA_ROPE), F32), cos], axis=0)
    sin = jnp.concatenate([jnp.zeros((n_ctx, MLA_ROPE), F32), sin], axis=0)
    return cos, sin


def _rot_cols(w):
    q = MLA_ROPE // 4
    a, b, c, d = w[..., :q], w[..., q:2 * q], w[..., 2 * q:3 * q], w[..., 3 * q:]
    return jnp.concatenate([-b, a, -d, c], axis=-1)


def _rmsnorm_epilogue(accs, ex):
    x = accs[0]
    return x * lax.rsqrt(jnp.mean(x * x, axis=-1, keepdims=True) + RMS_EPS) * ex[0]


def _hgrn_layer(xs, modt, mod_map, w_in, gnorm, w_o, lbs, bsz, l, n_ctx):
    h = _ln_mod(xs, modt, mod_map, 0)
    wb = w_in.astype(BF16)
    silu_ep = lambda accs, ex: _silu(accs[0])
    q = _mm(h, [wb], HG_W, 1024, BF16, silu_ep, w_col0=0, name="hg_q")
    fr = _mm(h, [wb], 2 * HG_W, 1024, F32, w_col0=HG_W, name="hg_f")
    v = _mm(h, [wb], HG_W, 1024, BF16, w_col0=3 * HG_W, name="hg_v")
    g = _mm(h, [wb], HG_W, 1024, BF16, silu_ep, w_col0=4 * HG_W, name="hg_g")
    r3 = lambda a: a.reshape(bsz, l, a.shape[-1])
    o_f = _hgrn_scan(r3(q), r3(fr), r3(v), lbs[0], False, n_ctx)
    o = _hgrn_scan(r3(q), r3(fr), r3(v), lbs[1], True, n_ctx, o_fwd=o_f, g=r3(g), gnorm=gnorm)
    return _mm(o.reshape(bsz * l, HG_W), [w_o.astype(BF16)], D_MODEL, 1024, BF16, name="hg_o")


def _mla_layer(xs, modt, mod_map, w_in, q_norm, kv_norm, w_uq, w_ukv, w_o, bsz, l, n_ctx):
    h = _ln_mod(xs, modt, mod_map, 0)
    wb = w_in.astype(BF16)
    nl = l // ROW_TILE
    vec = lambda n: ((1, n), lambda j, i: (0, 0))
    cq = _mm(h, [wb], MLA_Q_RANK, 512, BF16, _rmsnorm_epilogue,
             extras=[(q_norm.reshape(1, -1),) + vec(MLA_Q_RANK)], tm=ROW_TILE, w_col0=0, name="mla_cq")
    ckv = _mm(h, [wb], MLA_KV_RANK, 512, BF16, _rmsnorm_epilogue,
              extras=[(kv_norm.reshape(1, -1),) + vec(MLA_KV_RANK)], tm=ROW_TILE, w_col0=MLA_Q_RANK, name="mla_ckv")
    cos, sin = _rope_tables(n_ctx, l - n_ctx)
    rope_ep = lambda accs, ex: accs[0] * ex[0] + accs[1] * ex[1]
    tab = lambda n: ((ROW_TILE, n), lambda j, i: (i % nl, 0))
    ns, nc = (l - n_ctx) // ROW_TILE, n_ctx // ROW_TILE
    qtab = lambda n: ((ROW_TILE, n), lambda j, i: (nc + i % ns, 0))
    lat_rows = (bsz * (l - n_ctx), lambda i: (i // ns) * nl + nc + i % ns)
    w_kp = w_in[:, MLA_Q_RANK + MLA_KV_RANK:]
    zk = jnp.zeros((D_MODEL, LANE - MLA_ROPE), F32)
    zt = jnp.zeros((l, LANE - MLA_ROPE), F32)
    kp = _mm(h, [jnp.concatenate([w_kp, zk], 1).astype(BF16), jnp.concatenate([_rot_cols(w_kp), zk], 1).astype(BF16)],
             LANE, LANE, BF16, rope_ep,
             extras=[(jnp.concatenate([cos, zt], 1),) + tab(LANE), (jnp.concatenate([sin, zt], 1),) + tab(LANE)],
             tm=ROW_TILE, name="mla_kp")
    wq = w_uq.reshape(MLA_Q_RANK, MLA_HEADS, MLA_NOPE + MLA_ROPE)
    zq = jnp.zeros((MLA_Q_RANK, MLA_HEADS, MLA_QK_PAD - MLA_NOPE - MLA_ROPE), F32)
    wqa = jnp.concatenate([wq, zq], -1).reshape(MLA_Q_RANK, -1).astype(BF16)
    wqb = jnp.concatenate([jnp.zeros_like(wq[..., :MLA_NOPE]), _rot_cols(wq[..., MLA_NOPE:]), zq], -1)
    wqb = wqb.reshape(MLA_Q_RANK, -1).astype(BF16)
    zt = jnp.zeros((l, MLA_QK_PAD - MLA_NOPE - MLA_ROPE), F32)
    q_scale = MLA_SCALE * LOG2_E
    cq_tab = jnp.concatenate([jnp.ones((l, MLA_NOPE), F32), cos, zt], 1) * q_scale
    sq_tab = jnp.concatenate([jnp.zeros((l, MLA_NOPE), F32), sin, zt], 1) * q_scale
    hq = 8
    qf = _mm(cq, [wqa, wqb], MLA_HEADS * MLA_QK_PAD, hq * MLA_QK_PAD, BF16, rope_ep,
             extras=[(jnp.tile(cq_tab, (1, hq)),) + qtab(hq * MLA_QK_PAD),
                     (jnp.tile(sq_tab, (1, hq)),) + qtab(hq * MLA_QK_PAD)], tm=ROW_TILE, rows=lat_rows,
             name="mla_q")
    kv = _mm(ckv, [w_ukv.astype(BF16)], MLA_HEADS * (MLA_NOPE + MLA_V), 1024, BF16, name="mla_kv")
    r3 = lambda a: a.reshape(bsz, l, a.shape[-1])
    o = _attention(qf.reshape(bsz, l - n_ctx, -1), r3(kv), r3(kp))
    return _mm(o.reshape(bsz * (l - n_ctx), MLA_HEADS * MLA_V), [w_o.astype(BF16)], D_MODEL, 1024, BF16, name="mla_o")


def kernel(x, c, ctx, c_ctx, ada_w, ada_b, ln_g, ln_b, hg_w_in, hg_gnorm, hg_lb_logits, hg_w_o, mla_w_in,
           mla_q_norm, mla_kv_norm, mla_w_uq, mla_w_ukv, mla_w_o, router_w, router_b, exp_w_gu, exp_b_gu,
           exp_w_dn, exp_b_dn):
    bsz, seq, d = x.shape
    n_ctx = ctx.shape[1]
    l = n_ctx + seq
    nl = l // ROW_TILE
    ns = seq // ROW_TILE
    nc = n_ctx // ROW_TILE

    cc = jnp.zeros((8, d), F32).at[:bsz].set(c).at[bsz].set(c_ctx)
    mods = _ada(cc, ada_w, ada_b).reshape(DEPTH, 8, N_MOD, d)
    pad = jnp.zeros((DEPTH, bsz, 8 - N_MOD, d), F32)
    m_lat = jnp.concatenate([mods[:, :bsz], pad], axis=2)
    m_ctx = jnp.concatenate([jnp.broadcast_to(mods[:, bsz:bsz + 1], (DEPTH, bsz, N_MOD, d)), pad], axis=2)
    modt = jnp.stack([m_ctx, m_lat], axis=2)
    lower = jnp.cumsum(jax.nn.softmax(hg_lb_logits.astype(F32), axis=0), axis=0)

    xs = jnp.concatenate([ctx, x], axis=1).reshape(bsz * l, d)

    def map_all(r):
        per = l // r
        return lambda i: (i // per, ((i % per) >= n_ctx // r).astype(I32))

    def map_lat(r):
        per = seq // r
        return lambda i: (i // per, 1)

    y = _hgrn_layer(xs, modt[0], map_all(ROW_TILE), hg_w_in[0], hg_gnorm[0], hg_w_o[0], lower[0], bsz, l, n_ctx)
    xs, h, te, tg = _pn_router(xs, y, modt[0], ln_g[0, 0], ln_b[0, 0], router_w[0], router_b[0], bsz * l,
                               lambda i: i, map_all(ROW_TILE))
    xs = _moe(h, te, tg, xs, modt[0], map_all(MOE_TOK), ln_g[0, 1], ln_b[0, 1],
              0, exp_w_gu, exp_b_gu, exp_w_dn, exp_b_dn)

    y = _mla_layer(xs, modt[1], map_all(ROW_TILE), mla_w_in[0], mla_q_norm[0], mla_kv_norm[0], mla_w_uq[0],
                   mla_w_ukv[0], mla_w_o[0], bsz, l, n_ctx)
    xl, h, te, tg = _pn_router(xs, y, modt[1], ln_g[1, 0], ln_b[1, 0], router_w[1], router_b[1], bsz * seq,
                               lambda i: (i // ns) * nl + nc + i % ns, map_lat(ROW_TILE))
    out = _moe(h, te, tg, xl, modt[1], map_lat(MOE_TOK), ln_g[1, 1], ln_b[1, 1],
               1, exp_w_gu, exp_b_gu, exp_w_dn, exp_b_dn)
    return out.reshape(bsz, seq, d)
```

```python
import functools

import jax
import jax.numpy as jnp
from jax import lax
from jax.experimental import pallas as pl
from jax.experimental.pallas import tpu as pltpu

F32 = jnp.float32
BF16 = jnp.bfloat16
I32 = jnp.int32
U32 = jnp.uint32

D_MODEL = 2048
DEPTH = 2
GRID_W = 64
N_MOD = 6

HG_HEADS = 16
HG_DK = 128
HG_W = HG_HEADS * HG_DK
HG_SUB = 16

MLA_HEADS = 16
MLA_Q_RANK = 512
MLA_KV_RANK = 512
MLA_NOPE = 128
MLA_ROPE = 64
MLA_V = 128
MLA_QK_PAD = 256
MLA_SCALE = (MLA_NOPE + MLA_ROPE) ** -0.5
ROPE_BASE = 10000.0
LOG2_E = 1.4426950408889634

N_EXPERTS = 32
TOP_K = 4
D_EXPERT = 2048
SWIGLU_LIMIT = 7.0
SWIGLU_ALPHA = 1.702

DEEPNORM_ALPHA = (2 * DEPTH) ** 0.25
LN_EPS = 1e-5
RMS_EPS = 1e-6

LANE = 128
MXU_N = 256
ROW_TILE = 256
MOE_TM = 512
MOE_TOK = 128
MOE_DOWN_TN = 2048
ATTN_CHUNKS = 4
VMEM_LIMIT = 48 * 1024 * 1024
VMEM_LIMIT_EXPERTS = 60 * 1024 * 1024

_HIGHEST = lax.Precision.HIGHEST


def _cparams(sem, vmem=VMEM_LIMIT):
    return pltpu.CompilerParams(dimension_semantics=sem, vmem_limit_bytes=vmem)


def _ln(x):
    mu = jnp.mean(x, axis=-1, keepdims=True)
    xc = x - mu
    var = jnp.mean(xc * xc, axis=-1, keepdims=True)
    return xc * lax.rsqrt(var + LN_EPS)


def _silu(x):
    return x * jax.nn.sigmoid(x)


def _pack_bf16_pairs(x):
    n = x.shape[1] // 2
    bits = lax.bitcast_convert_type(x.astype(BF16).astype(F32), U32)
    return (bits[:, :n] >> 16) | (bits[:, n:] & jnp.uint32(0xFFFF0000))


def _unpack_bf16_pairs(w):
    lo = lax.bitcast_convert_type(w << 16, F32).astype(BF16)
    hi = lax.bitcast_convert_type(w & jnp.uint32(0xFFFF0000), F32).astype(BF16)
    return lo, hi


def _ada_kernel(c_ref, w_ref, b_ref, o_ref):
    s = _silu(c_ref[...])
    o_ref[0] = jnp.dot(s, w_ref[0], precision=_HIGHEST, preferred_element_type=F32) + b_ref[0]


def _ada(cc, ada_w, ada_b):
    n = N_MOD * D_MODEL
    tn = 1024
    return pl.pallas_call(
        _ada_kernel,
        grid=(DEPTH, n // tn),
        in_specs=[
            pl.BlockSpec((8, D_MODEL), lambda i, j: (0, 0)),
            pl.BlockSpec((1, D_MODEL, tn), lambda i, j: (i, 0, j)),
            pl.BlockSpec((1, 1, tn), lambda i, j: (i, 0, j)),
        ],
        out_specs=pl.BlockSpec((1, 8, tn), lambda i, j: (i, 0, j)),
        out_shape=jax.ShapeDtypeStruct((DEPTH, 8, n), F32),
        compiler_params=_cparams(("parallel", "parallel")),
        name="ada",
    )(cc, ada_w, ada_b.reshape(DEPTH, 1, n))


def _lnmod_kernel(x_ref, m_ref, o_ref, *, si):
    m = m_ref[0, 0]
    o_ref[...] = (_ln(x_ref[...]) * (1.0 + m[si + 1:si + 2]) + m[si:si + 1]).astype(o_ref.dtype)


def _ln_mod(x, modt, mod_map, si):
    t = x.shape[0]
    return pl.pallas_call(
        functools.partial(_lnmod_kernel, si=si),
        grid=(t // ROW_TILE,),
        in_specs=[
            pl.BlockSpec((ROW_TILE, D_MODEL), lambda i: (i, 0)),
            pl.BlockSpec((1, 1, 8, D_MODEL), lambda i: mod_map(i) + (0, 0)),
        ],
        out_specs=pl.BlockSpec((ROW_TILE, D_MODEL), lambda i: (i, 0)),
        out_shape=jax.ShapeDtypeStruct((t, D_MODEL), BF16),
        compiler_params=_cparams(("parallel",)),
        name="ln_mod",
    )(x, modt)


def _pn_router_kernel(x_ref, y_ref, m_ref, g_ref, b_ref, rw_ref, rb_ref, xn_ref, h_ref, te_ref, tg_ref):
    m = m_ref[0, 0]
    z = DEEPNORM_ALPHA * x_ref[...] + m[2:3] * y_ref[...].astype(F32)
    xn = _ln(z) * g_ref[...] + b_ref[...]
    xn_ref[...] = xn
    h = _ln(xn) * (1.0 + m[4:5]) + m[3:4]
    h_ref[...] = _pack_bf16_pairs(h)
    logits = jnp.dot(h, rw_ref[...], precision=_HIGHEST, preferred_element_type=F32) + rb_ref[...]
    lane = lax.broadcasted_iota(I32, logits.shape, 1).astype(F32)
    neg = jnp.float32(-jnp.inf)
    cur = jnp.where(lane < N_EXPERTS, logits, neg)
    vals, idxs = [], []
    for _ in range(TOP_K):
        mx = jnp.max(cur, axis=-1, keepdims=True)
        ix = jnp.min(jnp.where(cur == mx, lane, float(LANE)), axis=-1, keepdims=True)
        vals.append(mx)
        idxs.append(ix)
        cur = jnp.where(lane == ix, neg, cur)
    es = [jnp.exp(v - vals[0]) for v in vals]
    den = es[0] + es[1] + es[2] + es[3]
    te = jnp.zeros(logits.shape, F32)
    tg = jnp.zeros(logits.shape, F32)
    for k in range(TOP_K):
        te = jnp.where(lane == k, idxs[k], te)
        tg = jnp.where(lane == k, es[k] / den, tg)
    te_ref[...] = te.astype(I32)
    tg_ref[...] = tg


def _pn_router(x, y, modt, ln_g, ln_b, rw, rb, n_tok, x_map, mod_map):
    rwp = jnp.zeros((D_MODEL, LANE), F32).at[:, :N_EXPERTS].set(rw)
    rbp = jnp.zeros((1, LANE), F32).at[0, :N_EXPERTS].set(rb)
    row = pl.BlockSpec((ROW_TILE, D_MODEL), lambda i: (i, 0))
    vec = pl.BlockSpec((1, D_MODEL), lambda i: (0, 0))
    nar = pl.BlockSpec((ROW_TILE, LANE), lambda i: (i, 0))
    return pl.pallas_call(
        _pn_router_kernel,
        grid=(n_tok // ROW_TILE,),
        in_specs=[
            pl.BlockSpec((ROW_TILE, D_MODEL), lambda i: (x_map(i), 0)),
            row,
            pl.BlockSpec((1, 1, 8, D_MODEL), lambda i: mod_map(i) + (0, 0)),
            vec, vec,
            pl.BlockSpec((D_MODEL, LANE), lambda i: (0, 0)),
            pl.BlockSpec((1, LANE), lambda i: (0, 0)),
        ],
        out_specs=[row, pl.BlockSpec((ROW_TILE, D_MODEL // 2), lambda i: (i, 0)), nar, nar],
        out_shape=[
            jax.ShapeDtypeStruct((n_tok, D_MODEL), F32),
            jax.ShapeDtypeStruct((n_tok, D_MODEL // 2), U32),
            jax.ShapeDtypeStruct((n_tok, LANE), I32),
            jax.ShapeDtypeStruct((n_tok, LANE), F32),
        ],
        compiler_params=_cparams(("parallel",)),
        name="pn_router",
    )(x, y, modt, ln_g.reshape(1, D_MODEL), ln_b.reshape(1, D_MODEL), rwp, rbp)


def _mm_kernel(*refs, n_w, n_x, epilogue):
    a = refs[0][...]
    accs = [jnp.dot(a, w[...], preferred_element_type=F32) for w in refs[1:1 + n_w]]
    extras = [e[...] for e in refs[1 + n_w:1 + n_w + n_x]]
    o_ref = refs[-1]
    o_ref[...] = epilogue(accs, extras).astype(o_ref.dtype)


def _mm(a, ws, n_out, tn, out_dtype, epilogue=None, extras=(), tm=512, w_col0=0, rows=None, name="mm"):
    m, k = a.shape
    tm = min(tm, m)
    a_map = lambda i: i
    if rows is not None:
        m, a_map = rows
    assert m % tm == 0 and n_out % tn == 0 and w_col0 % tn == 0
    if epilogue is None:
        epilogue = lambda accs, ex: accs[0]
    c0 = w_col0 // tn
    in_specs = [pl.BlockSpec((tm, k), lambda j, i: (a_map(i), 0))]
    in_specs += [pl.BlockSpec((k, tn), lambda j, i: (0, j + c0)) for _ in ws]
    in_specs += [pl.BlockSpec(bs, im) for _, bs, im in extras]
    return pl.pallas_call(
        functools.partial(_mm_kernel, n_w=len(ws), n_x=len(extras), epilogue=epilogue),
        grid=(n_out // tn, m // tm),
        in_specs=in_specs,
        out_specs=pl.BlockSpec((tm, tn), lambda j, i: (i, j)),
        out_shape=jax.ShapeDtypeStruct((m, n_out), out_dtype),
        compiler_params=_cparams(("parallel", "parallel")),
        name=name,
    )(a, *ws, *[e[0] for e in extras])


def _scan_kernel(*refs, rev, hb, tl, final):
    n_in = 8 if final else 5
    q_ref, f_ref, v_ref, lb_ref, ones_ref = refs[:5]
    o_ref = refs[n_in]
    st_ref, nat_ref, dil_ref, p_ref, upd_ref, sb_ref = refs[n_in + 1:]
    nblk = tl // HG_SUB
    q_nat, v_nat, f_nat, qs_nat, kd_nat, o_nat = (nat_ref.at[n] for n in range(6))
    dq, dk, dv, dl, dcf, dcr = (dil_ref.at[n] for n in range(6))
    db, dother = (dcr, dcf) if rev else (dcf, dcr)
    rows = lambda i: pl.ds(i, nblk, stride=HG_SUB)
    lanes = lambda h: slice(h * HG_DK, (h + 1) * HG_DK)
    pairs = [list(range(i, HG_SUB)) if rev else list(range(i + 1)) for i in range(HG_SUB)]
    bases = [sum(len(p) for p in pairs[:i]) * nblk for i in range(HG_SUB)]

    @pl.when(pl.program_id(2) == 0)
    def _():
        st_ref[...] = jnp.zeros(st_ref.shape, F32)

    vb = v_ref[0]
    tots = []
    for h in range(hb):
        lb = lb_ref[:, lanes(h)]
        q_nat[h] = q_ref[0, :, lanes(h)].astype(F32)
        v_nat[h] = vb[:, lanes(h)].astype(F32)
        f_nat[h] = f_ref[0, :, lanes(h)]
        for i in range(HG_SUB):
            forget = lb + (1.0 - lb) * jax.nn.sigmoid(f_nat[h, rows(i), :])
            dl[h, i] = jnp.log2(forget)
            dk[h, i] = 1.0 - forget
            dq[h, i] = q_nat[h, rows(i), :]
            dv[h, i] = v_nat[h, rows(i), :]
        run = dl[h, 0]
        dcf[h, 0] = run
        for i in range(1, HG_SUB):
            run = run + dl[h, i]
            dcf[h, i] = run
        tots.append(run)
        run = dl[h, HG_SUB - 1]
        dcr[h, HG_SUB - 1] = run
        for i in range(HG_SUB - 2, -1, -1):
            run = run + dl[h, i]
            dcr[h, i] = run
        for i in range(HG_SUB):
            nxt = i - 1 if rev else i + 1
            rest = dother[h, nxt] if 0 <= nxt < HG_SUB else jnp.zeros((nblk, HG_DK), F32)
            qs_nat[h, rows(i), :] = dq[h, i] * jnp.exp2(db[h, i])
            kd_nat[h, rows(i), :] = dk[h, i] * jnp.exp2(rest)
        for i in range(HG_SUB):
            qi = dq[h, i]
            bi = db[h, i]
            for n, j in enumerate(pairs[i]):
                p_ref[pl.ds(bases[i] + n * nblk, nblk), lanes(h)] = (
                    qi * dk[h, j] * jnp.exp2(bi - db[h, j])).astype(BF16)
    ones = ones_ref[...]
    hpg = MXU_N // HG_DK
    for i in range(HG_SUB):
        for g0 in range(0, hb, hpg):
            rs = jnp.dot(p_ref[pl.ds(bases[i], len(pairs[i]) * nblk), g0 * HG_DK:(g0 + hpg) * HG_DK], ones,
                         preferred_element_type=F32)
            for h in range(g0, g0 + hpg):
                oi = rs[0:nblk, lanes(h - g0)] * dv[h, pairs[i][0]]
                for n in range(1, len(pairs[i])):
                    oi = oi + rs[n * nblk:(n + 1) * nblk, lanes(h - g0)] * dv[h, pairs[i][n]]
                o_nat[h, rows(i), :] = oi

    blk = lambda j: pl.ds(j * HG_SUB, HG_SUB)
    for j in range(nblk):
        for h in range(hb):
            upd_ref[j, h] = lax.dot_general(vb[j * HG_SUB:(j + 1) * HG_SUB, lanes(h)],
                                            kd_nat[h, blk(j), :].astype(BF16),
                                            (((0,), (0,)), ((), ())), preferred_element_type=F32)
    for h in range(hb):
        st = st_ref[h]
        for j in (range(nblk - 1, -1, -1) if rev else range(nblk)):
            sb_ref[j, h] = st.astype(BF16)
            st = st * jnp.exp2(tots[h][j:j + 1, :]) + upd_ref[j, h]
        st_ref[h] = st
    for j in range(nblk):
        for h in range(hb):
            oi = lax.dot_general(qs_nat[h, blk(j), :].astype(BF16), sb_ref[j, h],
                                 (((1,), (1,)), ((), ())), preferred_element_type=F32)
            if final:
                o_nat[h, blk(j), :] = o_nat[h, blk(j), :] + oi
            else:
                o_ref[0, blk(j), lanes(h)] = o_nat[h, blk(j), :] + oi

    if final:
        of_ref, g_ref, gw_ref = refs[5:8]
        for h in range(hb):
            oh = o_nat[h] + of_ref[0, :, lanes(h)]
            ms = jnp.mean(oh * oh, axis=-1, keepdims=True)
            o_ref[0, :, lanes(h)] = (oh * lax.rsqrt(ms + RMS_EPS) * gw_ref[:, lanes(h)]
                                     * g_ref[0, :, lanes(h)].astype(F32)).astype(o_ref.dtype)


def _hgrn_scan(q, fr, v, lb, rev, n_ctx, o_fwd=None, g=None, gnorm=None, hb=4, tl=256):
    bsz, l, _ = q.shape
    w = hb * HG_DK
    nl = l // tl
    nc = n_ctx // tl
    final = rev

    def lmap(i):
        if not rev:
            return i
        return jnp.where(i < nc, nc - 1 - i, nl - 1 - (i - nc))

    foff = (HG_W // w) if rev else 0
    tile = lambda b, h, i: (b, lmap(i), h)
    ones = (jnp.arange(MXU_N)[:, None] // HG_DK == jnp.arange(MXU_N)[None, :] // HG_DK).astype(BF16)
    in_specs = [
        pl.BlockSpec((1, tl, w), tile),
        pl.BlockSpec((1, tl, w), lambda b, h, i: (b, lmap(i), h + foff)),
        pl.BlockSpec((1, tl, w), tile),
        pl.BlockSpec((1, w), lambda b, h, i: (0, h)),
        pl.BlockSpec((MXU_N, MXU_N), lambda b, h, i: (0, 0)),
    ]
    args = [q, fr, v, lb.reshape(1, HG_W), ones]
    nblk = tl // HG_SUB
    n_pairs = HG_SUB * (HG_SUB + 1) // 2
    scratch = [pltpu.VMEM((hb, HG_DK, HG_DK), F32), pltpu.VMEM((6, hb, tl, HG_DK), F32),
               pltpu.VMEM((6, hb, HG_SUB, nblk, HG_DK), F32), pltpu.VMEM((n_pairs * nblk, w), BF16),
               pltpu.VMEM((nblk, hb, HG_DK, HG_DK), F32), pltpu.VMEM((nblk, hb, HG_DK, HG_DK), BF16)]
    if final:
        in_specs += [pl.BlockSpec((1, tl, w), tile), pl.BlockSpec((1, tl, w), tile),
                     pl.BlockSpec((1, w), lambda b, h, i: (0, h))]
        args += [o_fwd, g, gnorm.reshape(1, HG_W)]
    return pl.pallas_call(
        functools.partial(_scan_kernel, rev=rev, hb=hb, tl=tl, final=final),
        grid=(bsz, HG_W // w, nl),
        in_specs=in_specs,
        out_specs=pl.BlockSpec((1, tl, w), tile),
        out_shape=jax.ShapeDtypeStruct((bsz, l, HG_W), BF16 if final else F32),
        scratch_shapes=scratch,
        compiler_params=_cparams(("parallel", "parallel", "arbitrary")),
        name="hgrn_scan_bwd" if rev else "hgrn_scan_fwd",
    )(*args)


def _attn_kernel(q_ref, kn_ref, kp_ref, v_ref, o_ref, k_scr, v_scr):
    @pl.when(pl.program_id(2) == 0)
    def _():
        k_scr[:, 0:MLA_NOPE] = kn_ref[0]
        k_scr[:, MLA_NOPE:MLA_QK_PAD] = kp_ref[0]
        v_scr[:, 0:MLA_V] = v_ref[0]
        v_scr[:, MLA_V:] = jnp.ones((v_scr.shape[0], v_scr.shape[1] - MLA_V), v_scr.dtype)

    q = q_ref[0]
    l = k_scr.shape[0]
    step = -(-l // (ATTN_CHUNKS * MXU_N)) * MXU_N
    m = acc = None
    for c0 in range(0, l, step):
        rows = pl.ds(c0, min(step, l - c0))
        s = lax.dot_general(q, k_scr[rows, :], (((1,), (1,)), ((), ())), preferred_element_type=F32)
        mc = jnp.max(s, axis=-1, keepdims=True)
        m_new = mc if m is None else jnp.maximum(m, mc)
        pv = jnp.dot(jnp.exp2(s - m_new).astype(BF16), v_scr[rows, :], preferred_element_type=F32)
        acc = pv if m is None else acc * jnp.exp2(m - m_new) + pv
        m = m_new
    o_ref[0] = (acc[:, :MLA_V] / acc[:, MLA_V:MLA_V + 1]).astype(o_ref.dtype)


def _attention(qf, kv, kp, tq=1024):
    bsz, s, _ = qf.shape
    l = kv.shape[1]
    tq = min(tq, s)
    assert s % tq == 0
    return pl.pallas_call(
        _attn_kernel,
        grid=(bsz, MLA_HEADS, s // tq),
        in_specs=[
            pl.BlockSpec((1, tq, MLA_QK_PAD), lambda b, h, i: (b, i, h)),
            pl.BlockSpec((1, l, MLA_NOPE), lambda b, h, i: (b, 0, 2 * h)),
            pl.BlockSpec((1, l, LANE), lambda b, h, i: (b, 0, 0)),
            pl.BlockSpec((1, l, MLA_V), lambda b, h, i: (b, 0, 2 * h + 1)),
        ],
        out_specs=pl.BlockSpec((1, tq, MLA_V), lambda b, h, i: (b, i, h)),
        out_shape=jax.ShapeDtypeStruct((bsz, s, MLA_HEADS * MLA_V), BF16),
        scratch_shapes=[pltpu.VMEM((l, MLA_QK_PAD), BF16), pltpu.VMEM((l, MXU_N), BF16)],
        compiler_params=_cparams(("parallel", "parallel", "arbitrary")),
        name="mla_attn",
    )(qf, kv, kp, kv)


def _dispatch_kernel(cnt_ref, pst_ref, dest_ref, h_hbm, xs_hbm, hbuf, lsem, rsem, psem, *, nb):
    i = pl.program_id(0)
    slot = i % 3

    def load(j, s):
        return pltpu.make_async_copy(h_hbm.at[pl.ds(j * MOE_TOK, MOE_TOK), :], hbuf.at[s], lsem.at[s])

    def row_copy(s, t, dst, sem):
        return pltpu.make_async_copy(hbuf.at[s, pl.ds(t, 1), :], xs_hbm.at[pl.ds(dst, 1), :], sem)

    def wait_rows(sem):
        for _ in range(TOP_K):
            pltpu.make_async_copy(hbuf.at[0], xs_hbm.at[pl.ds(0, MOE_TOK), :], sem).wait()

    @pl.when(i == 0)
    def _():
        for j in range(min(2, nb)):
            load(j, j).start()

    load(i, slot).wait()

    def issue(t, c):
        for k in range(TOP_K):
            row_copy(slot, t, dest_ref[0, 0, t * TOP_K + k], rsem.at[i % 2]).start()
        return c

    lax.fori_loop(0, MOE_TOK, issue, 0)

    @pl.when(i == 0)
    def _():
        for e in range(N_EXPERTS):
            first = pst_ref[e] + cnt_ref[e]
            last = pst_ref[e + 1]
            lax.fori_loop(first, last, lambda r, c: (row_copy(0, 0, r, psem).start(), c)[1], 0)
            lax.fori_loop(first, last, lambda r, c: (row_copy(0, 0, r, psem).wait(), c)[1], 0)

        def block_copy(m):
            return pltpu.make_async_copy(hbuf.at[0], xs_hbm.at[pl.ds(m * MOE_TOK, MOE_TOK), :], psem)

        used = pst_ref[N_EXPERTS] // MOE_TOK
        total = xs_hbm.shape[0] // MOE_TOK
        lax.fori_loop(used, total, lambda m, c: (block_copy(m).start(), c)[1], 0)
        lax.fori_loop(used, total, lambda m, c: (block_copy(m).wait(), c)[1], 0)

    @pl.when(i > 0)
    def _():
        wait_rows(rsem.at[(i - 1) % 2])

    @pl.when(i + 2 < nb)
    def _():
        load(i + 2, (i + 2) % 3).start()

    @pl.when(i == nb - 1)
    def _():
        wait_rows(rsem.at[i % 2])


def _dispatch(h, dest, counts, pad_start, n_rows):
    t = h.shape[0]
    nb = t // MOE_TOK
    return pl.pallas_call(
        functools.partial(_dispatch_kernel, nb=nb),
        grid_spec=pltpu.PrefetchScalarGridSpec(
            num_scalar_prefetch=2,
            grid=(nb,),
            in_specs=[
                pl.BlockSpec((1, 1, MOE_TOK * TOP_K), lambda i, c, p: (i, 0, 0), memory_space=pltpu.SMEM),
                pl.BlockSpec(memory_space=pl.ANY),
            ],
            out_specs=pl.BlockSpec(memory_space=pl.ANY),
            scratch_shapes=[pltpu.VMEM((3, MOE_TOK, h.shape[1]), h.dtype), pltpu.SemaphoreType.DMA((3,)),
                            pltpu.SemaphoreType.DMA((2,)), pltpu.SemaphoreType.DMA(())],
        ),
        out_shape=jax.ShapeDtypeStruct((n_rows, h.shape[1]), h.dtype),
        compiler_params=_cparams(("arbitrary",)),
        name="moe_dispatch",
    )(counts, pad_start, dest.reshape(nb, 1, MOE_TOK * TOP_K), h)


def _expert_switch(be_ref, nu_ref):
    m = pl.program_id(1)
    live = m < nu_ref[0]
    fresh = jnp.logical_or(m == 0, be_ref[m] != be_ref[jnp.maximum(m - 1, 0)])
    return live, jnp.logical_and(live, fresh)


def _gmm1_kernel(be_ref, nu_ref, x_ref, w_ref, p_ref, bg_ref, bl_ref, o_ref, wg_scr, wl_scr):
    live, fresh = _expert_switch(be_ref, nu_ref)
    half = MXU_N // 2

    @pl.when(fresh)
    def _():
        for c in range(w_ref.shape[3] // MXU_N):
            blk = w_ref[0, 0, :, c * MXU_N:(c + 1) * MXU_N].astype(BF16)
            sp = jnp.dot(blk, p_ref[...], preferred_element_type=F32)
            wg_scr[:, c * half:(c + 1) * half] = sp[:, :half].astype(BF16)
            wl_scr[:, c * half:(c + 1) * half] = sp[:, half:].astype(BF16)

    @pl.when(live)
    def _():
        x_lo, x_hi = _unpack_bf16_pairs(x_ref[...])
        kh = x_lo.shape[1]
        g = (jnp.dot(x_lo, wg_scr[:kh, :], preferred_element_type=F32)
             + jnp.dot(x_hi, wg_scr[kh:, :], preferred_element_type=F32) + bg_ref[0])
        u = (jnp.dot(x_lo, wl_scr[:kh, :], preferred_element_type=F32)
             + jnp.dot(x_hi, wl_scr[kh:, :], preferred_element_type=F32) + bl_ref[0])
        g = jnp.minimum(g, SWIGLU_LIMIT)
        u = jnp.clip(u, -SWIGLU_LIMIT, SWIGLU_LIMIT)
        o_ref[...] = (g * jax.nn.sigmoid(SWIGLU_ALPHA * g) * (u + 1.0)).astype(o_ref.dtype)

    @pl.when(jnp.logical_not(live))
    def _():
        o_ref[...] = jnp.zeros(o_ref.shape, o_ref.dtype)


def _gmm2_kernel(be_ref, nu_ref, x_ref, w_ref, b_ref, o_ref, w_scr):
    live, fresh = _expert_switch(be_ref, nu_ref)

    @pl.when(fresh)
    def _():
        w_scr[...] = w_ref[0, 0].astype(BF16)

    @pl.when(live)
    def _():
        o_ref[...] = _pack_bf16_pairs(jnp.dot(x_ref[...], w_scr[...], preferred_element_type=F32) + b_ref[0])

    @pl.when(jnp.logical_not(live))
    def _():
        o_ref[...] = jnp.zeros(o_ref.shape, o_ref.dtype)


def _experts(xs, block_e, n_used, layer, w_gu, bg, bl, w_dn, bd, tn=1024, tn2=MOE_DOWN_TN):
    n_rows = xs.shape[0]
    nb = n_rows // MOE_TM
    idx = jnp.arange(MXU_N)
    perm = (idx[:, None] == jnp.where(idx < MXU_N // 2, 2 * idx, 2 * (idx - MXU_N // 2) + 1)[None, :]).astype(BF16)

    def mrow(n, m, be, nu):
        return (jnp.minimum(m, nu[0] - 1), 0)

    def bmap(n, m, be, nu):
        return (be[jnp.minimum(m, nu[0] - 1)], 0, n)

    def wmap(n, m, be, nu):
        return (layer, be[jnp.minimum(m, nu[0] - 1)], 0, n)

    hid = pl.pallas_call(
        _gmm1_kernel,
        grid_spec=pltpu.PrefetchScalarGridSpec(
            num_scalar_prefetch=2,
            grid=(D_EXPERT // tn, nb),
            in_specs=[
                pl.BlockSpec((MOE_TM, D_MODEL // 2), mrow),
                pl.BlockSpec((1, 1, D_MODEL, 2 * tn), wmap),
                pl.BlockSpec((MXU_N, MXU_N), lambda n, m, be, nu: (0, 0)),
                pl.BlockSpec((1, 1, tn), bmap),
                pl.BlockSpec((1, 1, tn), bmap),
            ],
            out_specs=pl.BlockSpec((MOE_TM, tn), lambda n, m, be, nu: (m, n)),
            scratch_shapes=[pltpu.VMEM((D_MODEL, tn), BF16), pltpu.VMEM((D_MODEL, tn), BF16)],
        ),
        out_shape=jax.ShapeDtypeStruct((n_rows, D_EXPERT), BF16),
        compiler_params=_cparams(("parallel", "arbitrary"), VMEM_LIMIT_EXPERTS),
        name="moe_gate_up",
    )(block_e, n_used, xs, w_gu, perm, bg, bl)
    return pl.pallas_call(
        _gmm2_kernel,
        grid_spec=pltpu.PrefetchScalarGridSpec(
            num_scalar_prefetch=2,
            grid=(D_MODEL // tn2, nb),
            in_specs=[
                pl.BlockSpec((MOE_TM, D_EXPERT), mrow),
                pl.BlockSpec((1, 1, D_EXPERT, tn2), wmap),
                pl.BlockSpec((1, 1, tn2), bmap),
            ],
            out_specs=pl.BlockSpec((MOE_TM, tn2 // 2), lambda n, m, be, nu: (m, n)),
            scratch_shapes=[pltpu.VMEM((D_EXPERT, tn2), BF16)],
        ),
        out_shape=jax.ShapeDtypeStruct((n_rows, D_MODEL // 2), U32),
        compiler_params=_cparams(("parallel", "arbitrary"), VMEM_LIMIT_EXPERTS),
        name="moe_down",
    )(block_e, n_used, hid, w_dn, bd)


def _combine_kernel(dest_ref, next_ref, gate_ref, y_hbm, x_ref, m_ref, g_ref, b_ref, *rest, with_next):
    if with_next:
        m2_ref, o_ref, h_ref, buf, sems = rest
    else:
        o_ref, buf, sems = rest
    i = pl.program_id(0)
    slot = i % 2

    def gather(idx_ref, s):
        def issue(t, c):
            for k in range(TOP_K):
                pltpu.make_async_copy(y_hbm.at[pl.ds(idx_ref[0, 0, t * TOP_K + k], 1), :],
                                      buf.at[s, k, pl.ds(t, 1), :], sems.at[s]).start()
            return c

        lax.fori_loop(0, MOE_TOK, issue, 0)

    @pl.when(i == 0)
    def _():
        gather(dest_ref, 0)

    @pl.when(i + 1 < pl.num_programs(0))
    def _():
        gather(next_ref, 1 - slot)

    for k in range(TOP_K):
        pltpu.make_async_copy(y_hbm.at[pl.ds(0, MOE_TOK), :], buf.at[slot, k], sems.at[slot]).wait()
    gate = gate_ref[...]
    y = None
    for k in range(TOP_K):
        lo, hi = _unpack_bf16_pairs(buf[slot, k])
        hw = MOE_DOWN_TN // 2
        parts = []
        for c in range(lo.shape[1] // hw):
            parts += [lo[:, c * hw:(c + 1) * hw], hi[:, c * hw:(c + 1) * hw]]
        yk = gate[:, k:k + 1] * jnp.concatenate(parts, axis=1).astype(F32)
        y = yk if y is None else y + yk
    m = m_ref[0, 0]
    z = DEEPNORM_ALPHA * x_ref[...] + m[5:6] * y
    xn = _ln(z) * g_ref[...] + b_ref[...]
    o_ref[...] = xn
    if with_next:
        m2 = m2_ref[0, 0]
        h_ref[...] = (_ln(xn) * (1.0 + m2[1:2]) + m2[0:1]).astype(h_ref.dtype)


def _combine_pn(y_rows, dest, gate, x, modt, ln_g, ln_b, mod_map, next_modt=None):
    t = x.shape[0]
    nb = t // MOE_TOK
    dest3 = dest.reshape(nb, 1, MOE_TOK * TOP_K)
    row = pl.BlockSpec((MOE_TOK, D_MODEL), lambda i: (i, 0))
    vec = pl.BlockSpec((1, D_MODEL), lambda i: (0, 0))
    mod = pl.BlockSpec((1, 1, 8, D_MODEL), lambda i: mod_map(i) + (0, 0))
    with_next = next_modt is not None
    out_shape = jax.ShapeDtypeStruct((t, D_MODEL), F32)
    return pl.pallas_call(
        functools.partial(_combine_kernel, with_next=with_next),
        grid=(nb,),
        in_specs=[
            pl.BlockSpec((1, 1, MOE_TOK * TOP_K), lambda i: (i, 0, 0), memory_space=pltpu.SMEM),
            pl.BlockSpec((1, 1, MOE_TOK * TOP_K), lambda i: (jnp.minimum(i + 1, nb - 1), 0, 0),
                         memory_space=pltpu.SMEM),
            pl.BlockSpec((MOE_TOK, LANE), lambda i: (i, 0)),
            pl.BlockSpec(memory_space=pl.ANY),
            row,
            mod,
            vec, vec,
        ] + ([mod] if with_next else []),
        out_specs=[row, row] if with_next else row,
        out_shape=[out_shape, jax.ShapeDtypeStruct((t, D_MODEL), BF16)] if with_next else out_shape,
        scratch_shapes=[pltpu.VMEM((2, TOP_K, MOE_TOK, D_MODEL // 2), U32), pltpu.SemaphoreType.DMA((2,))],
        compiler_params=_cparams(("arbitrary",)),
        name="moe_combine",
    )(dest3, dest3, gate, y_rows, x, modt, ln_g.reshape(1, D_MODEL), ln_b.reshape(1, D_MODEL),
      *([next_modt] if with_next else []))


def _rank_kernel(te_ref, tri_ref, rank_ref, cnt_ref, run_ref):
    @pl.when(pl.program_id(0) == 0)
    def _():
        run_ref[...] = jnp.zeros(run_ref.shape, F32)

    te = te_ref[...]
    lane = lax.broadcasted_iota(I32, te.shape, 1)
    base = run_ref[...]
    rank = jnp.zeros(te.shape, F32)
    for k in range(TOP_K):
        hit = te[:, k:k + 1] == lane
        onehot = jnp.where(hit, 1.0, 0.0).astype(BF16)
        before = jnp.dot(tri_ref[...], onehot, preferred_element_type=F32)
        rk = jnp.sum(jnp.where(hit, before + base, 0.0), axis=-1, keepdims=True)
        rank = jnp.where(lane == k, rk, rank)
        base = base + jnp.sum(onehot.astype(F32), axis=0, keepdims=True)
    run_ref[...] = base
    rank_ref[...] = rank.astype(I32)
    cnt_ref[...] = base.astype(I32)


def _ranks(top_e):
    t = top_e.shape[0]
    tri = (jnp.arange(MOE_TOK)[:, None] > jnp.arange(MOE_TOK)[None, :]).astype(BF16)
    return pl.pallas_call(
        _rank_kernel,
        grid=(t // MOE_TOK,),
        in_specs=[pl.BlockSpec((MOE_TOK, LANE), lambda i: (i, 0)), pl.BlockSpec((MOE_TOK, MOE_TOK), lambda i: (0, 0))],
        out_specs=[pl.BlockSpec((MOE_TOK, LANE), lambda i: (i, 0)), pl.BlockSpec((1, LANE), lambda i: (0, 0))],
        out_shape=[jax.ShapeDtypeStruct((t, LANE), I32), jax.ShapeDtypeStruct((1, LANE), I32)],
        scratch_shapes=[pltpu.VMEM((1, LANE), F32)],
        compiler_params=_cparams(("arbitrary",)),
        name="moe_rank",
    )(top_e, tri)


def _moe(h, top_e, gate, x, modt, mod_map, ln_g, ln_b, layer, w_gu, b_gu, w_dn, b_dn, next_modt=None):
    t = h.shape[0]
    n_assign = t * TOP_K
    nb = n_assign // MOE_TM + N_EXPERTS
    rank, cnt = _ranks(top_e)
    counts = cnt[0, :N_EXPERTS]
    padded = (counts + MOE_TM - 1) // MOE_TM * MOE_TM
    pad_end = jnp.cumsum(padded)
    pad_start = jnp.concatenate([jnp.zeros((1,), I32), pad_end]).astype(I32)
    te4 = top_e[:, :TOP_K]
    hit = te4[:, :, None] == jnp.arange(N_EXPERTS, dtype=I32)[None, None, :]
    dest = (jnp.sum(jnp.where(hit, pad_start[None, None, :N_EXPERTS], 0), axis=-1) + rank[:, :TOP_K]).astype(I32)
    starts = jnp.arange(nb, dtype=I32) * MOE_TM
    block_e = jnp.minimum(jnp.sum((pad_end[None, :] <= starts[:, None]).astype(I32), axis=1), N_EXPERTS - 1)
    n_used = (pad_end[-1:] // MOE_TM).astype(I32)

    xs = _dispatch(h, dest, counts.astype(I32), pad_start, nb * MOE_TM)
    bg = b_gu[layer, :, None, 0::2]
    bl = b_gu[layer, :, None, 1::2]
    y_rows = _experts(xs, block_e, n_used, layer, w_gu, bg, bl, w_dn, b_dn[layer, :, None, :])
    return _combine_pn(y_rows, dest, gate, x, modt, ln_g, ln_b, mod_map, next_modt)


def _rope_tables(n_ctx, seq):
    pos = jnp.arange(seq)
    row = (pos // GRID_W).astype(F32)
    col = (pos % GRID_W).astype(F32)
    n_freq = MLA_ROPE // 4
    freqs = ROPE_BASE ** (-jnp.arange(n_freq, dtype=F32) / n_freq)
    ar = row[:, None] * freqs
    ac = col[:, None] * freqs
    cos = jnp.concatenate([jnp.cos(ar), jnp.cos(ar), jnp.cos(ac), jnp.cos(ac)], axis=-1)
    sin = jnp.concatenate([jnp.sin(ar), jnp.sin(ar), jnp.sin(ac), jnp.sin(ac)], axis=-1)
    cos = jnp.concatenate([jnp.ones((n_ctx, MLA_ROPE), F32), cos], axis=0)
    sin = jnp.concatenate([jnp.zeros((n_ctx, MLA_ROPE), F32), sin], axis=0)
    return cos, sin


def _rot_cols(w):
    q = MLA_ROPE // 4
    a, b, c, d = w[..., :q], w[..., q:2 * q], w[..., 2 * q:3 * q], w[..., 3 * q:]
    return jnp.concatenate([-b, a, -d, c], axis=-1)


def _rmsnorm_epilogue(accs, ex):
    x = accs[0]
    return x * lax.rsqrt(jnp.mean(x * x, axis=-1, keepdims=True) + RMS_EPS) * ex[0]


def _hgrn_layer(xs, modt, mod_map, w_in, gnorm, w_o, lbs, bsz, l, n_ctx):
    h = _ln_mod(xs, modt, mod_map, 0)
    wb = w_in.astype(BF16)
    silu_ep = lambda accs, ex: _silu(accs[0])
    q = _mm(h, [wb], HG_W, 1024, BF16, silu_ep, w_col0=0, name="hg_q")
    fr = _mm(h, [wb], 2 * HG_W, 1024, F32, w_col0=HG_W, name="hg_f")
    v = _mm(h, [wb], HG_W, 1024, BF16, w_col0=3 * HG_W, name="hg_v")
    g = _mm(h, [wb], HG_W, 1024, BF16, silu_ep, w_col0=4 * HG_W, name="hg_g")
    r3 = lambda a: a.reshape(bsz, l, a.shape[-1])
    o_f = _hgrn_scan(r3(q), r3(fr), r3(v), lbs[0], False, n_ctx)
    o = _hgrn_scan(r3(q), r3(fr), r3(v), lbs[1], True, n_ctx, o_fwd=o_f, g=r3(g), gnorm=gnorm)
    return _mm(o.reshape(bsz * l, HG_W), [w_o.astype(BF16)], D_MODEL, 1024, BF16, name="hg_o")


def _mla_layer(h, w_in, q_norm, kv_norm, w_uq, w_ukv, w_o, bsz, l, n_ctx):
    wb = w_in.astype(BF16)
    nl = l // ROW_TILE
    vec = lambda n: ((1, n), lambda j, i: (0, 0))
    cq = _mm(h, [wb], MLA_Q_RANK, 512, BF16, _rmsnorm_epilogue,
             extras=[(q_norm.reshape(1, -1),) + vec(MLA_Q_RANK)], tm=ROW_TILE, w_col0=0, name="mla_cq")
    ckv = _mm(h, [wb], MLA_KV_RANK, 512, BF16, _rmsnorm_epilogue,
              extras=[(kv_norm.reshape(1, -1),) + vec(MLA_KV_RANK)], tm=ROW_TILE, w_col0=MLA_Q_RANK, name="mla_ckv")
    cos, sin = _rope_tables(n_ctx, l - n_ctx)
    rope_ep = lambda accs, ex: accs[0] * ex[0] + accs[1] * ex[1]
    tab = lambda n: ((ROW_TILE, n), lambda j, i: (i % nl, 0))
    ns, nc = (l - n_ctx) // ROW_TILE, n_ctx // ROW_TILE
    qtab = lambda n: ((ROW_TILE, n), lambda j, i: (nc + i % ns, 0))
    lat_rows = (bsz * (l - n_ctx), lambda i: (i // ns) * nl + nc + i % ns)
    w_kp = w_in[:, MLA_Q_RANK + MLA_KV_RANK:]
    zk = jnp.zeros((D_MODEL, LANE - MLA_ROPE), F32)
    zt = jnp.zeros((l, LANE - MLA_ROPE), F32)
    kp = _mm(h, [jnp.concatenate([w_kp, zk], 1).astype(BF16), jnp.concatenate([_rot_cols(w_kp), zk], 1).astype(BF16)],
             LANE, LANE, BF16, rope_ep,
             extras=[(jnp.concatenate([cos, zt], 1),) + tab(LANE), (jnp.concatenate([sin, zt], 1),) + tab(LANE)],
             tm=ROW_TILE, name="mla_kp")
    wq = w_uq.reshape(MLA_Q_RANK, MLA_HEADS, MLA_NOPE + MLA_ROPE)
    zq = jnp.zeros((MLA_Q_RANK, MLA_HEADS, MLA_QK_PAD - MLA_NOPE - MLA_ROPE), F32)
    wqa = jnp.concatenate([wq, zq], -1).reshape(MLA_Q_RANK, -1).astype(BF16)
    wqb = jnp.concatenate([jnp.zeros_like(wq[..., :MLA_NOPE]), _rot_cols(wq[..., MLA_NOPE:]), zq], -1)
    wqb = wqb.reshape(MLA_Q_RANK, -1).astype(BF16)
    zt = jnp.zeros((l, MLA_QK_PAD - MLA_NOPE - MLA_ROPE), F32)
    q_scale = MLA_SCALE * LOG2_E
    cq_tab = jnp.concatenate([jnp.ones((l, MLA_NOPE), F32), cos, zt], 1) * q_scale
    sq_tab = jnp.concatenate([jnp.zeros((l, MLA_NOPE), F32), sin, zt], 1) * q_scale
    hq = 8
    qf = _mm(cq, [wqa, wqb], MLA_HEADS * MLA_QK_PAD, hq * MLA_QK_PAD, BF16, rope_ep,
             extras=[(jnp.tile(cq_tab, (1, hq)),) + qtab(hq * MLA_QK_PAD),
                     (jnp.tile(sq_tab, (1, hq)),) + qtab(hq * MLA_QK_PAD)], tm=ROW_TILE, rows=lat_rows,
             name="mla_q")
    kv = _mm(ckv, [w_ukv.astype(BF16)], MLA_HEADS * (MLA_NOPE + MLA_V), 1024, BF16, name="mla_kv")
    r3 = lambda a: a.reshape(bsz, l, a.shape[-1])
    o = _attention(qf.reshape(bsz, l - n_ctx, -1), r3(kv), r3(kp))
    return _mm(o.reshape(bsz * (l - n_ctx), MLA_HEADS * MLA_V), [w_o.astype(BF16)], D_MODEL, 1024, BF16, name="mla_o")


def kernel(x, c, ctx, c_ctx, ada_w, ada_b, ln_g, ln_b, hg_w_in, hg_gnorm, hg_lb_logits, hg_w_o, mla_w_in,
           mla_q_norm, mla_kv_norm, mla_w_uq, mla_w_ukv, mla_w_o, router_w, router_b, exp_w_gu, exp_b_gu,
           exp_w_dn, exp_b_dn):
    bsz, seq, d = x.shape
    n_ctx = ctx.shape[1]
    l = n_ctx + seq
    nl = l // ROW_TILE
    ns = seq // ROW_TILE
    nc = n_ctx // ROW_TILE

    cc = jnp.zeros((8, d), F32).at[:bsz].set(c).at[bsz].set(c_ctx)
    mods = _ada(cc, ada_w, ada_b).reshape(DEPTH, 8, N_MOD, d)
    pad = jnp.zeros((DEPTH, bsz, 8 - N_MOD, d), F32)
    m_lat = jnp.concatenate([mods[:, :bsz], pad], axis=2)
    m_ctx = jnp.concatenate([jnp.broadcast_to(mods[:, bsz:bsz + 1], (DEPTH, bsz, N_MOD, d)), pad], axis=2)
    modt = jnp.stack([m_ctx, m_lat], axis=2)
    lower = jnp.cumsum(jax.nn.softmax(hg_lb_logits.astype(F32), axis=0), axis=0)

    xs = jnp.concatenate([ctx, x], axis=1).reshape(bsz * l, d)

    def map_all(r):
        per = l // r
        return lambda i: (i // per, ((i % per) >= n_ctx // r).astype(I32))

    def map_lat(r):
        per = seq // r
        return lambda i: (i // per, 1)

    y = _hgrn_layer(xs, modt[0], map_all(ROW_TILE), hg_w_in[0], hg_gnorm[0], hg_w_o[0], lower[0], bsz, l, n_ctx)
    xs, h, te, tg = _pn_router(xs, y, modt[0], ln_g[0, 0], ln_b[0, 0], router_w[0], router_b[0], bsz * l,
                               lambda i: i, map_all(ROW_TILE))
    xs, h1 = _moe(h, te, tg, xs, modt[0], map_all(MOE_TOK), ln_g[0, 1], ln_b[0, 1],
                  0, exp_w_gu, exp_b_gu, exp_w_dn, exp_b_dn, next_modt=modt[1])

    y = _mla_layer(h1, mla_w_in[0], mla_q_norm[0], mla_kv_norm[0], mla_w_uq[0],
                   mla_w_ukv[0], mla_w_o[0], bsz, l, n_ctx)
    xl, h, te, tg = _pn_router(xs, y, modt[1], ln_g[1, 0], ln_b[1, 0], router_w[1], router_b[1], bsz * seq,
                               lambda i: (i // ns) * nl + nc + i % ns, map_lat(ROW_TILE))
    out = _moe(h, te, tg, xl, modt[1], map_lat(MOE_TOK), ln_g[1, 1], ln_b[1, 1],
               1, exp_w_gu, exp_b_gu, exp_w_dn, exp_b_dn)
    return out.reshape(bsz, seq, d)
```

```python
import functools

import jax
import jax.numpy as jnp
from jax import lax
from jax.experimental import pallas as pl
from jax.experimental.pallas import tpu as pltpu

F32 = jnp.float32
BF16 = jnp.bfloat16
I32 = jnp.int32
U32 = jnp.uint32

D_MODEL = 2048
DEPTH = 2
GRID_W = 64
N_MOD = 6

HG_HEADS = 16
HG_DK = 128
HG_W = HG_HEADS * HG_DK
HG_SUB = 16

MLA_HEADS = 16
MLA_Q_RANK = 512
MLA_KV_RANK = 512
MLA_NOPE = 128
MLA_ROPE = 64
MLA_V = 128
MLA_QK_PAD = 256
MLA_SCALE = (MLA_NOPE + MLA_ROPE) ** -0.5
ROPE_BASE = 10000.0
LOG2_E = 1.4426950408889634

N_EXPERTS = 32
TOP_K = 4
D_EXPERT = 2048
SWIGLU_LIMIT = 7.0
SWIGLU_ALPHA = 1.702

DEEPNORM_ALPHA = (2 * DEPTH) ** 0.25
LN_EPS = 1e-5
RMS_EPS = 1e-6

LANE = 128
MXU_N = 256
MM_TN = 2048
ROW_TILE = 256
MOE_TM = 512
MOE_TOK = 128
MOE_DOWN_TN = 2048
ATTN_CHUNKS = 4
VMEM_LIMIT = 48 * 1024 * 1024
VMEM_LIMIT_EXPERTS = 60 * 1024 * 1024

_HIGHEST = lax.Precision.HIGHEST


def _cparams(sem, vmem=VMEM_LIMIT):
    return pltpu.CompilerParams(dimension_semantics=sem, vmem_limit_bytes=vmem)


def _ln(x):
    mu = jnp.mean(x, axis=-1, keepdims=True)
    xc = x - mu
    var = jnp.mean(xc * xc, axis=-1, keepdims=True)
    return xc * lax.rsqrt(var + LN_EPS)


def _silu(x):
    return x * jax.nn.sigmoid(x)


def _pack_bf16_pairs(x):
    n = x.shape[1] // 2
    bits = lax.bitcast_convert_type(x.astype(BF16).astype(F32), U32)
    return (bits[:, :n] >> 16) | (bits[:, n:] & jnp.uint32(0xFFFF0000))


def _unpack_bf16_pairs(w):
    lo = lax.bitcast_convert_type(w << 16, F32).astype(BF16)
    hi = lax.bitcast_convert_type(w & jnp.uint32(0xFFFF0000), F32).astype(BF16)
    return lo, hi


def _ada_kernel(c_ref, w_ref, b_ref, o_ref):
    s = _silu(c_ref[...])
    o_ref[0] = jnp.dot(s, w_ref[0], precision=_HIGHEST, preferred_element_type=F32) + b_ref[0]


def _ada(cc, ada_w, ada_b):
    n = N_MOD * D_MODEL
    tn = 1024
    return pl.pallas_call(
        _ada_kernel,
        grid=(DEPTH, n // tn),
        in_specs=[
            pl.BlockSpec((8, D_MODEL), lambda i, j: (0, 0)),
            pl.BlockSpec((1, D_MODEL, tn), lambda i, j: (i, 0, j)),
            pl.BlockSpec((1, 1, tn), lambda i, j: (i, 0, j)),
        ],
        out_specs=pl.BlockSpec((1, 8, tn), lambda i, j: (i, 0, j)),
        out_shape=jax.ShapeDtypeStruct((DEPTH, 8, n), F32),
        compiler_params=_cparams(("parallel", "parallel")),
        name="ada",
    )(cc, ada_w, ada_b.reshape(DEPTH, 1, n))


def _lnmod_kernel(x_ref, m_ref, o_ref, *, si):
    m = m_ref[0, 0]
    o_ref[...] = (_ln(x_ref[...]) * (1.0 + m[si + 1:si + 2]) + m[si:si + 1]).astype(o_ref.dtype)


def _ln_mod(x, modt, mod_map, si):
    t = x.shape[0]
    return pl.pallas_call(
        functools.partial(_lnmod_kernel, si=si),
        grid=(t // ROW_TILE,),
        in_specs=[
            pl.BlockSpec((ROW_TILE, D_MODEL), lambda i: (i, 0)),
            pl.BlockSpec((1, 1, 8, D_MODEL), lambda i: mod_map(i) + (0, 0)),
        ],
        out_specs=pl.BlockSpec((ROW_TILE, D_MODEL), lambda i: (i, 0)),
        out_shape=jax.ShapeDtypeStruct((t, D_MODEL), BF16),
        compiler_params=_cparams(("parallel",)),
        name="ln_mod",
    )(x, modt)


def _pn_router_kernel(x_ref, y_ref, m_ref, g_ref, b_ref, rw_ref, rb_ref, xn_ref, h_ref, te_ref, tg_ref):
    m = m_ref[0, 0]
    z = DEEPNORM_ALPHA * x_ref[...] + m[2:3] * y_ref[...].astype(F32)
    xn = _ln(z) * g_ref[...] + b_ref[...]
    xn_ref[...] = xn
    h = _ln(xn) * (1.0 + m[4:5]) + m[3:4]
    h_ref[...] = _pack_bf16_pairs(h)
    logits = jnp.dot(h, rw_ref[...], precision=_HIGHEST, preferred_element_type=F32) + rb_ref[...]
    lane = lax.broadcasted_iota(I32, logits.shape, 1).astype(F32)
    neg = jnp.float32(-jnp.inf)
    cur = jnp.where(lane < N_EXPERTS, logits, neg)
    vals, idxs = [], []
    for _ in range(TOP_K):
        mx = jnp.max(cur, axis=-1, keepdims=True)
        ix = jnp.min(jnp.where(cur == mx, lane, float(LANE)), axis=-1, keepdims=True)
        vals.append(mx)
        idxs.append(ix)
        cur = jnp.where(lane == ix, neg, cur)
    es = [jnp.exp(v - vals[0]) for v in vals]
    den = es[0] + es[1] + es[2] + es[3]
    te = jnp.zeros(logits.shape, F32)
    tg = jnp.zeros(logits.shape, F32)
    for k in range(TOP_K):
        te = jnp.where(lane == k, idxs[k], te)
        tg = jnp.where(lane == k, es[k] / den, tg)
    te_ref[...] = te.astype(I32)
    tg_ref[...] = tg


def _pn_router(x, y, modt, ln_g, ln_b, rw, rb, n_tok, x_map, mod_map):
    rwp = jnp.zeros((D_MODEL, LANE), F32).at[:, :N_EXPERTS].set(rw)
    rbp = jnp.zeros((1, LANE), F32).at[0, :N_EXPERTS].set(rb)
    row = pl.BlockSpec((ROW_TILE, D_MODEL), lambda i: (i, 0))
    vec = pl.BlockSpec((1, D_MODEL), lambda i: (0, 0))
    nar = pl.BlockSpec((ROW_TILE, LANE), lambda i: (i, 0))
    return pl.pallas_call(
        _pn_router_kernel,
        grid=(n_tok // ROW_TILE,),
        in_specs=[
            pl.BlockSpec((ROW_TILE, D_MODEL), lambda i: (x_map(i), 0)),
            row,
            pl.BlockSpec((1, 1, 8, D_MODEL), lambda i: mod_map(i) + (0, 0)),
            vec, vec,
            pl.BlockSpec((D_MODEL, LANE), lambda i: (0, 0)),
            pl.BlockSpec((1, LANE), lambda i: (0, 0)),
        ],
        out_specs=[row, pl.BlockSpec((ROW_TILE, D_MODEL // 2), lambda i: (i, 0)), nar, nar],
        out_shape=[
            jax.ShapeDtypeStruct((n_tok, D_MODEL), F32),
            jax.ShapeDtypeStruct((n_tok, D_MODEL // 2), U32),
            jax.ShapeDtypeStruct((n_tok, LANE), I32),
            jax.ShapeDtypeStruct((n_tok, LANE), F32),
        ],
        compiler_params=_cparams(("parallel",)),
        name="pn_router",
    )(x, y, modt, ln_g.reshape(1, D_MODEL), ln_b.reshape(1, D_MODEL), rwp, rbp)


def _mm_kernel(*refs, n_w, n_x, epilogue):
    a = refs[0][...]
    accs = [jnp.dot(a, w[...], preferred_element_type=F32) for w in refs[1:1 + n_w]]
    extras = [e[...] for e in refs[1 + n_w:1 + n_w + n_x]]
    o_ref = refs[-1]
    o_ref[...] = epilogue(accs, extras).astype(o_ref.dtype)


def _mm(a, ws, n_out, tn, out_dtype, epilogue=None, extras=(), tm=512, w_col0=0, rows=None, name="mm"):
    m, k = a.shape
    tm = min(tm, m)
    a_map = lambda i: i
    if rows is not None:
        m, a_map = rows
    assert m % tm == 0 and n_out % tn == 0 and w_col0 % tn == 0
    if epilogue is None:
        epilogue = lambda accs, ex: accs[0]
    c0 = w_col0 // tn
    in_specs = [pl.BlockSpec((tm, k), lambda j, i: (a_map(i), 0))]
    in_specs += [pl.BlockSpec((k, tn), lambda j, i: (0, j + c0)) for _ in ws]
    in_specs += [pl.BlockSpec(bs, im) for _, bs, im in extras]
    return pl.pallas_call(
        functools.partial(_mm_kernel, n_w=len(ws), n_x=len(extras), epilogue=epilogue),
        grid=(n_out // tn, m // tm),
        in_specs=in_specs,
        out_specs=pl.BlockSpec((tm, tn), lambda j, i: (i, j)),
        out_shape=jax.ShapeDtypeStruct((m, n_out), out_dtype),
        compiler_params=_cparams(("parallel", "parallel")),
        name=name,
    )(a, *ws, *[e[0] for e in extras])


def _scan_kernel(*refs, rev, hb, tl, final):
    n_in = 8 if final else 5
    q_ref, f_ref, v_ref, lb_ref, ones_ref = refs[:5]
    o_ref = refs[n_in]
    st_ref, nat_ref, dil_ref, p_ref, upd_ref, sb_ref = refs[n_in + 1:]
    nblk = tl // HG_SUB
    q_nat, v_nat, f_nat, qs_nat, kd_nat, o_nat = (nat_ref.at[n] for n in range(6))
    dq, dk, dv, dl, dcf, dcr = (dil_ref.at[n] for n in range(6))
    db, dother = (dcr, dcf) if rev else (dcf, dcr)
    rows = lambda i: pl.ds(i, nblk, stride=HG_SUB)
    lanes = lambda h: slice(h * HG_DK, (h + 1) * HG_DK)
    pairs = [list(range(i, HG_SUB)) if rev else list(range(i + 1)) for i in range(HG_SUB)]
    bases = [sum(len(p) for p in pairs[:i]) * nblk for i in range(HG_SUB)]

    @pl.when(pl.program_id(2) == 0)
    def _():
        st_ref[...] = jnp.zeros(st_ref.shape, F32)

    vb = v_ref[0]
    tots = []
    for h in range(hb):
        lb = lb_ref[:, lanes(h)]
        q_nat[h] = q_ref[0, :, lanes(h)].astype(F32)
        v_nat[h] = vb[:, lanes(h)].astype(F32)
        f_nat[h] = f_ref[0, :, lanes(h)]
        for i in range(HG_SUB):
            forget = lb + (1.0 - lb) * jax.nn.sigmoid(f_nat[h, rows(i), :])
            dl[h, i] = jnp.log2(forget)
            dk[h, i] = 1.0 - forget
            dq[h, i] = q_nat[h, rows(i), :]
            dv[h, i] = v_nat[h, rows(i), :]
        run = dl[h, 0]
        dcf[h, 0] = run
        for i in range(1, HG_SUB):
            run = run + dl[h, i]
            dcf[h, i] = run
        tots.append(run)
        run = dl[h, HG_SUB - 1]
        dcr[h, HG_SUB - 1] = run
        for i in range(HG_SUB - 2, -1, -1):
            run = run + dl[h, i]
            dcr[h, i] = run
        for i in range(HG_SUB):
            nxt = i - 1 if rev else i + 1
            rest = dother[h, nxt] if 0 <= nxt < HG_SUB else jnp.zeros((nblk, HG_DK), F32)
            qs_nat[h, rows(i), :] = dq[h, i] * jnp.exp2(db[h, i])
            kd_nat[h, rows(i), :] = dk[h, i] * jnp.exp2(rest)
        for i in range(HG_SUB):
            qi = dq[h, i]
            bi = db[h, i]
            for n, j in enumerate(pairs[i]):
                p_ref[pl.ds(bases[i] + n * nblk, nblk), lanes(h)] = (
                    qi * dk[h, j] * jnp.exp2(bi - db[h, j])).astype(BF16)
    ones = ones_ref[...]
    hpg = MXU_N // HG_DK
    for i in range(HG_SUB):
        for g0 in range(0, hb, hpg):
            rs = jnp.dot(p_ref[pl.ds(bases[i], len(pairs[i]) * nblk), g0 * HG_DK:(g0 + hpg) * HG_DK], ones,
                         preferred_element_type=F32)
            for h in range(g0, g0 + hpg):
                oi = rs[0:nblk, lanes(h - g0)] * dv[h, pairs[i][0]]
                for n in range(1, len(pairs[i])):
                    oi = oi + rs[n * nblk:(n + 1) * nblk, lanes(h - g0)] * dv[h, pairs[i][n]]
                o_nat[h, rows(i), :] = oi

    blk = lambda j: pl.ds(j * HG_SUB, HG_SUB)
    for j in range(nblk):
        for h in range(hb):
            upd_ref[j, h] = lax.dot_general(vb[j * HG_SUB:(j + 1) * HG_SUB, lanes(h)],
                                            kd_nat[h, blk(j), :].astype(BF16),
                                            (((0,), (0,)), ((), ())), preferred_element_type=F32)
    for h in range(hb):
        st = st_ref[h]
        for j in (range(nblk - 1, -1, -1) if rev else range(nblk)):
            sb_ref[j, h] = st.astype(BF16)
            st = st * jnp.exp2(tots[h][j:j + 1, :]) + upd_ref[j, h]
        st_ref[h] = st
    for j in range(nblk):
        for h in range(hb):
            oi = lax.dot_general(qs_nat[h, blk(j), :].astype(BF16), sb_ref[j, h],
                                 (((1,), (1,)), ((), ())), preferred_element_type=F32)
            if final:
                o_nat[h, blk(j), :] = o_nat[h, blk(j), :] + oi
            else:
                o_ref[0, blk(j), lanes(h)] = o_nat[h, blk(j), :] + oi

    if final:
        of_ref, g_ref, gw_ref = refs[5:8]
        for h in range(hb):
            oh = o_nat[h] + of_ref[0, :, lanes(h)]
            ms = jnp.mean(oh * oh, axis=-1, keepdims=True)
            o_ref[0, :, lanes(h)] = (oh * lax.rsqrt(ms + RMS_EPS) * gw_ref[:, lanes(h)]
                                     * g_ref[0, :, lanes(h)].astype(F32)).astype(o_ref.dtype)


def _hgrn_scan(q, fr, v, lb, rev, n_ctx, o_fwd=None, g=None, gnorm=None, hb=4, tl=256):
    bsz, l, _ = q.shape
    w = hb * HG_DK
    nl = l // tl
    nc = n_ctx // tl
    final = rev

    def lmap(i):
        if not rev:
            return i
        return jnp.where(i < nc, nc - 1 - i, nl - 1 - (i - nc))

    foff = (HG_W // w) if rev else 0
    tile = lambda b, h, i: (b, lmap(i), h)
    ones = (jnp.arange(MXU_N)[:, None] // HG_DK == jnp.arange(MXU_N)[None, :] // HG_DK).astype(BF16)
    in_specs = [
        pl.BlockSpec((1, tl, w), tile),
        pl.BlockSpec((1, tl, w), lambda b, h, i: (b, lmap(i), h + foff)),
        pl.BlockSpec((1, tl, w), tile),
        pl.BlockSpec((1, w), lambda b, h, i: (0, h)),
        pl.BlockSpec((MXU_N, MXU_N), lambda b, h, i: (0, 0)),
    ]
    args = [q, fr, v, lb.reshape(1, HG_W), ones]
    nblk = tl // HG_SUB
    n_pairs = HG_SUB * (HG_SUB + 1) // 2
    scratch = [pltpu.VMEM((hb, HG_DK, HG_DK), F32), pltpu.VMEM((6, hb, tl, HG_DK), F32),
               pltpu.VMEM((6, hb, HG_SUB, nblk, HG_DK), F32), pltpu.VMEM((n_pairs * nblk, w), BF16),
               pltpu.VMEM((nblk, hb, HG_DK, HG_DK), F32), pltpu.VMEM((nblk, hb, HG_DK, HG_DK), BF16)]
    if final:
        in_specs += [pl.BlockSpec((1, tl, w), tile), pl.BlockSpec((1, tl, w), tile),
                     pl.BlockSpec((1, w), lambda b, h, i: (0, h))]
        args += [o_fwd, g, gnorm.reshape(1, HG_W)]
    return pl.pallas_call(
        functools.partial(_scan_kernel, rev=rev, hb=hb, tl=tl, final=final),
        grid=(bsz, HG_W // w, nl),
        in_specs=in_specs,
        out_specs=pl.BlockSpec((1, tl, w), tile),
        out_shape=jax.ShapeDtypeStruct((bsz, l, HG_W), BF16 if final else F32),
        scratch_shapes=scratch,
        compiler_params=_cparams(("parallel", "parallel", "arbitrary")),
        name="hgrn_scan_bwd" if rev else "hgrn_scan_fwd",
    )(*args)


def _attn_kernel(q_ref, kn_ref, kp_ref, v_ref, o_ref, k_scr, v_scr):
    @pl.when(pl.program_id(2) == 0)
    def _():
        k_scr[:, 0:MLA_NOPE] = kn_ref[0]
        k_scr[:, MLA_NOPE:MLA_QK_PAD] = kp_ref[0]
        v_scr[:, 0:MLA_V] = v_ref[0]
        v_scr[:, MLA_V:] = jnp.ones((v_scr.shape[0], v_scr.shape[1] - MLA_V), v_scr.dtype)

    q = q_ref[0]
    l = k_scr.shape[0]
    step = -(-l // (ATTN_CHUNKS * MXU_N)) * MXU_N
    m = acc = None
    for c0 in range(0, l, step):
        rows = pl.ds(c0, min(step, l - c0))
        s = lax.dot_general(q, k_scr[rows, :], (((1,), (1,)), ((), ())), preferred_element_type=F32)
        mc = jnp.max(s, axis=-1, keepdims=True)
        m_new = mc if m is None else jnp.maximum(m, mc)
        pv = jnp.dot(jnp.exp2(s - m_new).astype(BF16), v_scr[rows, :], preferred_element_type=F32)
        acc = pv if m is None else acc * jnp.exp2(m - m_new) + pv
        m = m_new
    o_ref[0] = (acc[:, :MLA_V] / acc[:, MLA_V:MLA_V + 1]).astype(o_ref.dtype)


def _attention(qf, kv, kp, tq=1024):
    bsz, s, _ = qf.shape
    l = kv.shape[1]
    tq = min(tq, s)
    assert s % tq == 0
    return pl.pallas_call(
        _attn_kernel,
        grid=(bsz, MLA_HEADS, s // tq),
        in_specs=[
            pl.BlockSpec((1, tq, MLA_QK_PAD), lambda b, h, i: (b, i, h)),
            pl.BlockSpec((1, l, MLA_NOPE), lambda b, h, i: (b, 0, 2 * h)),
            pl.BlockSpec((1, l, LANE), lambda b, h, i: (b, 0, 0)),
            pl.BlockSpec((1, l, MLA_V), lambda b, h, i: (b, 0, 2 * h + 1)),
        ],
        out_specs=pl.BlockSpec((1, tq, MLA_V), lambda b, h, i: (b, i, h)),
        out_shape=jax.ShapeDtypeStruct((bsz, s, MLA_HEADS * MLA_V), BF16),
        scratch_shapes=[pltpu.VMEM((l, MLA_QK_PAD), BF16), pltpu.VMEM((l, MXU_N), BF16)],
        compiler_params=_cparams(("parallel", "parallel", "arbitrary")),
        name="mla_attn",
    )(qf, kv, kp, kv)


def _dispatch_kernel(cnt_ref, pst_ref, dest_ref, h_hbm, xs_hbm, hbuf, lsem, rsem, psem, *, nb):
    i = pl.program_id(0)
    slot = i % 3

    def load(j, s):
        return pltpu.make_async_copy(h_hbm.at[pl.ds(j * MOE_TOK, MOE_TOK), :], hbuf.at[s], lsem.at[s])

    def row_copy(s, t, dst, sem):
        return pltpu.make_async_copy(hbuf.at[s, pl.ds(t, 1), :], xs_hbm.at[pl.ds(dst, 1), :], sem)

    def wait_rows(sem):
        for _ in range(TOP_K):
            pltpu.make_async_copy(hbuf.at[0], xs_hbm.at[pl.ds(0, MOE_TOK), :], sem).wait()

    @pl.when(i == 0)
    def _():
        for j in range(min(2, nb)):
            load(j, j).start()

    load(i, slot).wait()

    def issue(t, c):
        for k in range(TOP_K):
            row_copy(slot, t, dest_ref[0, 0, t * TOP_K + k], rsem.at[i % 2]).start()
        return c

    lax.fori_loop(0, MOE_TOK, issue, 0)

    @pl.when(i == 0)
    def _():
        for e in range(N_EXPERTS):
            first = pst_ref[e] + cnt_ref[e]
            last = pst_ref[e + 1]
            lax.fori_loop(first, last, lambda r, c: (row_copy(0, 0, r, psem).start(), c)[1], 0)
            lax.fori_loop(first, last, lambda r, c: (row_copy(0, 0, r, psem).wait(), c)[1], 0)

        def block_copy(m):
            return pltpu.make_async_copy(hbuf.at[0], xs_hbm.at[pl.ds(m * MOE_TOK, MOE_TOK), :], psem)

        used = pst_ref[N_EXPERTS] // MOE_TOK
        total = xs_hbm.shape[0] // MOE_TOK
        lax.fori_loop(used, total, lambda m, c: (block_copy(m).start(), c)[1], 0)
        lax.fori_loop(used, total, lambda m, c: (block_copy(m).wait(), c)[1], 0)

    @pl.when(i > 0)
    def _():
        wait_rows(rsem.at[(i - 1) % 2])

    @pl.when(i + 2 < nb)
    def _():
        load(i + 2, (i + 2) % 3).start()

    @pl.when(i == nb - 1)
    def _():
        wait_rows(rsem.at[i % 2])


def _dispatch(h, dest, counts, pad_start, n_rows):
    t = h.shape[0]
    nb = t // MOE_TOK
    return pl.pallas_call(
        functools.partial(_dispatch_kernel, nb=nb),
        grid_spec=pltpu.PrefetchScalarGridSpec(
            num_scalar_prefetch=2,
            grid=(nb,),
            in_specs=[
                pl.BlockSpec((1, 1, MOE_TOK * TOP_K), lambda i, c, p: (i, 0, 0), memory_space=pltpu.SMEM),
                pl.BlockSpec(memory_space=pl.ANY),
            ],
            out_specs=pl.BlockSpec(memory_space=pl.ANY),
            scratch_shapes=[pltpu.VMEM((3, MOE_TOK, h.shape[1]), h.dtype), pltpu.SemaphoreType.DMA((3,)),
                            pltpu.SemaphoreType.DMA((2,)), pltpu.SemaphoreType.DMA(())],
        ),
        out_shape=jax.ShapeDtypeStruct((n_rows, h.shape[1]), h.dtype),
        compiler_params=_cparams(("arbitrary",)),
        name="moe_dispatch",
    )(counts, pad_start, dest.reshape(nb, 1, MOE_TOK * TOP_K), h)


def _expert_switch(be_ref, nu_ref):
    m = pl.program_id(1)
    live = m < nu_ref[0]
    fresh = jnp.logical_or(m == 0, be_ref[m] != be_ref[jnp.maximum(m - 1, 0)])
    return live, jnp.logical_and(live, fresh)


def _gmm1_kernel(be_ref, nu_ref, x_ref, w_ref, p_ref, bg_ref, bl_ref, o_ref, wg_scr, wl_scr):
    live, fresh = _expert_switch(be_ref, nu_ref)
    half = MXU_N // 2

    @pl.when(fresh)
    def _():
        for c in range(w_ref.shape[3] // MXU_N):
            blk = w_ref[0, 0, :, c * MXU_N:(c + 1) * MXU_N].astype(BF16)
            sp = jnp.dot(blk, p_ref[...], preferred_element_type=F32)
            wg_scr[:, c * half:(c + 1) * half] = sp[:, :half].astype(BF16)
            wl_scr[:, c * half:(c + 1) * half] = sp[:, half:].astype(BF16)

    @pl.when(live)
    def _():
        x_lo, x_hi = _unpack_bf16_pairs(x_ref[...])
        kh = x_lo.shape[1]
        g = (jnp.dot(x_lo, wg_scr[:kh, :], preferred_element_type=F32)
             + jnp.dot(x_hi, wg_scr[kh:, :], preferred_element_type=F32) + bg_ref[0])
        u = (jnp.dot(x_lo, wl_scr[:kh, :], preferred_element_type=F32)
             + jnp.dot(x_hi, wl_scr[kh:, :], preferred_element_type=F32) + bl_ref[0])
        g = jnp.minimum(g, SWIGLU_LIMIT)
        u = jnp.clip(u, -SWIGLU_LIMIT, SWIGLU_LIMIT)
        o_ref[...] = (g * jax.nn.sigmoid(SWIGLU_ALPHA * g) * (u + 1.0)).astype(o_ref.dtype)

    @pl.when(jnp.logical_not(live))
    def _():
        o_ref[...] = jnp.zeros(o_ref.shape, o_ref.dtype)


def _gmm2_kernel(be_ref, nu_ref, x_ref, w_ref, b_ref, o_ref, w_scr):
    live, fresh = _expert_switch(be_ref, nu_ref)

    @pl.when(fresh)
    def _():
        w_scr[...] = w_ref[0, 0].astype(BF16)

    @pl.when(live)
    def _():
        o_ref[...] = _pack_bf16_pairs(jnp.dot(x_ref[...], w_scr[...], preferred_element_type=F32) + b_ref[0])

    @pl.when(jnp.logical_not(live))
    def _():
        o_ref[...] = jnp.zeros(o_ref.shape, o_ref.dtype)


def _experts(xs, block_e, n_used, layer, w_gu, bg, bl, w_dn, bd, tn=1024, tn2=MOE_DOWN_TN):
    n_rows = xs.shape[0]
    nb = n_rows // MOE_TM
    idx = jnp.arange(MXU_N)
    perm = (idx[:, None] == jnp.where(idx < MXU_N // 2, 2 * idx, 2 * (idx - MXU_N // 2) + 1)[None, :]).astype(BF16)

    def mrow(n, m, be, nu):
        return (jnp.minimum(m, nu[0] - 1), 0)

    def bmap(n, m, be, nu):
        return (be[jnp.minimum(m, nu[0] - 1)], 0, n)

    def wmap(n, m, be, nu):
        return (layer, be[jnp.minimum(m, nu[0] - 1)], 0, n)

    hid = pl.pallas_call(
        _gmm1_kernel,
        grid_spec=pltpu.PrefetchScalarGridSpec(
            num_scalar_prefetch=2,
            grid=(D_EXPERT // tn, nb),
            in_specs=[
                pl.BlockSpec((MOE_TM, D_MODEL // 2), mrow),
                pl.BlockSpec((1, 1, D_MODEL, 2 * tn), wmap),
                pl.BlockSpec((MXU_N, MXU_N), lambda n, m, be, nu: (0, 0)),
                pl.BlockSpec((1, 1, tn), bmap),
                pl.BlockSpec((1, 1, tn), bmap),
            ],
            out_specs=pl.BlockSpec((MOE_TM, tn), lambda n, m, be, nu: (m, n)),
            scratch_shapes=[pltpu.VMEM((D_MODEL, tn), BF16), pltpu.VMEM((D_MODEL, tn), BF16)],
        ),
        out_shape=jax.ShapeDtypeStruct((n_rows, D_EXPERT), BF16),
        compiler_params=_cparams(("parallel", "arbitrary"), VMEM_LIMIT_EXPERTS),
        name="moe_gate_up",
    )(block_e, n_used, xs, w_gu, perm, bg, bl)
    return pl.pallas_call(
        _gmm2_kernel,
        grid_spec=pltpu.PrefetchScalarGridSpec(
            num_scalar_prefetch=2,
            grid=(D_MODEL // tn2, nb),
            in_specs=[
                pl.BlockSpec((MOE_TM, D_EXPERT), mrow),
                pl.BlockSpec((1, 1, D_EXPERT, tn2), wmap),
                pl.BlockSpec((1, 1, tn2), bmap),
            ],
            out_specs=pl.BlockSpec((MOE_TM, tn2 // 2), lambda n, m, be, nu: (m, n)),
            scratch_shapes=[pltpu.VMEM((D_EXPERT, tn2), BF16)],
        ),
        out_shape=jax.ShapeDtypeStruct((n_rows, D_MODEL // 2), U32),
        compiler_params=_cparams(("parallel", "arbitrary"), VMEM_LIMIT_EXPERTS),
        name="moe_down",
    )(block_e, n_used, hid, w_dn, bd)


def _combine_kernel(dest_ref, next_ref, gate_ref, y_hbm, x_ref, m_ref, g_ref, b_ref, *rest, with_next):
    if with_next:
        m2_ref, o_ref, h_ref, buf, sems = rest
    else:
        o_ref, buf, sems = rest
    i = pl.program_id(0)
    slot = i % 2

    def gather(idx_ref, s):
        def issue(t, c):
            for k in range(TOP_K):
                pltpu.make_async_copy(y_hbm.at[pl.ds(idx_ref[0, 0, t * TOP_K + k], 1), :],
                                      buf.at[s, k, pl.ds(t, 1), :], sems.at[s]).start()
            return c

        lax.fori_loop(0, MOE_TOK, issue, 0)

    @pl.when(i == 0)
    def _():
        gather(dest_ref, 0)

    @pl.when(i + 1 < pl.num_programs(0))
    def _():
        gather(next_ref, 1 - slot)

    for k in range(TOP_K):
        pltpu.make_async_copy(y_hbm.at[pl.ds(0, MOE_TOK), :], buf.at[slot, k], sems.at[slot]).wait()
    gate = gate_ref[...]
    y = None
    for k in range(TOP_K):
        lo, hi = _unpack_bf16_pairs(buf[slot, k])
        hw = MOE_DOWN_TN // 2
        parts = []
        for c in range(lo.shape[1] // hw):
            parts += [lo[:, c * hw:(c + 1) * hw], hi[:, c * hw:(c + 1) * hw]]
        yk = gate[:, k:k + 1] * jnp.concatenate(parts, axis=1).astype(F32)
        y = yk if y is None else y + yk
    m = m_ref[0, 0]
    z = DEEPNORM_ALPHA * x_ref[...] + m[5:6] * y
    xn = _ln(z) * g_ref[...] + b_ref[...]
    o_ref[...] = xn
    if with_next:
        m2 = m2_ref[0, 0]
        h_ref[...] = (_ln(xn) * (1.0 + m2[1:2]) + m2[0:1]).astype(h_ref.dtype)


def _combine_pn(y_rows, dest, gate, x, modt, ln_g, ln_b, mod_map, next_modt=None):
    t = x.shape[0]
    nb = t // MOE_TOK
    dest3 = dest.reshape(nb, 1, MOE_TOK * TOP_K)
    row = pl.BlockSpec((MOE_TOK, D_MODEL), lambda i: (i, 0))
    vec = pl.BlockSpec((1, D_MODEL), lambda i: (0, 0))
    mod = pl.BlockSpec((1, 1, 8, D_MODEL), lambda i: mod_map(i) + (0, 0))
    with_next = next_modt is not None
    out_shape = jax.ShapeDtypeStruct((t, D_MODEL), F32)
    return pl.pallas_call(
        functools.partial(_combine_kernel, with_next=with_next),
        grid=(nb,),
        in_specs=[
            pl.BlockSpec((1, 1, MOE_TOK * TOP_K), lambda i: (i, 0, 0), memory_space=pltpu.SMEM),
            pl.BlockSpec((1, 1, MOE_TOK * TOP_K), lambda i: (jnp.minimum(i + 1, nb - 1), 0, 0),
                         memory_space=pltpu.SMEM),
            pl.BlockSpec((MOE_TOK, LANE), lambda i: (i, 0)),
            pl.BlockSpec(memory_space=pl.ANY),
            row,
            mod,
            vec, vec,
        ] + ([mod] if with_next else []),
        out_specs=[row, row] if with_next else row,
        out_shape=[out_shape, jax.ShapeDtypeStruct((t, D_MODEL), BF16)] if with_next else out_shape,
        scratch_shapes=[pltpu.VMEM((2, TOP_K, MOE_TOK, D_MODEL // 2), U32), pltpu.SemaphoreType.DMA((2,))],
        compiler_params=_cparams(("arbitrary",)),
        name="moe_combine",
    )(dest3, dest3, gate, y_rows, x, modt, ln_g.reshape(1, D_MODEL), ln_b.reshape(1, D_MODEL),
      *([next_modt] if with_next else []))


def _rank_kernel(te_ref, tri_ref, rank_ref, cnt_ref, run_ref):
    @pl.when(pl.program_id(0) == 0)
    def _():
        run_ref[...] = jnp.zeros(run_ref.shape, F32)

    te = te_ref[...]
    lane = lax.broadcasted_iota(I32, te.shape, 1)
    base = run_ref[...]
    rank = jnp.zeros(te.shape, F32)
    for k in range(TOP_K):
        hit = te[:, k:k + 1] == lane
        onehot = jnp.where(hit, 1.0, 0.0).astype(BF16)
        before = jnp.dot(tri_ref[...], onehot, preferred_element_type=F32)
        rk = jnp.sum(jnp.where(hit, before + base, 0.0), axis=-1, keepdims=True)
        rank = jnp.where(lane == k, rk, rank)
        base = base + jnp.sum(onehot.astype(F32), axis=0, keepdims=True)
    run_ref[...] = base
    rank_ref[...] = rank.astype(I32)
    cnt_ref[...] = base.astype(I32)


def _ranks(top_e):
    t = top_e.shape[0]
    tri = (jnp.arange(MOE_TOK)[:, None] > jnp.arange(MOE_TOK)[None, :]).astype(BF16)
    return pl.pallas_call(
        _rank_kernel,
        grid=(t // MOE_TOK,),
        in_specs=[pl.BlockSpec((MOE_TOK, LANE), lambda i: (i, 0)), pl.BlockSpec((MOE_TOK, MOE_TOK), lambda i: (0, 0))],
        out_specs=[pl.BlockSpec((MOE_TOK, LANE), lambda i: (i, 0)), pl.BlockSpec((1, LANE), lambda i: (0, 0))],
        out_shape=[jax.ShapeDtypeStruct((t, LANE), I32), jax.ShapeDtypeStruct((1, LANE), I32)],
        scratch_shapes=[pltpu.VMEM((1, LANE), F32)],
        compiler_params=_cparams(("arbitrary",)),
        name="moe_rank",
    )(top_e, tri)


def _moe(h, top_e, gate, x, modt, mod_map, ln_g, ln_b, layer, w_gu, b_gu, w_dn, b_dn, next_modt=None):
    t = h.shape[0]
    n_assign = t * TOP_K
    nb = n_assign // MOE_TM + N_EXPERTS
    rank, cnt = _ranks(top_e)
    counts = cnt[0, :N_EXPERTS]
    padded = (counts + MOE_TM - 1) // MOE_TM * MOE_TM
    pad_end = jnp.cumsum(padded)
    pad_start = jnp.concatenate([jnp.zeros((1,), I32), pad_end]).astype(I32)
    te4 = top_e[:, :TOP_K]
    hit = te4[:, :, None] == jnp.arange(N_EXPERTS, dtype=I32)[None, None, :]
    dest = (jnp.sum(jnp.where(hit, pad_start[None, None, :N_EXPERTS], 0), axis=-1) + rank[:, :TOP_K]).astype(I32)
    starts = jnp.arange(nb, dtype=I32) * MOE_TM
    block_e = jnp.minimum(jnp.sum((pad_end[None, :] <= starts[:, None]).astype(I32), axis=1), N_EXPERTS - 1)
    n_used = (pad_end[-1:] // MOE_TM).astype(I32)

    xs = _dispatch(h, dest, counts.astype(I32), pad_start, nb * MOE_TM)
    bg = b_gu[layer, :, None, 0::2]
    bl = b_gu[layer, :, None, 1::2]
    y_rows = _experts(xs, block_e, n_used, layer, w_gu, bg, bl, w_dn, b_dn[layer, :, None, :])
    return _combine_pn(y_rows, dest, gate, x, modt, ln_g, ln_b, mod_map, next_modt)


def _rope_tables(n_ctx, seq):
    pos = jnp.arange(seq)
    row = (pos // GRID_W).astype(F32)
    col = (pos % GRID_W).astype(F32)
    n_freq = MLA_ROPE // 4
    freqs = ROPE_BASE ** (-jnp.arange(n_freq, dtype=F32) / n_freq)
    ar = row[:, None] * freqs
    ac = col[:, None] * freqs
    cos = jnp.concatenate([jnp.cos(ar), jnp.cos(ar), jnp.cos(ac), jnp.cos(ac)], axis=-1)
    sin = jnp.concatenate([jnp.sin(ar), jnp.sin(ar), jnp.sin(ac), jnp.sin(ac)], axis=-1)
    cos = jnp.concatenate([jnp.ones((n_ctx, MLA_ROPE), F32), cos], axis=0)
    sin = jnp.concatenate([jnp.zeros((n_ctx, MLA_ROPE), F32), sin], axis=0)
    return cos, sin


def _rot_cols(w):
    q = MLA_ROPE // 4
    a, b, c, d = w[..., :q], w[..., q:2 * q], w[..., 2 * q:3 * q], w[..., 3 * q:]
    return jnp.concatenate([-b, a, -d, c], axis=-1)


def _rmsnorm_epilogue(accs, ex):
    x = accs[0]
    return x * lax.rsqrt(jnp.mean(x * x, axis=-1, keepdims=True) + RMS_EPS) * ex[0]


def _hgrn_layer(xs, modt, mod_map, w_in, gnorm, w_o, lbs, bsz, l, n_ctx):
    h = _ln_mod(xs, modt, mod_map, 0)
    wb = w_in.astype(BF16)
    silu_ep = lambda accs, ex: _silu(accs[0])
    q = _mm(h, [wb], HG_W, MM_TN, BF16, silu_ep, w_col0=0, name="hg_q")
    fr = _mm(h, [wb], 2 * HG_W, MM_TN, F32, w_col0=HG_W, name="hg_f")
    v = _mm(h, [wb], HG_W, MM_TN, BF16, w_col0=3 * HG_W, name="hg_v")
    g = _mm(h, [wb], HG_W, MM_TN, BF16, silu_ep, w_col0=4 * HG_W, name="hg_g")
    r3 = lambda a: a.reshape(bsz, l, a.shape[-1])
    o_f = _hgrn_scan(r3(q), r3(fr), r3(v), lbs[0], False, n_ctx)
    o = _hgrn_scan(r3(q), r3(fr), r3(v), lbs[1], True, n_ctx, o_fwd=o_f, g=r3(g), gnorm=gnorm)
    return _mm(o.reshape(bsz * l, HG_W), [w_o.astype(BF16)], D_MODEL, MM_TN, BF16, name="hg_o")


def _mla_layer(h, w_in, q_norm, kv_norm, w_uq, w_ukv, w_o, bsz, l, n_ctx):
    wb = w_in.astype(BF16)
    nl = l // ROW_TILE
    vec = lambda n: ((1, n), lambda j, i: (0, 0))
    cq = _mm(h, [wb], MLA_Q_RANK, 512, BF16, _rmsnorm_epilogue,
             extras=[(q_norm.reshape(1, -1),) + vec(MLA_Q_RANK)], tm=ROW_TILE, w_col0=0, name="mla_cq")
    ckv = _mm(h, [wb], MLA_KV_RANK, 512, BF16, _rmsnorm_epilogue,
              extras=[(kv_norm.reshape(1, -1),) + vec(MLA_KV_RANK)], tm=ROW_TILE, w_col0=MLA_Q_RANK, name="mla_ckv")
    cos, sin = _rope_tables(n_ctx, l - n_ctx)
    rope_ep = lambda accs, ex: accs[0] * ex[0] + accs[1] * ex[1]
    tab = lambda n: ((ROW_TILE, n), lambda j, i: (i % nl, 0))
    ns, nc = (l - n_ctx) // ROW_TILE, n_ctx // ROW_TILE
    qtab = lambda n: ((ROW_TILE, n), lambda j, i: (nc + i % ns, 0))
    lat_rows = (bsz * (l - n_ctx), lambda i: (i // ns) * nl + nc + i % ns)
    w_kp = w_in[:, MLA_Q_RANK + MLA_KV_RANK:]
    zk = jnp.zeros((D_MODEL, LANE - MLA_ROPE), F32)
    zt = jnp.zeros((l, LANE - MLA_ROPE), F32)
    kp = _mm(h, [jnp.concatenate([w_kp, zk], 1).astype(BF16), jnp.concatenate([_rot_cols(w_kp), zk], 1).astype(BF16)],
             LANE, LANE, BF16, rope_ep,
             extras=[(jnp.concatenate([cos, zt], 1),) + tab(LANE), (jnp.concatenate([sin, zt], 1),) + tab(LANE)],
             tm=ROW_TILE, name="mla_kp")
    wq = w_uq.reshape(MLA_Q_RANK, MLA_HEADS, MLA_NOPE + MLA_ROPE)
    zq = jnp.zeros((MLA_Q_RANK, MLA_HEADS, MLA_QK_PAD - MLA_NOPE - MLA_ROPE), F32)
    wqa = jnp.concatenate([wq, zq], -1).reshape(MLA_Q_RANK, -1).astype(BF16)
    wqb = jnp.concatenate([jnp.zeros_like(wq[..., :MLA_NOPE]), _rot_cols(wq[..., MLA_NOPE:]), zq], -1)
    wqb = wqb.reshape(MLA_Q_RANK, -1).astype(BF16)
    zt = jnp.zeros((l, MLA_QK_PAD - MLA_NOPE - MLA_ROPE), F32)
    q_scale = MLA_SCALE * LOG2_E
    cq_tab = jnp.concatenate([jnp.ones((l, MLA_NOPE), F32), cos, zt], 1) * q_scale
    sq_tab = jnp.concatenate([jnp.zeros((l, MLA_NOPE), F32), sin, zt], 1) * q_scale
    hq = 8
    qf = _mm(cq, [wqa, wqb], MLA_HEADS * MLA_QK_PAD, hq * MLA_QK_PAD, BF16, rope_ep,
             extras=[(jnp.tile(cq_tab, (1, hq)),) + qtab(hq * MLA_QK_PAD),
                     (jnp.tile(sq_tab, (1, hq)),) + qtab(hq * MLA_QK_PAD)], tm=ROW_TILE, rows=lat_rows,
             name="mla_q")
    kv = _mm(ckv, [w_ukv.astype(BF16)], MLA_HEADS * (MLA_NOPE + MLA_V), MM_TN, BF16, name="mla_kv")
    r3 = lambda a: a.reshape(bsz, l, a.shape[-1])
    o = _attention(qf.reshape(bsz, l - n_ctx, -1), r3(kv), r3(kp))
    return _mm(o.reshape(bsz * (l - n_ctx), MLA_HEADS * MLA_V), [w_o.astype(BF16)], D_MODEL, MM_TN, BF16, name="mla_o")


def kernel(x, c, ctx, c_ctx, ada_w, ada_b, ln_g, ln_b, hg_w_in, hg_gnorm, hg_lb_logits, hg_w_o, mla_w_in,
           mla_q_norm, mla_kv_norm, mla_w_uq, mla_w_ukv, mla_w_o, router_w, router_b, exp_w_gu, exp_b_gu,
           exp_w_dn, exp_b_dn):
    bsz, seq, d = x.shape
    n_ctx = ctx.shape[1]
    l = n_ctx + seq
    nl = l // ROW_TILE
    ns = seq // ROW_TILE
    nc = n_ctx // ROW_TILE

    cc = jnp.zeros((8, d), F32).at[:bsz].set(c).at[bsz].set(c_ctx)
    mods = _ada(cc, ada_w, ada_b).reshape(DEPTH, 8, N_MOD, d)
    pad = jnp.zeros((DEPTH, bsz, 8 - N_MOD, d), F32)
    m_lat = jnp.concatenate([mods[:, :bsz], pad], axis=2)
    m_ctx = jnp.concatenate([jnp.broadcast_to(mods[:, bsz:bsz + 1], (DEPTH, bsz, N_MOD, d)), pad], axis=2)
    modt = jnp.stack([m_ctx, m_lat], axis=2)
    lower = jnp.cumsum(jax.nn.softmax(hg_lb_logits.astype(F32), axis=0), axis=0)

    xs = jnp.concatenate([ctx, x], axis=1).reshape(bsz * l, d)

    def map_all(r):
        per = l // r
        return lambda i: (i // per, ((i % per) >= n_ctx // r).astype(I32))

    def map_lat(r):
        per = seq // r
        return lambda i: (i // per, 1)

    y = _hgrn_layer(xs, modt[0], map_all(ROW_TILE), hg_w_in[0], hg_gnorm[0], hg_w_o[0], lower[0], bsz, l, n_ctx)
    xs, h, te, tg = _pn_router(xs, y, modt[0], ln_g[0, 0], ln_b[0, 0], router_w[0], router_b[0], bsz * l,
                               lambda i: i, map_all(ROW_TILE))
    xs, h1 = _moe(h, te, tg, xs, modt[0], map_all(MOE_TOK), ln_g[0, 1], ln_b[0, 1],
                  0, exp_w_gu, exp_b_gu, exp_w_dn, exp_b_dn, next_modt=modt[1])

    y = _mla_layer(h1, mla_w_in[0], mla_q_norm[0], mla_kv_norm[0], mla_w_uq[0],
                   mla_w_ukv[0], mla_w_o[0], bsz, l, n_ctx)
    xl, h, te, tg = _pn_router(xs, y, modt[1], ln_g[1, 0], ln_b[1, 0], router_w[1], router_b[1], bsz * seq,
                               lambda i: (i // ns) * nl + nc + i % ns, map_lat(ROW_TILE))
    out = _moe(h, te, tg, xl, modt[1], map_lat(MOE_TOK), ln_g[1, 1], ln_b[1, 1],
               1, exp_w_gu, exp_b_gu, exp_w_dn, exp_b_dn)
    return out.reshape(bsz, seq, d)
```

```python
import functools

import jax
import jax.numpy as jnp
from jax import lax
from jax.experimental import pallas as pl
from jax.experimental.pallas import tpu as pltpu

F32 = jnp.float32
BF16 = jnp.bfloat16
I32 = jnp.int32
U32 = jnp.uint32

D_MODEL = 2048
DEPTH = 2
GRID_W = 64
N_MOD = 6

HG_HEADS = 16
HG_DK = 128
HG_W = HG_HEADS * HG_DK
HG_SUB = 16

MLA_HEADS = 16
MLA_Q_RANK = 512
MLA_KV_RANK = 512
MLA_NOPE = 128
MLA_ROPE = 64
MLA_V = 128
MLA_QK_PAD = 256
MLA_SCALE = (MLA_NOPE + MLA_ROPE) ** -0.5
ROPE_BASE = 10000.0
LOG2_E = 1.4426950408889634

N_EXPERTS = 32
TOP_K = 4
D_EXPERT = 2048
SWIGLU_LIMIT = 7.0
SWIGLU_ALPHA = 1.702

DEEPNORM_ALPHA = (2 * DEPTH) ** 0.25
LN_EPS = 1e-5
RMS_EPS = 1e-6

LANE = 128
MXU_N = 256
MM_TN = 2048
ROW_TILE = 256
MOE_TM = 512
MOE_TOK = 128
MOE_DOWN_TN = 2048
ATTN_CHUNKS = 4
VMEM_LIMIT = 48 * 1024 * 1024
VMEM_LIMIT_EXPERTS = 60 * 1024 * 1024

_HIGHEST = lax.Precision.HIGHEST


def _cparams(sem, vmem=VMEM_LIMIT):
    return pltpu.CompilerParams(dimension_semantics=sem, vmem_limit_bytes=vmem)


def _ln(x):
    mu = jnp.mean(x, axis=-1, keepdims=True)
    xc = x - mu
    var = jnp.mean(xc * xc, axis=-1, keepdims=True)
    return xc * lax.rsqrt(var + LN_EPS)


def _silu(x):
    return x * jax.nn.sigmoid(x)


def _pack_bf16_pairs(x):
    n = x.shape[1] // 2
    bits = lax.bitcast_convert_type(x.astype(BF16).astype(F32), U32)
    return (bits[:, :n] >> 16) | (bits[:, n:] & jnp.uint32(0xFFFF0000))


def _unpack_bf16_pairs(w):
    lo = lax.bitcast_convert_type(w << 16, F32).astype(BF16)
    hi = lax.bitcast_convert_type(w & jnp.uint32(0xFFFF0000), F32).astype(BF16)
    return lo, hi


def _ada_kernel(c_ref, w_ref, b_ref, o_ref):
    s = _silu(c_ref[...])
    o_ref[0] = jnp.dot(s, w_ref[0], precision=_HIGHEST, preferred_element_type=F32) + b_ref[0]


def _ada(cc, ada_w, ada_b):
    n = N_MOD * D_MODEL
    tn = 1024
    return pl.pallas_call(
        _ada_kernel,
        grid=(DEPTH, n // tn),
        in_specs=[
            pl.BlockSpec((8, D_MODEL), lambda i, j: (0, 0)),
            pl.BlockSpec((1, D_MODEL, tn), lambda i, j: (i, 0, j)),
            pl.BlockSpec((1, 1, tn), lambda i, j: (i, 0, j)),
        ],
        out_specs=pl.BlockSpec((1, 8, tn), lambda i, j: (i, 0, j)),
        out_shape=jax.ShapeDtypeStruct((DEPTH, 8, n), F32),
        compiler_params=_cparams(("parallel", "parallel")),
        name="ada",
    )(cc, ada_w, ada_b.reshape(DEPTH, 1, n))


def _lnmod_kernel(x_ref, m_ref, o_ref, *, si):
    m = m_ref[0, 0]
    o_ref[...] = (_ln(x_ref[...]) * (1.0 + m[si + 1:si + 2]) + m[si:si + 1]).astype(o_ref.dtype)


def _ln_mod(x, modt, mod_map, si):
    t = x.shape[0]
    return pl.pallas_call(
        functools.partial(_lnmod_kernel, si=si),
        grid=(t // ROW_TILE,),
        in_specs=[
            pl.BlockSpec((ROW_TILE, D_MODEL), lambda i: (i, 0)),
            pl.BlockSpec((1, 1, 8, D_MODEL), lambda i: mod_map(i) + (0, 0)),
        ],
        out_specs=pl.BlockSpec((ROW_TILE, D_MODEL), lambda i: (i, 0)),
        out_shape=jax.ShapeDtypeStruct((t, D_MODEL), BF16),
        compiler_params=_cparams(("parallel",)),
        name="ln_mod",
    )(x, modt)


def _pn_router_kernel(x_ref, y_ref, m_ref, g_ref, b_ref, rw_ref, rb_ref, xn_ref, h_ref, te_ref, tg_ref):
    m = m_ref[0, 0]
    z = DEEPNORM_ALPHA * x_ref[...] + m[2:3] * y_ref[...].astype(F32)
    xn = _ln(z) * g_ref[...] + b_ref[...]
    xn_ref[...] = xn
    h = _ln(xn) * (1.0 + m[4:5]) + m[3:4]
    h_ref[...] = _pack_bf16_pairs(h)
    logits = jnp.dot(h, rw_ref[...], precision=_HIGHEST, preferred_element_type=F32) + rb_ref[...]
    lane = lax.broadcasted_iota(I32, logits.shape, 1).astype(F32)
    neg = jnp.float32(-jnp.inf)
    cur = jnp.where(lane < N_EXPERTS, logits, neg)
    vals, idxs = [], []
    for _ in range(TOP_K):
        mx = jnp.max(cur, axis=-1, keepdims=True)
        ix = jnp.min(jnp.where(cur == mx, lane, float(LANE)), axis=-1, keepdims=True)
        vals.append(mx)
        idxs.append(ix)
        cur = jnp.where(lane == ix, neg, cur)
    es = [jnp.exp(v - vals[0]) for v in vals]
    den = es[0] + es[1] + es[2] + es[3]
    te = jnp.zeros(logits.shape, F32)
    tg = jnp.zeros(logits.shape, F32)
    for k in range(TOP_K):
        te = jnp.where(lane == k, idxs[k], te)
        tg = jnp.where(lane == k, es[k] / den, tg)
    te_ref[...] = te.astype(I32)
    tg_ref[...] = tg


def _pn_router(x, y, modt, ln_g, ln_b, rw, rb, n_tok, x_map, mod_map):
    rwp = jnp.zeros((D_MODEL, LANE), F32).at[:, :N_EXPERTS].set(rw)
    rbp = jnp.zeros((1, LANE), F32).at[0, :N_EXPERTS].set(rb)
    row = pl.BlockSpec((ROW_TILE, D_MODEL), lambda i: (i, 0))
    vec = pl.BlockSpec((1, D_MODEL), lambda i: (0, 0))
    nar = pl.BlockSpec((ROW_TILE, LANE), lambda i: (i, 0))
    return pl.pallas_call(
        _pn_router_kernel,
        grid=(n_tok // ROW_TILE,),
        in_specs=[
            pl.BlockSpec((ROW_TILE, D_MODEL), lambda i: (x_map(i), 0)),
            row,
            pl.BlockSpec((1, 1, 8, D_MODEL), lambda i: mod_map(i) + (0, 0)),
            vec, vec,
            pl.BlockSpec((D_MODEL, LANE), lambda i: (0, 0)),
            pl.BlockSpec((1, LANE), lambda i: (0, 0)),
        ],
        out_specs=[row, pl.BlockSpec((ROW_TILE, D_MODEL // 2), lambda i: (i, 0)), nar, nar],
        out_shape=[
            jax.ShapeDtypeStruct((n_tok, D_MODEL), F32),
            jax.ShapeDtypeStruct((n_tok, D_MODEL // 2), U32),
            jax.ShapeDtypeStruct((n_tok, LANE), I32),
            jax.ShapeDtypeStruct((n_tok, LANE), F32),
        ],
        compiler_params=_cparams(("parallel",)),
        name="pn_router",
    )(x, y, modt, ln_g.reshape(1, D_MODEL), ln_b.reshape(1, D_MODEL), rwp, rbp)


def _mm_kernel(*refs, n_w, n_x, epilogue):
    a = refs[0][...]
    accs = [jnp.dot(a, w[...], preferred_element_type=F32) for w in refs[1:1 + n_w]]
    extras = [e[...] for e in refs[1 + n_w:1 + n_w + n_x]]
    o_ref = refs[-1]
    o_ref[...] = epilogue(accs, extras).astype(o_ref.dtype)


def _mm(a, ws, n_out, tn, out_dtype, epilogue=None, extras=(), tm=512, w_col0=0, rows=None, name="mm"):
    m, k = a.shape
    tm = min(tm, m)
    a_map = lambda i: i
    if rows is not None:
        m, a_map = rows
    assert m % tm == 0 and n_out % tn == 0 and w_col0 % tn == 0
    if epilogue is None:
        epilogue = lambda accs, ex: accs[0]
    c0 = w_col0 // tn
    in_specs = [pl.BlockSpec((tm, k), lambda j, i: (a_map(i), 0))]
    in_specs += [pl.BlockSpec((k, tn), lambda j, i: (0, j + c0)) for _ in ws]
    in_specs += [pl.BlockSpec(bs, im) for _, bs, im in extras]
    return pl.pallas_call(
        functools.partial(_mm_kernel, n_w=len(ws), n_x=len(extras), epilogue=epilogue),
        grid=(n_out // tn, m // tm),
        in_specs=in_specs,
        out_specs=pl.BlockSpec((tm, tn), lambda j, i: (i, j)),
        out_shape=jax.ShapeDtypeStruct((m, n_out), out_dtype),
        compiler_params=_cparams(("parallel", "parallel")),
        name=name,
    )(a, *ws, *[e[0] for e in extras])


def _scan_kernel(*refs, rev, hb, tl, final):
    n_in = 8 if final else 5
    q_ref, f_ref, v_ref, lb_ref, ones_ref = refs[:5]
    o_ref = refs[n_in]
    st_ref, nat_ref, dil_ref, p_ref, upd_ref, sb_ref = refs[n_in + 1:]
    nblk = tl // HG_SUB
    q_nat, v_nat, f_nat, qs_nat, kd_nat, o_nat = (nat_ref.at[n] for n in range(6))
    dq, dk, dv, dl, dcf, dcr = (dil_ref.at[n] for n in range(6))
    db, dother = (dcr, dcf) if rev else (dcf, dcr)
    rows = lambda i: pl.ds(i, nblk, stride=HG_SUB)
    lanes = lambda h: slice(h * HG_DK, (h + 1) * HG_DK)
    pairs = [list(range(i, HG_SUB)) if rev else list(range(i + 1)) for i in range(HG_SUB)]
    bases = [sum(len(p) for p in pairs[:i]) * nblk for i in range(HG_SUB)]

    @pl.when(pl.program_id(2) == 0)
    def _():
        st_ref[...] = jnp.zeros(st_ref.shape, F32)

    vb = v_ref[0]
    tots = []
    for h in range(hb):
        lb = lb_ref[:, lanes(h)]
        q_nat[h] = q_ref[0, :, lanes(h)].astype(F32)
        v_nat[h] = vb[:, lanes(h)].astype(F32)
        f_nat[h] = f_ref[0, :, lanes(h)]
        for i in range(HG_SUB):
            forget = lb + (1.0 - lb) * jax.nn.sigmoid(f_nat[h, rows(i), :])
            dl[h, i] = jnp.log2(forget)
            dk[h, i] = 1.0 - forget
            dq[h, i] = q_nat[h, rows(i), :]
            dv[h, i] = v_nat[h, rows(i), :]
        run = dl[h, 0]
        dcf[h, 0] = run
        for i in range(1, HG_SUB):
            run = run + dl[h, i]
            dcf[h, i] = run
        tots.append(run)
        run = dl[h, HG_SUB - 1]
        dcr[h, HG_SUB - 1] = run
        for i in range(HG_SUB - 2, -1, -1):
            run = run + dl[h, i]
            dcr[h, i] = run
        for i in range(HG_SUB):
            nxt = i - 1 if rev else i + 1
            rest = dother[h, nxt] if 0 <= nxt < HG_SUB else jnp.zeros((nblk, HG_DK), F32)
            qs_nat[h, rows(i), :] = dq[h, i] * jnp.exp2(db[h, i])
            kd_nat[h, rows(i), :] = dk[h, i] * jnp.exp2(rest)
        for i in range(HG_SUB):
            qi = dq[h, i]
            bi = db[h, i]
            for n, j in enumerate(pairs[i]):
                p_ref[pl.ds(bases[i] + n * nblk, nblk), lanes(h)] = (
                    qi * dk[h, j] * jnp.exp2(bi - db[h, j])).astype(BF16)
    ones = ones_ref[...]
    hpg = MXU_N // HG_DK
    for i in range(HG_SUB):
        for g0 in range(0, hb, hpg):
            rs = jnp.dot(p_ref[pl.ds(bases[i], len(pairs[i]) * nblk), g0 * HG_DK:(g0 + hpg) * HG_DK], ones,
                         preferred_element_type=F32)
            for h in range(g0, g0 + hpg):
                oi = rs[0:nblk, lanes(h - g0)] * dv[h, pairs[i][0]]
                for n in range(1, len(pairs[i])):
                    oi = oi + rs[n * nblk:(n + 1) * nblk, lanes(h - g0)] * dv[h, pairs[i][n]]
                o_nat[h, rows(i), :] = oi

    blk = lambda j: pl.ds(j * HG_SUB, HG_SUB)
    for j in range(nblk):
        for h in range(hb):
            upd_ref[j, h] = lax.dot_general(vb[j * HG_SUB:(j + 1) * HG_SUB, lanes(h)],
                                            kd_nat[h, blk(j), :].astype(BF16),
                                            (((0,), (0,)), ((), ())), preferred_element_type=F32)
    for h in range(hb):
        st = st_ref[h]
        for j in (range(nblk - 1, -1, -1) if rev else range(nblk)):
            sb_ref[j, h] = st.astype(BF16)
            st = st * jnp.exp2(tots[h][j:j + 1, :]) + upd_ref[j, h]
        st_ref[h] = st
    for j in range(nblk):
        for h in range(hb):
            oi = lax.dot_general(qs_nat[h, blk(j), :].astype(BF16), sb_ref[j, h],
                                 (((1,), (1,)), ((), ())), preferred_element_type=F32)
            if final:
                o_nat[h, blk(j), :] = o_nat[h, blk(j), :] + oi
            else:
                o_ref[0, blk(j), lanes(h)] = o_nat[h, blk(j), :] + oi

    if final:
        of_ref, g_ref, gw_ref = refs[5:8]
        for h in range(hb):
            oh = o_nat[h] + of_ref[0, :, lanes(h)]
            ms = jnp.mean(oh * oh, axis=-1, keepdims=True)
            o_ref[0, :, lanes(h)] = (oh * lax.rsqrt(ms + RMS_EPS) * gw_ref[:, lanes(h)]
                                     * g_ref[0, :, lanes(h)].astype(F32)).astype(o_ref.dtype)


def _hgrn_scan(q, fr, v, lb, rev, n_ctx, o_fwd=None, g=None, gnorm=None, hb=4, tl=256):
    bsz, l, _ = q.shape
    w = hb * HG_DK
    nl = l // tl
    nc = n_ctx // tl
    final = rev

    def lmap(i):
        if not rev:
            return i
        return jnp.where(i < nc, nc - 1 - i, nl - 1 - (i - nc))

    foff = (HG_W // w) if rev else 0
    tile = lambda b, h, i: (b, lmap(i), h)
    ones = (jnp.arange(MXU_N)[:, None] // HG_DK == jnp.arange(MXU_N)[None, :] // HG_DK).astype(BF16)
    in_specs = [
        pl.BlockSpec((1, tl, w), tile),
        pl.BlockSpec((1, tl, w), lambda b, h, i: (b, lmap(i), h + foff)),
        pl.BlockSpec((1, tl, w), tile),
        pl.BlockSpec((1, w), lambda b, h, i: (0, h)),
        pl.BlockSpec((MXU_N, MXU_N), lambda b, h, i: (0, 0)),
    ]
    args = [q, fr, v, lb.reshape(1, HG_W), ones]
    nblk = tl // HG_SUB
    n_pairs = HG_SUB * (HG_SUB + 1) // 2
    scratch = [pltpu.VMEM((hb, HG_DK, HG_DK), F32), pltpu.VMEM((6, hb, tl, HG_DK), F32),
               pltpu.VMEM((6, hb, HG_SUB, nblk, HG_DK), F32), pltpu.VMEM((n_pairs * nblk, w), BF16),
               pltpu.VMEM((nblk, hb, HG_DK, HG_DK), F32), pltpu.VMEM((nblk, hb, HG_DK, HG_DK), BF16)]
    if final:
        in_specs += [pl.BlockSpec((1, tl, w), tile), pl.BlockSpec((1, tl, w), tile),
                     pl.BlockSpec((1, w), lambda b, h, i: (0, h))]
        args += [o_fwd, g, gnorm.reshape(1, HG_W)]
    return pl.pallas_call(
        functools.partial(_scan_kernel, rev=rev, hb=hb, tl=tl, final=final),
        grid=(bsz, HG_W // w, nl),
        in_specs=in_specs,
        out_specs=pl.BlockSpec((1, tl, w), tile),
        out_shape=jax.ShapeDtypeStruct((bsz, l, HG_W), BF16 if final else F32),
        scratch_shapes=scratch,
        compiler_params=_cparams(("parallel", "parallel", "arbitrary")),
        name="hgrn_scan_bwd" if rev else "hgrn_scan_fwd",
    )(*args)


def _attn_kernel(q_ref, kn_ref, kp_ref, v_ref, o_ref, k_scr, v_scr):
    @pl.when(pl.program_id(2) == 0)
    def _():
        k_scr[:, 0:MLA_NOPE] = kn_ref[0]
        k_scr[:, MLA_NOPE:MLA_QK_PAD] = kp_ref[0]
        v_scr[:, 0:MLA_V] = v_ref[0]
        v_scr[:, MLA_V:] = jnp.ones((v_scr.shape[0], v_scr.shape[1] - MLA_V), v_scr.dtype)

    q = q_ref[0]
    l = k_scr.shape[0]
    step = -(-l // (ATTN_CHUNKS * MXU_N)) * MXU_N
    m = acc = None
    for c0 in range(0, l, step):
        rows = pl.ds(c0, min(step, l - c0))
        s = lax.dot_general(q, k_scr[rows, :], (((1,), (1,)), ((), ())), preferred_element_type=F32)
        mc = jnp.max(s, axis=-1, keepdims=True)
        m_new = mc if m is None else jnp.maximum(m, mc)
        pv = jnp.dot(jnp.exp2(s - m_new).astype(BF16), v_scr[rows, :], preferred_element_type=F32)
        acc = pv if m is None else acc * jnp.exp2(m - m_new) + pv
        m = m_new
    o_ref[0] = (acc[:, :MLA_V] / acc[:, MLA_V:MLA_V + 1]).astype(o_ref.dtype)


def _attention(qf, kv, kp, tq=1024):
    bsz, s, _ = qf.shape
    l = kv.shape[1]
    tq = min(tq, s)
    assert s % tq == 0
    return pl.pallas_call(
        _attn_kernel,
        grid=(bsz, MLA_HEADS, s // tq),
        in_specs=[
            pl.BlockSpec((1, tq, MLA_QK_PAD), lambda b, h, i: (b, i, h)),
            pl.BlockSpec((1, l, MLA_NOPE), lambda b, h, i: (b, 0, 2 * h)),
            pl.BlockSpec((1, l, LANE), lambda b, h, i: (b, 0, 0)),
            pl.BlockSpec((1, l, MLA_V), lambda b, h, i: (b, 0, 2 * h + 1)),
        ],
        out_specs=pl.BlockSpec((1, tq, MLA_V), lambda b, h, i: (b, i, h)),
        out_shape=jax.ShapeDtypeStruct((bsz, s, MLA_HEADS * MLA_V), BF16),
        scratch_shapes=[pltpu.VMEM((l, MLA_QK_PAD), BF16), pltpu.VMEM((l, MXU_N), BF16)],
        compiler_params=_cparams(("parallel", "parallel", "arbitrary")),
        name="mla_attn",
    )(qf, kv, kp, kv)


def _dispatch_kernel(cnt_ref, pst_ref, dest_ref, h_hbm, xs_hbm, hbuf, lsem, rsem, psem, *, nb):
    i = pl.program_id(0)
    slot = i % 3

    def load(j, s):
        return pltpu.make_async_copy(h_hbm.at[pl.ds(j * MOE_TOK, MOE_TOK), :], hbuf.at[s], lsem.at[s])

    def row_copy(s, t, dst, sem):
        return pltpu.make_async_copy(hbuf.at[s, pl.ds(t, 1), :], xs_hbm.at[pl.ds(dst, 1), :], sem)

    def wait_rows(sem):
        for _ in range(TOP_K):
            pltpu.make_async_copy(hbuf.at[0], xs_hbm.at[pl.ds(0, MOE_TOK), :], sem).wait()

    @pl.when(i == 0)
    def _():
        for j in range(min(2, nb)):
            load(j, j).start()

    load(i, slot).wait()

    def issue(t, c):
        for k in range(TOP_K):
            row_copy(slot, t, dest_ref[0, 0, t * TOP_K + k], rsem.at[i % 2]).start(priority=k % 2)
        return c

    lax.fori_loop(0, MOE_TOK, issue, 0)

    @pl.when(i == 0)
    def _():
        for e in range(N_EXPERTS):
            first = pst_ref[e] + cnt_ref[e]
            last = pst_ref[e + 1]
            lax.fori_loop(first, last, lambda r, c: (row_copy(0, 0, r, psem).start(), c)[1], 0)
            lax.fori_loop(first, last, lambda r, c: (row_copy(0, 0, r, psem).wait(), c)[1], 0)

        def block_copy(m):
            return pltpu.make_async_copy(hbuf.at[0], xs_hbm.at[pl.ds(m * MOE_TOK, MOE_TOK), :], psem)

        used = pst_ref[N_EXPERTS] // MOE_TOK
        total = xs_hbm.shape[0] // MOE_TOK
        lax.fori_loop(used, total, lambda m, c: (block_copy(m).start(), c)[1], 0)
        lax.fori_loop(used, total, lambda m, c: (block_copy(m).wait(), c)[1], 0)

    @pl.when(i > 0)
    def _():
        wait_rows(rsem.at[(i - 1) % 2])

    @pl.when(i + 2 < nb)
    def _():
        load(i + 2, (i + 2) % 3).start()

    @pl.when(i == nb - 1)
    def _():
        wait_rows(rsem.at[i % 2])


def _dispatch(h, dest, counts, pad_start, n_rows):
    t = h.shape[0]
    nb = t // MOE_TOK
    return pl.pallas_call(
        functools.partial(_dispatch_kernel, nb=nb),
        grid_spec=pltpu.PrefetchScalarGridSpec(
            num_scalar_prefetch=2,
            grid=(nb,),
            in_specs=[
                pl.BlockSpec((1, 1, MOE_TOK * TOP_K), lambda i, c, p: (i, 0, 0), memory_space=pltpu.SMEM),
                pl.BlockSpec(memory_space=pl.ANY),
            ],
            out_specs=pl.BlockSpec(memory_space=pl.ANY),
            scratch_shapes=[pltpu.VMEM((3, MOE_TOK, h.shape[1]), h.dtype), pltpu.SemaphoreType.DMA((3,)),
                            pltpu.SemaphoreType.DMA((2,)), pltpu.SemaphoreType.DMA(())],
        ),
        out_shape=jax.ShapeDtypeStruct((n_rows, h.shape[1]), h.dtype),
        compiler_params=_cparams(("arbitrary",)),
        name="moe_dispatch",
    )(counts, pad_start, dest.reshape(nb, 1, MOE_TOK * TOP_K), h)


def _expert_switch(be_ref, nu_ref):
    m = pl.program_id(1)
    live = m < nu_ref[0]
    fresh = jnp.logical_or(m == 0, be_ref[m] != be_ref[jnp.maximum(m - 1, 0)])
    return live, jnp.logical_and(live, fresh)


def _gmm1_kernel(be_ref, nu_ref, x_ref, w_ref, p_ref, bg_ref, bl_ref, o_ref, wg_scr, wl_scr):
    live, fresh = _expert_switch(be_ref, nu_ref)
    half = MXU_N // 2

    @pl.when(fresh)
    def _():
        for c in range(w_ref.shape[3] // MXU_N):
            blk = w_ref[0, 0, :, c * MXU_N:(c + 1) * MXU_N].astype(BF16)
            sp = jnp.dot(blk, p_ref[...], preferred_element_type=F32)
            wg_scr[:, c * half:(c + 1) * half] = sp[:, :half].astype(BF16)
            wl_scr[:, c * half:(c + 1) * half] = sp[:, half:].astype(BF16)

    @pl.when(live)
    def _():
        x_lo, x_hi = _unpack_bf16_pairs(x_ref[...])
        kh = x_lo.shape[1]
        g = (jnp.dot(x_lo, wg_scr[:kh, :], preferred_element_type=F32)
             + jnp.dot(x_hi, wg_scr[kh:, :], preferred_element_type=F32) + bg_ref[0])
        u = (jnp.dot(x_lo, wl_scr[:kh, :], preferred_element_type=F32)
             + jnp.dot(x_hi, wl_scr[kh:, :], preferred_element_type=F32) + bl_ref[0])
        g = jnp.minimum(g, SWIGLU_LIMIT)
        u = jnp.clip(u, -SWIGLU_LIMIT, SWIGLU_LIMIT)
        o_ref[...] = (g * jax.nn.sigmoid(SWIGLU_ALPHA * g) * (u + 1.0)).astype(o_ref.dtype)

    @pl.when(jnp.logical_not(live))
    def _():
        o_ref[...] = jnp.zeros(o_ref.shape, o_ref.dtype)


def _gmm2_kernel(be_ref, nu_ref, x_ref, w_ref, b_ref, o_ref, w_scr):
    live, fresh = _expert_switch(be_ref, nu_ref)

    @pl.when(fresh)
    def _():
        w_scr[...] = w_ref[0, 0].astype(BF16)

    @pl.when(live)
    def _():
        o_ref[...] = _pack_bf16_pairs(jnp.dot(x_ref[...], w_scr[...], preferred_element_type=F32) + b_ref[0])

    @pl.when(jnp.logical_not(live))
    def _():
        o_ref[...] = jnp.zeros(o_ref.shape, o_ref.dtype)


def _experts(xs, block_e, n_used, layer, w_gu, bg, bl, w_dn, bd, tn=1024, tn2=MOE_DOWN_TN):
    n_rows = xs.shape[0]
    nb = n_rows // MOE_TM
    idx = jnp.arange(MXU_N)
    perm = (idx[:, None] == jnp.where(idx < MXU_N // 2, 2 * idx, 2 * (idx - MXU_N // 2) + 1)[None, :]).astype(BF16)

    def mrow(n, m, be, nu):
        return (jnp.minimum(m, nu[0] - 1), 0)

    def bmap(n, m, be, nu):
        return (be[jnp.minimum(m, nu[0] - 1)], 0, n)

    def wmap(n, m, be, nu):
        return (layer, be[jnp.minimum(m, nu[0] - 1)], 0, n)

    hid = pl.pallas_call(
        _gmm1_kernel,
        grid_spec=pltpu.PrefetchScalarGridSpec(
            num_scalar_prefetch=2,
            grid=(D_EXPERT // tn, nb),
            in_specs=[
                pl.BlockSpec((MOE_TM, D_MODEL // 2), mrow),
                pl.BlockSpec((1, 1, D_MODEL, 2 * tn), wmap),
                pl.BlockSpec((MXU_N, MXU_N), lambda n, m, be, nu: (0, 0)),
                pl.BlockSpec((1, 1, tn), bmap),
                pl.BlockSpec((1, 1, tn), bmap),
            ],
            out_specs=pl.BlockSpec((MOE_TM, tn), lambda n, m, be, nu: (m, n)),
            scratch_shapes=[pltpu.VMEM((D_MODEL, tn), BF16), pltpu.VMEM((D_MODEL, tn), BF16)],
        ),
        out_shape=jax.ShapeDtypeStruct((n_rows, D_EXPERT), BF16),
        compiler_params=_cparams(("parallel", "arbitrary"), VMEM_LIMIT_EXPERTS),
        name="moe_gate_up",
    )(block_e, n_used, xs, w_gu, perm, bg, bl)
    return pl.pallas_call(
        _gmm2_kernel,
        grid_spec=pltpu.PrefetchScalarGridSpec(
            num_scalar_prefetch=2,
            grid=(D_MODEL // tn2, nb),
            in_specs=[
                pl.BlockSpec((MOE_TM, D_EXPERT), mrow),
                pl.BlockSpec((1, 1, D_EXPERT, tn2), wmap),
                pl.BlockSpec((1, 1, tn2), bmap),
            ],
            out_specs=pl.BlockSpec((MOE_TM, tn2 // 2), lambda n, m, be, nu: (m, n)),
            scratch_shapes=[pltpu.VMEM((D_EXPERT, tn2), BF16)],
        ),
        out_shape=jax.ShapeDtypeStruct((n_rows, D_MODEL // 2), U32),
        compiler_params=_cparams(("parallel", "arbitrary"), VMEM_LIMIT_EXPERTS),
        name="moe_down",
    )(block_e, n_used, hid, w_dn, bd)


def _combine_kernel(dest_ref, next_ref, gate_ref, y_hbm, x_ref, m_ref, g_ref, b_ref, *rest, with_next):
    if with_next:
        m2_ref, o_ref, h_ref, buf, sems = rest
    else:
        o_ref, buf, sems = rest
    i = pl.program_id(0)
    slot = i % 2

    def gather(idx_ref, s):
        def issue(t, c):
            for k in range(TOP_K):
                pltpu.make_async_copy(y_hbm.at[pl.ds(idx_ref[0, 0, t * TOP_K + k], 1), :],
                                      buf.at[s, k, pl.ds(t, 1), :], sems.at[s]).start(priority=k % 2)
            return c

        lax.fori_loop(0, MOE_TOK, issue, 0)

    @pl.when(i == 0)
    def _():
        gather(dest_ref, 0)

    @pl.when(i + 1 < pl.num_programs(0))
    def _():
        gather(next_ref, 1 - slot)

    for k in range(TOP_K):
        pltpu.make_async_copy(y_hbm.at[pl.ds(0, MOE_TOK), :], buf.at[slot, k], sems.at[slot]).wait()
    gate = gate_ref[...]
    y = None
    for k in range(TOP_K):
        lo, hi = _unpack_bf16_pairs(buf[slot, k])
        hw = MOE_DOWN_TN // 2
        parts = []
        for c in range(lo.shape[1] // hw):
            parts += [lo[:, c * hw:(c + 1) * hw], hi[:, c * hw:(c + 1) * hw]]
        yk = gate[:, k:k + 1] * jnp.concatenate(parts, axis=1).astype(F32)
        y = yk if y is None else y + yk
    m = m_ref[0, 0]
    z = DEEPNORM_ALPHA * x_ref[...] + m[5:6] * y
    xn = _ln(z) * g_ref[...] + b_ref[...]
    o_ref[...] = xn
    if with_next:
        m2 = m2_ref[0, 0]
        h_ref[...] = (_ln(xn) * (1.0 + m2[1:2]) + m2[0:1]).astype(h_ref.dtype)


def _combine_pn(y_rows, dest, gate, x, modt, ln_g, ln_b, mod_map, next_modt=None):
    t = x.shape[0]
    nb = t // MOE_TOK
    dest3 = dest.reshape(nb, 1, MOE_TOK * TOP_K)
    row = pl.BlockSpec((MOE_TOK, D_MODEL), lambda i: (i, 0))
    vec = pl.BlockSpec((1, D_MODEL), lambda i: (0, 0))
    mod = pl.BlockSpec((1, 1, 8, D_MODEL), lambda i: mod_map(i) + (0, 0))
    with_next = next_modt is not None
    out_shape = jax.ShapeDtypeStruct((t, D_MODEL), F32)
    return pl.pallas_call(
        functools.partial(_combine_kernel, with_next=with_next),
        grid=(nb,),
        in_specs=[
            pl.BlockSpec((1, 1, MOE_TOK * TOP_K), lambda i: (i, 0, 0), memory_space=pltpu.SMEM),
            pl.BlockSpec((1, 1, MOE_TOK * TOP_K), lambda i: (jnp.minimum(i + 1, nb - 1), 0, 0),
                         memory_space=pltpu.SMEM),
            pl.BlockSpec((MOE_TOK, LANE), lambda i: (i, 0)),
            pl.BlockSpec(memory_space=pl.ANY),
            row,
            mod,
            vec, vec,
        ] + ([mod] if with_next else []),
        out_specs=[row, row] if with_next else row,
        out_shape=[out_shape, jax.ShapeDtypeStruct((t, D_MODEL), BF16)] if with_next else out_shape,
        scratch_shapes=[pltpu.VMEM((2, TOP_K, MOE_TOK, D_MODEL // 2), U32), pltpu.SemaphoreType.DMA((2,))],
        compiler_params=_cparams(("arbitrary",)),
        name="moe_combine",
    )(dest3, dest3, gate, y_rows, x, modt, ln_g.reshape(1, D_MODEL), ln_b.reshape(1, D_MODEL),
      *([next_modt] if with_next else []))


def _rank_kernel(te_ref, tri_ref, rank_ref, cnt_ref, run_ref):
    @pl.when(pl.program_id(0) == 0)
    def _():
        run_ref[...] = jnp.zeros(run_ref.shape, F32)

    te = te_ref[...]
    lane = lax.broadcasted_iota(I32, te.shape, 1)
    base = run_ref[...]
    rank = jnp.zeros(te.shape, F32)
    for k in range(TOP_K):
        hit = te[:, k:k + 1] == lane
        onehot = jnp.where(hit, 1.0, 0.0).astype(BF16)
        before = jnp.dot(tri_ref[...], onehot, preferred_element_type=F32)
        rk = jnp.sum(jnp.where(hit, before + base, 0.0), axis=-1, keepdims=True)
        rank = jnp.where(lane == k, rk, rank)
        base = base + jnp.sum(onehot.astype(F32), axis=0, keepdims=True)
    run_ref[...] = base
    rank_ref[...] = rank.astype(I32)
    cnt_ref[...] = base.astype(I32)


def _ranks(top_e):
    t = top_e.shape[0]
    tri = (jnp.arange(MOE_TOK)[:, None] > jnp.arange(MOE_TOK)[None, :]).astype(BF16)
    return pl.pallas_call(
        _rank_kernel,
        grid=(t // MOE_TOK,),
        in_specs=[pl.BlockSpec((MOE_TOK, LANE), lambda i: (i, 0)), pl.BlockSpec((MOE_TOK, MOE_TOK), lambda i: (0, 0))],
        out_specs=[pl.BlockSpec((MOE_TOK, LANE), lambda i: (i, 0)), pl.BlockSpec((1, LANE), lambda i: (0, 0))],
        out_shape=[jax.ShapeDtypeStruct((t, LANE), I32), jax.ShapeDtypeStruct((1, LANE), I32)],
        scratch_shapes=[pltpu.VMEM((1, LANE), F32)],
        compiler_params=_cparams(("arbitrary",)),
        name="moe_rank",
    )(top_e, tri)


def _moe(h, top_e, gate, x, modt, mod_map, ln_g, ln_b, layer, w_gu, b_gu, w_dn, b_dn, next_modt=None):
    t = h.shape[0]
    n_assign = t * TOP_K
    nb = n_assign // MOE_TM + N_EXPERTS
    rank, cnt = _ranks(top_e)
    counts = cnt[0, :N_EXPERTS]
    padded = (counts + MOE_TM - 1) // MOE_TM * MOE_TM
    pad_end = jnp.cumsum(padded)
    pad_start = jnp.concatenate([jnp.zeros((1,), I32), pad_end]).astype(I32)
    te4 = top_e[:, :TOP_K]
    hit = te4[:, :, None] == jnp.arange(N_EXPERTS, dtype=I32)[None, None, :]
    dest = (jnp.sum(jnp.where(hit, pad_start[None, None, :N_EXPERTS], 0), axis=-1) + rank[:, :TOP_K]).astype(I32)
    starts = jnp.arange(nb, dtype=I32) * MOE_TM
    block_e = jnp.minimum(jnp.sum((pad_end[None, :] <= starts[:, None]).astype(I32), axis=1), N_EXPERTS - 1)
    n_used = (pad_end[-1:] // MOE_TM).astype(I32)

    xs = _dispatch(h, dest, counts.astype(I32), pad_start, nb * MOE_TM)
    bg = b_gu[layer, :, None, 0::2]
    bl = b_gu[layer, :, None, 1::2]
    y_rows = _experts(xs, block_e, n_used, layer, w_gu, bg, bl, w_dn, b_dn[layer, :, None, :])
    return _combine_pn(y_rows, dest, gate, x, modt, ln_g, ln_b, mod_map, next_modt)


def _rope_tables(n_ctx, seq):
    pos = jnp.arange(seq)
    row = (pos // GRID_W).astype(F32)
    col = (pos % GRID_W).astype(F32)
    n_freq = MLA_ROPE // 4
    freqs = ROPE_BASE ** (-jnp.arange(n_freq, dtype=F32) / n_freq)
    ar = row[:, None] * freqs
    ac = col[:, None] * freqs
    cos = jnp.concatenate([jnp.cos(ar), jnp.cos(ar), jnp.cos(ac), jnp.cos(ac)], axis=-1)
    sin = jnp.concatenate([jnp.sin(ar), jnp.sin(ar), jnp.sin(ac), jnp.sin(ac)], axis=-1)
    cos = jnp.concatenate([jnp.ones((n_ctx, MLA_ROPE), F32), cos], axis=0)
    sin = jnp.concatenate([jnp.zeros((n_ctx, MLA_ROPE), F32), sin], axis=0)
    return cos, sin


def _rot_cols(w):
    q = MLA_ROPE // 4
    a, b, c, d = w[..., :q], w[..., q:2 * q], w[..., 2 * q:3 * q], w[..., 3 * q:]
    return jnp.concatenate([-b, a, -d, c], axis=-1)


def _rmsnorm_epilogue(accs, ex):
    x = accs[0]
    return x * lax.rsqrt(jnp.mean(x * x, axis=-1, keepdims=True) + RMS_EPS) * ex[0]


def _hgrn_layer(xs, modt, mod_map, w_in, gnorm, w_o, lbs, bsz, l, n_ctx):
    h = _ln_mod(xs, modt, mod_map, 0)
    wb = w_in.astype(BF16)
    silu_ep = lambda accs, ex: _silu(accs[0])
    q = _mm(h, [wb], HG_W, MM_TN, BF16, silu_ep, w_col0=0, name="hg_q")
    fr = _mm(h, [wb], 2 * HG_W, MM_TN, F32, w_col0=HG_W, name="hg_f")
    v = _mm(h, [wb], HG_W, MM_TN, BF16, w_col0=3 * HG_W, name="hg_v")
    g = _mm(h, [wb], HG_W, MM_TN, BF16, silu_ep, w_col0=4 * HG_W, name="hg_g")
    r3 = lambda a: a.reshape(bsz, l, a.shape[-1])
    o_f = _hgrn_scan(r3(q), r3(fr), r3(v), lbs[0], False, n_ctx)
    o = _hgrn_scan(r3(q), r3(fr), r3(v), lbs[1], True, n_ctx, o_fwd=o_f, g=r3(g), gnorm=gnorm)
    return _mm(o.reshape(bsz * l, HG_W), [w_o.astype(BF16)], D_MODEL, MM_TN, BF16, name="hg_o")


def _mla_layer(h, w_in, q_norm, kv_norm, w_uq, w_ukv, w_o, bsz, l, n_ctx):
    wb = w_in.astype(BF16)
    nl = l // ROW_TILE
    vec = lambda n: ((1, n), lambda j, i: (0, 0))
    cq = _mm(h, [wb], MLA_Q_RANK, 512, BF16, _rmsnorm_epilogue,
             extras=[(q_norm.reshape(1, -1),) + vec(MLA_Q_RANK)], tm=ROW_TILE, w_col0=0, name="mla_cq")
    ckv = _mm(h, [wb], MLA_KV_RANK, 512, BF16, _rmsnorm_epilogue,
              extras=[(kv_norm.reshape(1, -1),) + vec(MLA_KV_RANK)], tm=ROW_TILE, w_col0=MLA_Q_RANK, name="mla_ckv")
    cos, sin = _rope_tables(n_ctx, l - n_ctx)
    rope_ep = lambda accs, ex: accs[0] * ex[0] + accs[1] * ex[1]
    tab = lambda n: ((ROW_TILE, n), lambda j, i: (i % nl, 0))
    ns, nc = (l - n_ctx) // ROW_TILE, n_ctx // ROW_TILE
    qtab = lambda n: ((ROW_TILE, n), lambda j, i: (nc + i % ns, 0))
    lat_rows = (bsz * (l - n_ctx), lambda i: (i // ns) * nl + nc + i % ns)
    w_kp = w_in[:, MLA_Q_RANK + MLA_KV_RANK:]
    zk = jnp.zeros((D_MODEL, LANE - MLA_ROPE), F32)
    zt = jnp.zeros((l, LANE - MLA_ROPE), F32)
    kp = _mm(h, [jnp.concatenate([w_kp, zk], 1).astype(BF16), jnp.concatenate([_rot_cols(w_kp), zk], 1).astype(BF16)],
             LANE, LANE, BF16, rope_ep,
             extras=[(jnp.concatenate([cos, zt], 1),) + tab(LANE), (jnp.concatenate([sin, zt], 1),) + tab(LANE)],
             tm=ROW_TILE, name="mla_kp")
    wq = w_uq.reshape(MLA_Q_RANK, MLA_HEADS, MLA_NOPE + MLA_ROPE)
    zq = jnp.zeros((MLA_Q_RANK, MLA_HEADS, MLA_QK_PAD - MLA_NOPE - MLA_ROPE), F32)
    wqa = jnp.concatenate([wq, zq], -1).reshape(MLA_Q_RANK, -1).astype(BF16)
    wqb = jnp.concatenate([jnp.zeros_like(wq[..., :MLA_NOPE]), _rot_cols(wq[..., MLA_NOPE:]), zq], -1)
    wqb = wqb.reshape(MLA_Q_RANK, -1).astype(BF16)
    zt = jnp.zeros((l, MLA_QK_PAD - MLA_NOPE - MLA_ROPE), F32)
    q_scale = MLA_SCALE * LOG2_E
    cq_tab = jnp.concatenate([jnp.ones((l, MLA_NOPE), F32), cos, zt], 1) * q_scale
    sq_tab = jnp.concatenate([jnp.zeros((l, MLA_NOPE), F32), sin, zt], 1) * q_scale
    hq = 8
    qf = _mm(cq, [wqa, wqb], MLA_HEADS * MLA_QK_PAD, hq * MLA_QK_PAD, BF16, rope_ep,
             extras=[(jnp.tile(cq_tab, (1, hq)),) + qtab(hq * MLA_QK_PAD),
                     (jnp.tile(sq_tab, (1, hq)),) + qtab(hq * MLA_QK_PAD)], tm=ROW_TILE, rows=lat_rows,
             name="mla_q")
    kv = _mm(ckv, [w_ukv.astype(BF16)], MLA_HEADS * (MLA_NOPE + MLA_V), MM_TN, BF16, name="mla_kv")
    r3 = lambda a: a.reshape(bsz, l, a.shape[-1])
    o = _attention(qf.reshape(bsz, l - n_ctx, -1), r3(kv), r3(kp))
    return _mm(o.reshape(bsz * (l - n_ctx), MLA_HEADS * MLA_V), [w_o.astype(BF16)], D_MODEL, MM_TN, BF16, name="mla_o")


def kernel(x, c, ctx, c_ctx, ada_w, ada_b, ln_g, ln_b, hg_w_in, hg_gnorm, hg_lb_logits, hg_w_o, mla_w_in,
           mla_q_norm, mla_kv_norm, mla_w_uq, mla_w_ukv, mla_w_o, router_w, router_b, exp_w_gu, exp_b_gu,
           exp_w_dn, exp_b_dn):
    bsz, seq, d = x.shape
    n_ctx = ctx.shape[1]
    l = n_ctx + seq
    nl = l // ROW_TILE
    ns = seq // ROW_TILE
    nc = n_ctx // ROW_TILE

    cc = jnp.zeros((8, d), F32).at[:bsz].set(c).at[bsz].set(c_ctx)
    mods = _ada(cc, ada_w, ada_b).reshape(DEPTH, 8, N_MOD, d)
    pad = jnp.zeros((DEPTH, bsz, 8 - N_MOD, d), F32)
    m_lat = jnp.concatenate([mods[:, :bsz], pad], axis=2)
    m_ctx = jnp.concatenate([jnp.broadcast_to(mods[:, bsz:bsz + 1], (DEPTH, bsz, N_MOD, d)), pad], axis=2)
    modt = jnp.stack([m_ctx, m_lat], axis=2)
    lower = jnp.cumsum(jax.nn.softmax(hg_lb_logits.astype(F32), axis=0), axis=0)

    xs = jnp.concatenate([ctx, x], axis=1).reshape(bsz * l, d)

    def map_all(r):
        per = l // r
        return lambda i: (i // per, ((i % per) >= n_ctx // r).astype(I32))

    def map_lat(r):
        per = seq // r
        return lambda i: (i // per, 1)

    y = _hgrn_layer(xs, modt[0], map_all(ROW_TILE), hg_w_in[0], hg_gnorm[0], hg_w_o[0], lower[0], bsz, l, n_ctx)
    xs, h, te, tg = _pn_router(xs, y, modt[0], ln_g[0, 0], ln_b[0, 0], router_w[0], router_b[0], bsz * l,
                               lambda i: i, map_all(ROW_TILE))
    xs, h1 = _moe(h, te, tg, xs, modt[0], map_all(MOE_TOK), ln_g[0, 1], ln_b[0, 1],
                  0, exp_w_gu, exp_b_gu, exp_w_dn, exp_b_dn, next_modt=modt[1])

    y = _mla_layer(h1, mla_w_in[0], mla_q_norm[0], mla_kv_norm[0], mla_w_uq[0],
                   mla_w_ukv[0], mla_w_o[0], bsz, l, n_ctx)
    xl, h, te, tg = _pn_router(xs, y, modt[1], ln_g[1, 0], ln_b[1, 0], router_w[1], router_b[1], bsz * seq,
                               lambda i: (i // ns) * nl + nc + i % ns, map_lat(ROW_TILE))
    out = _moe(h, te, tg, xl, modt[1], map_lat(MOE_TOK), ln_g[1, 1], ln_b[1, 1],
               1, exp_w_gu, exp_b_gu, exp_w_dn, exp_b_dn)
    return out.reshape(bsz, seq, d)
```
